```python
import math
import jax, jax.numpy as jnp
from jax import lax
import numpy as np

D_MODEL = 1024
BATCH = 8
SEQ = 4096
DEPTH = 4

N_MIXERS = 4
HEADS = 4
GROUP_WIDTH = D_MODEL // N_MIXERS
HEAD_DIM = GROUP_WIDTH // HEADS
D_MIX = N_MIXERS * GROUP_WIDTH
DIFF_DIM = HEAD_DIM // 2
ROPE_DIM = DIFF_DIM // 4
ROPE_THETA = 500000.0
POS_OFFSET_MAX = 1024
CONV_WIDTH = 3
GLA_RANK = 16
GLA_TAU = 16.0
CHUNK = 64
Q_BLOCK = 128
N_EXPERTS = 16
N_EXPERT_GROUPS = 4
EXPERTS_PER_GROUP = N_EXPERTS // N_EXPERT_GROUPS
TOP_K = 2
D_EXPERT = D_MODEL // 2
ALPHA = (2 * DEPTH) ** 0.25
BETA = (8 * DEPTH) ** -0.25
EPS = 1e-6
GW = GROUP_WIDTH
IN_SPLITS = (GW, GW, GW, GW, HEADS, HEADS, GW, GW, GW, GW, GW, GW, GW, GW, GW, GW, GLA_RANK)
P_IN = sum(IN_SPLITS)

kernel_name = 'hybrid_parallel_heads_moe'


def layer_norm(x, g, b):
    xf = x.astype(jnp.float32)
    mu = xf.mean(-1, keepdims=True)
    var = jnp.square(xf - mu).mean(-1, keepdims=True)
    return ((xf - mu) * lax.rsqrt(var + EPS) * g + b).astype(x.dtype)


def rms_norm(x, g, out_dtype):
    xf = x.astype(jnp.float32)
    return (xf * lax.rsqrt(jnp.square(xf).mean(-1, keepdims=True) + EPS) * g).astype(out_dtype)


def rope_partial(x, cos, sin):
    half = ROPE_DIM // 2
    x1, x2, rest = x[..., :half], x[..., half:ROPE_DIM], x[..., ROPE_DIM:]
    return jnp.concatenate([x1 * cos - x2 * sin, x2 * cos + x1 * sin, rest], axis=-1)


def to_chunks(x):
    B, T = x.shape[:2]
    x = x.reshape((B, T // CHUNK, CHUNK) + x.shape[2:])
    return x.transpose((1, 0, 3, 2) + tuple(range(4, x.ndim)))


def from_chunks(x):
    x = x.transpose((1, 0, 3, 2) + tuple(range(4, x.ndim)))
    return x.reshape((x.shape[0], x.shape[1] * x.shape[2]) + x.shape[3:])


def mlstm_chunkwise(q, k, v, log_i, log_f):
    B, T, H, d = q.shape
    causal = jnp.tril(jnp.ones((CHUNK, CHUNK), dtype=bool))

    def step(carry, xs):
        C, n, m = carry
        qb, kb, vb, li, lf = xs
        b = jnp.cumsum(lf, axis=-1)
        dmat = jnp.where(causal, b[..., :, None] - b[..., None, :] + li[..., None, :], -jnp.inf)
        inter = b + m[..., None]
        m_row = jnp.maximum(inter, dmat.max(-1))
        s = jnp.einsum('bhjd,bhld->bhjl', qb, kb) * jnp.exp(dmat - m_row[..., None])
        w_int = jnp.exp(inter - m_row)
        num = jnp.einsum('bhjl,bhle->bhje', s, vb) + w_int[..., None] * jnp.einsum('bhjd,bhde->bhje', qb, C)
        den = s.sum(-1) + w_int * jnp.einsum('bhjd,bhd->bhj', qb, n)
        h = num / jnp.maximum(jnp.abs(den), jnp.exp(-m_row))[..., None]
        b_end = b[..., -1]
        dk = b_end[..., None] - b + li
        m_new = jnp.maximum(b_end + m, dk.max(-1))
        wl = jnp.exp(dk - m_new[..., None])
        decay = jnp.exp(b_end + m - m_new)
        C = decay[..., None, None] * C + jnp.einsum('bhl,bhld,bhle->bhde', wl, kb, vb)
        n = decay[..., None] * n + jnp.einsum('bhl,bhld->bhd', wl, kb)
        return (C, n, m_new), h

    init = (jnp.zeros((B, H, d, d), jnp.float32), jnp.zeros((B, H, d), jnp.float32),
            jnp.zeros((B, H), jnp.float32))
    xs = (to_chunks(q), to_chunks(k), to_chunks(v), to_chunks(log_i), to_chunks(log_f))
    _, h = lax.scan(step, init, xs)
    return from_chunks(h)


def gla_chunkwise(q, k, v, log_a):
    B, T, H, dk = q.shape
    dv = v.shape[-1]
    causal = jnp.tril(jnp.ones((CHUNK, CHUNK), dtype=bool))

    def step(S, xs):
        qb, kb, vb, lab = xs
        b = jnp.cumsum(lab, axis=2)
        diff = jnp.where(causal[:, :, None], b[:, :, :, None, :] - b[:, :, None, :, :], -jnp.inf)
        att = jnp.einsum('bhjd,bhld,bhjld->bhjl', qb, kb, jnp.exp(diff))
        o = jnp.einsum('bhjl,bhle->bhje', att, vb) + jnp.einsum('bhjd,bhde->bhje', qb * jnp.exp(b), S)
        b_end = b[:, :, -1]
        S = jnp.exp(b_end)[..., None] * S + jnp.einsum('bhld,bhle->bhde', kb * jnp.exp(b_end[:, :, None, :] - b), vb)
        return S, o

    init = jnp.zeros((B, H, dk, dv), jnp.float32)
    _, o = lax.scan(step, init, (to_chunks(q), to_chunks(k), to_chunks(v), to_chunks(log_a)))
    return from_chunks(o)


def causal_short_conv(s, w, bias):
    out = lax.conv_general_dilated(s, w[:, None, :].astype(s.dtype), window_strides=(1,),
                                   padding=((CONV_WIDTH - 1, 0),),
                                   dimension_numbers=('NWC', 'WIO', 'NWC'),
                                   feature_group_count=s.shape[-1])
    return out + bias


def diff_attention(q, k, v, lam, lam_init, norm_g, cos, sin):
    B, T, H = q.shape[:3]
    qh = rope_partial(q, cos, sin).transpose(0, 2, 3, 1, 4)
    kh = rope_partial(k, cos, sin).transpose(0, 2, 3, 1, 4)
    vh = v.transpose(0, 2, 1, 3)
    key_pos = jnp.arange(T)
    scale = DIFF_DIM ** -0.5

    def block(i):
        start = i * Q_BLOCK
        qb = lax.dynamic_slice_in_dim(qh, start, Q_BLOCK, axis=3)
        s = jnp.einsum('bhcqd,bhckd->bhcqk', qb, kh).astype(jnp.float32) * scale
        q_pos = start + jnp.arange(Q_BLOCK)
        s = jnp.where(key_pos[None, :] <= q_pos[:, None], s, -jnp.inf)
        p = jax.nn.softmax(s, axis=-1)
        a = p[:, :, 0] - lam * p[:, :, 1]
        return jnp.einsum('bhqk,bhkd->bhqd', a.astype(vh.dtype), vh)

    o = lax.map(block, jnp.arange(T // Q_BLOCK))
    o = o.transpose(1, 0, 3, 2, 4).reshape(B, T, H, HEAD_DIM)
    o = rms_norm(o, norm_g, v.dtype) * (1.0 - lam_init)
    return o.reshape(B, T, H * HEAD_DIM)


def hybrid_mixer(u, w_in, w_out, b_i, b_f, mlstm_g, conv_w, conv_b, lq1, lk1, lq2, lk2, diff_g, lam_init,
                 w_a2, b_a, gla_g, cos, sin):
    B, T, _ = u.shape
    f32 = jnp.float32
    split_points = [int(v) for v in np.cumsum(IN_SPLITS)[:-1]]
    (mq, mk, mv, mo, mi, mf, cb, cc, ch, dq, dk, dv, gq, gk, gv, gr, ga) = jnp.split(u @ w_in, split_points, axis=-1)
    heads = lambda t: t.reshape(B, T, HEADS, HEAD_DIM)
    h_a = mlstm_chunkwise(heads(mq).astype(f32) * HEAD_DIM ** -0.5, heads(mk).astype(f32), heads(mv).astype(f32),
                          (mi + b_i).astype(f32), jax.nn.log_sigmoid((mf + b_f).astype(f32)))
    y_a = jax.nn.sigmoid(mo) * rms_norm(h_a, mlstm_g, u.dtype).reshape(B, T, GW)
    y_b = cb * causal_short_conv(cc * ch, conv_w, conv_b)
    lam = jnp.exp(jnp.sum(lq1 * lk1)) - jnp.exp(jnp.sum(lq2 * lk2)) + lam_init
    y_c = diff_attention(dq.reshape(B, T, HEADS, 2, DIFF_DIM), dk.reshape(B, T, HEADS, 2, DIFF_DIM), heads(dv),
                         lam, lam_init, diff_g, cos, sin)
    log_a = jax.nn.log_sigmoid((ga @ w_a2 + b_a).astype(f32)) / GLA_TAU
    o_d = gla_chunkwise(heads(gq).astype(f32) * HEAD_DIM ** -0.5, heads(gk).astype(f32), heads(gv).astype(f32),
                        heads(log_a))
    y_d = rms_norm(o_d, gla_g, u.dtype).reshape(B, T, GW) * jax.nn.silu(gr)
    return jnp.concatenate([y_a, y_b, y_c, y_d], axis=-1) @ w_out


def moe(u, router_w, router_bias, w_gate, w_up, w_down):
    B, T, D = u.shape
    t = u.reshape(B * T, D)
    scores = jax.nn.sigmoid((t @ router_w).astype(jnp.float32))
    biased = scores + router_bias.astype(jnp.float32)
    grouped = biased.reshape(-1, N_EXPERT_GROUPS, EXPERTS_PER_GROUP)
    group_score = lax.top_k(grouped, TOP_K)[0].sum(-1)
    _, g_idx = lax.top_k(group_score, 1)
    in_group = (jnp.arange(N_EXPERTS)[None, :] // EXPERTS_PER_GROUP) == g_idx
    _, e_idx = lax.top_k(jnp.where(in_group, biased, -jnp.inf), TOP_K)
    sel = jnp.take_along_axis(scores, e_idx, axis=-1)
    gates = sel / sel.sum(-1, keepdims=True)
    combine = (jax.nn.one_hot(e_idx, N_EXPERTS, dtype=jnp.float32) * gates[..., None]).sum(1).astype(u.dtype)
    y = jnp.zeros_like(t)
    for e in range(N_EXPERTS):
        h = jax.nn.silu(t @ w_gate[e]) * (t @ w_up[e])
        y = y + combine[:, e:e + 1] * (h @ w_down[e])
    return y.reshape(B, T, D)


def setup_inputs(seed: int = 0) -> dict:
    key = jax.random.key(seed)
    ks = jax.random.split(key, 32)
    nrm = lambda k, shape, s: jax.random.normal(k, shape, jnp.float32) * s
    x = nrm(ks[0], (BATCH, SEQ, D_MODEL), 1.0)
    c = nrm(ks[1], (BATCH, D_MODEL), 1.0)
    offsets = jax.random.randint(ks[2], (BATCH, 1), 0, POS_OFFSET_MAX, dtype=jnp.int32)
    positions = offsets + jnp.arange(SEQ, dtype=jnp.int32)[None, :]
    return {
        'x': x,
        'c': c,
        'positions': positions,
        'w_ada': nrm(ks[3], (DEPTH, D_MODEL, 6 * D_MODEL), 0.1 * D_MODEL ** -0.5),
        'b_ada': nrm(ks[4], (DEPTH, 6 * D_MODEL), 0.01),
        'w_in': nrm(ks[5], (DEPTH, D_MODEL, P_IN), D_MODEL ** -0.5),
        'w_out': nrm(ks[6], (DEPTH, D_MIX, D_MODEL), BETA * D_MIX ** -0.5),
        'mlstm_b_i': nrm(ks[7], (DEPTH, HEADS), 0.1),
        'mlstm_b_f': jax.random.uniform(ks[8], (DEPTH, HEADS), jnp.float32, 3.0, 6.0),
        'mlstm_norm_g': 1.0 + nrm(ks[9], (DEPTH, HEAD_DIM), 0.01),
        'conv_w': nrm(ks[10], (DEPTH, CONV_WIDTH, GW), CONV_WIDTH ** -0.5),
        'conv_b': nrm(ks[11], (DEPTH, GW), 0.01),
        'diff_lambda_q1': nrm(ks[12], (DEPTH, DIFF_DIM), 0.1),
        'diff_lambda_k1': nrm(ks[13], (DEPTH, DIFF_DIM), 0.1),
        'diff_lambda_q2': nrm(ks[14], (DEPTH, DIFF_DIM), 0.1),
        'diff_lambda_k2': nrm(ks[15], (DEPTH, DIFF_DIM), 0.1),
        'diff_norm_g': 1.0 + nrm(ks[16], (DEPTH, HEAD_DIM), 0.01),
        'gla_w_a2': nrm(ks[17], (DEPTH, GLA_RANK, GW), GLA_RANK ** -0.5),
        'gla_b_a': nrm(ks[18], (DEPTH, GW), 0.5),
        'gla_norm_g': 1.0 + nrm(ks[19], (DEPTH, HEAD_DIM), 0.01),
        'ln1_g': 1.0 + nrm(ks[20], (DEPTH, D_MODEL), 0.01),
        'ln1_b': nrm(ks[21], (DEPTH, D_MODEL), 0.01),
        'ln2_g': 1.0 + nrm(ks[22], (DEPTH, D_MODEL), 0.01),
        'ln2_b': nrm(ks[23], (DEPTH, D_MODEL), 0.01),
        'router_w': nrm(ks[24], (D_MODEL, N_EXPERTS), D_MODEL ** -0.5),
        'router_bias': nrm(ks[25], (N_EXPERTS,), 0.01),
        'exp_w_gate': nrm(ks[26], (DEPTH, N_EXPERTS, D_MODEL, D_EXPERT), D_MODEL ** -0.5),
        'exp_w_up': nrm(ks[27], (DEPTH, N_EXPERTS, D_MODEL, D_EXPERT), D_MODEL ** -0.5),
        'exp_w_down': nrm(ks[28], (DEPTH, N_EXPERTS, D_EXPERT, D_MODEL), BETA * D_EXPERT ** -0.5),
    }


def reference(x, c, positions, w_ada, b_ada, w_in, w_out, mlstm_b_i, mlstm_b_f, mlstm_norm_g, conv_w, conv_b,
              diff_lambda_q1, diff_lambda_k1, diff_lambda_q2, diff_lambda_k2, diff_norm_g, gla_w_a2, gla_b_a,
              gla_norm_g, ln1_g, ln1_b, ln2_g, ln2_b, router_w, router_bias, exp_w_gate, exp_w_up, exp_w_down):
    f32 = jnp.float32
    inv_freq = ROPE_THETA ** (-jnp.arange(0, ROPE_DIM, 2, dtype=f32) / ROPE_DIM)
    ang = positions.astype(f32)[..., None] * inv_freq
    cos = jnp.cos(ang)[:, :, None, None, :].astype(x.dtype)
    sin = jnp.sin(ang)[:, :, None, None, :].astype(x.dtype)
    c_act = jax.nn.silu(c)
    for l in range(DEPTH):
        lam_init = 0.8 - 0.6 * math.exp(-0.3 * l)
        mod = c_act @ w_ada[l] + b_ada[l]
        sh1, sc1, g1, sh2, sc2, g2 = [m[:, None, :] for m in jnp.split(mod, 6, axis=-1)]
        u = x * (1.0 + sc1) + sh1
        y = hybrid_mixer(u, w_in[l], w_out[l], mlstm_b_i[l], mlstm_b_f[l], mlstm_norm_g[l], conv_w[l], conv_b[l],
                         diff_lambda_q1[l], diff_lambda_k1[l], diff_lambda_q2[l], diff_lambda_k2[l], diff_norm_g[l],
                         lam_init, gla_w_a2[l], gla_b_a[l], gla_norm_g[l], cos, sin)
        x = layer_norm(ALPHA * x + (1.0 + g1) * y, ln1_g[l], ln1_b[l])
        u = x * (1.0 + sc2) + sh2
        y = moe(u, router_w, router_bias, exp_w_gate[l], exp_w_up[l], exp_w_down[l])
        x = layer_norm(ALPHA * x + (1.0 + g2) * y, ln2_g[l], ln2_b[l])
    return x
```

```python
import functools
import math

import numpy as np
import jax
import jax.numpy as jnp
from jax import lax
from jax.experimental import pallas as pl
from jax.experimental.pallas import tpu as pltpu

F32 = jnp.float32
BF16 = jnp.bfloat16

D_MODEL = 1024
DEPTH = 4
HEADS = 4
GW = 256
HEAD_DIM = 64
DIFF_DIM = 32
ROPE_DIM = 8
ROPE_THETA = 500000.0
CONV_WIDTH = 3
GLA_RANK = 16
GLA_TAU = 16.0
CHUNK = 64
GLA_CHUNK = 32
N_EXPERTS = 16
N_GROUPS = 4
GROUP_SIZE = 4
D_EXPERT = 512
ALPHA = (2 * DEPTH) ** 0.25
EPS = 1e-6
LANES = 128
NEG_INF = float("-inf")

ZM, ZC, ZQ, ZK, ZV, ZG, ZS = 1024, 768, 256, 256, 512, 1024, 128
Z_OFFS = np.cumsum([0, ZM, ZC, ZQ, ZK, ZV, ZG, ZS])
Z_TOTAL = int(Z_OFFS[-1])

TM_PROJ = 512
TB_SCAN = 512
TQ_ATTN = 512
TM_MOE = 1024
VMEM_LIMIT = 56 * 1024 * 1024


def _cparams(sem):
    return pltpu.CompilerParams(dimension_semantics=sem, vmem_limit_bytes=VMEM_LIMIT)


def _log_sigmoid(x):
    return jnp.minimum(x, 0.0) - jnp.log(1.0 + jnp.exp(-jnp.abs(x)))


def _sigmoid(x):
    return 1.0 / (1.0 + jnp.exp(-x))


def _layer_norm(v, g, b):
    mu = jnp.mean(v, axis=-1, keepdims=True)
    d = v - mu
    var = jnp.mean(d * d, axis=-1, keepdims=True)
    return d * lax.rsqrt(var + EPS) * g + b


def _tril(n):
    r = lax.broadcasted_iota(jnp.int32, (n, n), 0)
    c = lax.broadcasted_iota(jnp.int32, (n, n), 1)
    return c <= r


def _mod_kernel(c_ref, w_ref, b_ref, o_ref):
    c = c_ref[...]
    c_act = (c * _sigmoid(c)).astype(BF16)
    o_ref[0, 0] = jnp.dot(c_act, w_ref[0].astype(BF16), preferred_element_type=F32) + b_ref[0, 0]


def _mod_call(c, w_ada, b_ada):
    depth, d, _ = w_ada.shape
    bsz = c.shape[0]
    b3 = b_ada.reshape(depth, 6, 1, d)
    out = pl.pallas_call(
        _mod_kernel,
        grid=(depth, 6),
        in_specs=[pl.BlockSpec((bsz, d), lambda l, j: (0, 0)),
                  pl.BlockSpec((1, d, d), lambda l, j: (l, 0, j)),
                  pl.BlockSpec((1, 1, 1, d), lambda l, j: (l, j, 0, 0))],
        out_specs=pl.BlockSpec((1, 1, bsz, d), lambda l, j: (l, j, 0, 0)),
        out_shape=jax.ShapeDtypeStruct((depth, 6, bsz, d), F32),
        compiler_params=_cparams(("arbitrary", "arbitrary")),
        name="adaln_mod",
    )(c, w_ada, b3)
    return out.transpose(0, 2, 1, 3)


def _inproj_kernel(x_ref, mod_ref, w_ref, cos_ref, sin_ref, zm_ref, zc_ref, zq_ref, zk_ref, zv_ref, zg_ref,
                   zs_ref):
    mod = mod_ref[0]
    u = (x_ref[...] * (1.0 + mod[1:2, :]) + mod[0:1, :]).astype(BF16)

    def proj(idx):
        return jnp.dot(u, w_ref[:, int(Z_OFFS[idx]):int(Z_OFFS[idx + 1])], preferred_element_type=F32)

    zm_ref[...] = proj(0).astype(BF16)
    zc_ref[...] = proj(1).astype(BF16)

    lane = lax.broadcasted_iota(jnp.int32, (1, ZQ), 1)
    first_half = (lane % ROPE_DIM) < (ROPE_DIM // 2)
    cosf = cos_ref[...]
    sinf = sin_ref[...]

    def rope(r):
        swapped = jnp.where(first_half, pltpu.roll(r, ZQ - ROPE_DIM // 2, 1), pltpu.roll(r, ROPE_DIM // 2, 1))
        return r * cosf + swapped * sinf

    zq_ref[...] = (rope(proj(2)) * (DIFF_DIM ** -0.5)).astype(BF16)
    zk_ref[...] = rope(proj(3)).astype(BF16)
    lane_v = lax.broadcasted_iota(jnp.int32, (1, ZV), 1)
    ones_col = jnp.where((lane_v % LANES) == HEAD_DIM, 1.0, 0.0)
    zv_ref[...] = (proj(4) + ones_col).astype(BF16)
    zg_ref[...] = proj(5).astype(BF16)
    zs_ref[...] = proj(6)


def _inproj_call(x2, mod_l, w_l, cosf, sinf, seq):
    n, d = x2.shape
    tm = TM_PROJ
    tps = seq // tm
    row = lambda w: pl.BlockSpec((tm, w), lambda i: (i, 0))
    outs = [(ZM, BF16), (ZC, BF16), (ZQ, BF16), (ZK, BF16), (ZV, BF16), (ZG, BF16), (ZS, F32)]
    return pl.pallas_call(
        _inproj_kernel,
        grid=(n // tm,),
        in_specs=[row(d),
                  pl.BlockSpec((1, 6, d), lambda i: (i // tps, 0, 0)),
                  pl.BlockSpec((d, Z_TOTAL), lambda i: (0, 0)),
                  row(ZQ), row(ZQ)],
        out_specs=[row(w) for w, _ in outs],
        out_shape=[jax.ShapeDtypeStruct((n, w), dt) for w, dt in outs],
        compiler_params=_cparams(("parallel",)),
        name="in_proj",
    )(x2, mod_l, w_l, cosf, sinf)


def _mlstm_kernel(zm_ref, zs_ref, bif_ref, g_ref, o_ref, c_scr, m_scr):
    @pl.when(pl.program_id(1) == 0)
    def _():
        c_scr[...] = jnp.zeros_like(c_scr)
        m_scr[...] = jnp.zeros_like(m_scr)

    n_chunks = zm_ref.shape[0] // CHUNK
    causal = _tril(CHUNK)
    tril_f = causal.astype(F32)
    lane = lax.broadcasted_iota(jnp.int32, (1, LANES), 1)
    ones_col = jnp.where(lax.broadcasted_iota(jnp.int32, (1, HEAD_DIM), 1) == 0, 1.0, 0.0).astype(BF16)
    ones_blk = jnp.broadcast_to(ones_col, (CHUNK, HEAD_DIM))
    bif = bif_ref[...]
    gain = g_ref[...]

    def chunk_body(ci, carry):
        r0 = pl.multiple_of(ci * CHUNK, CHUNK)
        graw = zs_ref[pl.ds(r0, CHUNK), :] + bif
        gp = jnp.where(lane < HEADS, graw, jnp.where(lane < 2 * HEADS, _log_sigmoid(graw), 0.0))
        cum = jnp.dot(tril_f, gp, preferred_element_type=F32, precision=lax.Precision.HIGHEST)
        gp_t = gp.T
        cum_t = cum.T
        blk = zm_ref[pl.ds(r0, CHUNK), :]
        outs = []
        for h in range(HEADS):
            c0 = h * HEAD_DIM
            qh = blk[:, c0:c0 + HEAD_DIM] * (HEAD_DIM ** -0.5)
            kh = blk[:, GW + c0:GW + c0 + HEAD_DIM]
            vh = blk[:, 2 * GW + c0:2 * GW + c0 + HEAD_DIM]
            oh = blk[:, 3 * GW + c0:3 * GW + c0 + HEAD_DIM].astype(F32)
            vaug = jnp.concatenate([vh, ones_blk], axis=1)
            b_col = cum[:, HEADS + h:HEADS + h + 1]
            li_col = gp[:, h:h + 1]
            b_row = cum_t[HEADS + h:HEADS + h + 1, :]
            li_row = gp_t[h:h + 1, :]
            m_prev = m_scr[h][:, 0:1]
            dmat = jnp.where(causal, b_col - b_row + li_row, NEG_INF)
            inter = b_col + m_prev
            m_row = jnp.maximum(inter, jnp.max(dmat, axis=-1, keepdims=True))
            s = lax.dot_general(qh, kh, (((1,), (1,)), ((), ())), preferred_element_type=F32)
            s = s * jnp.exp(dmat - m_row)
            w_int = jnp.exp(inter - m_row)
            c_aug = c_scr[h]
            num = (jnp.dot(s.astype(BF16), vaug, preferred_element_type=F32)
                   + w_int * jnp.dot(qh, c_aug.astype(BF16), preferred_element_type=F32))
            den = num[:, HEAD_DIM:HEAD_DIM + 1]
            hv = num[:, :HEAD_DIM] / jnp.maximum(jnp.abs(den), jnp.exp(-m_row))
            hn = hv * lax.rsqrt(jnp.mean(hv * hv, axis=-1, keepdims=True) + EPS) * gain
            outs.append(_sigmoid(oh) * hn)
            b_end = b_col[CHUNK - 1:CHUNK, :]
            dk = b_end - b_col + li_col
            m_new = jnp.maximum(b_end + m_prev, jnp.max(dk, axis=0, keepdims=True))
            wl = jnp.exp(dk - m_new)
            decay = jnp.exp(b_end + m_prev - m_new)
            kw = (kh.astype(F32) * wl).astype(BF16)
            c_scr[h] = decay * c_aug + lax.dot_general(kw, vaug, (((0,), (0,)), ((), ())),
                                                       preferred_element_type=F32)
            m_scr[h] = jnp.broadcast_to(m_new, (1, LANES))
        o_ref[pl.ds(r0, CHUNK), :] = jnp.concatenate(outs, axis=1).astype(BF16)
        return carry

    lax.fori_loop(0, n_chunks, chunk_body, 0)


def _mlstm_call(zm, zs, bif, gain, bsz, seq):
    tb = TB_SCAN
    tps = seq // tb
    return pl.pallas_call(
        _mlstm_kernel,
        grid=(bsz, tps),
        in_specs=[pl.BlockSpec((tb, ZM), lambda b, t: (b * tps + t, 0)),
                  pl.BlockSpec((tb, ZS), lambda b, t: (b * tps + t, 0)),
                  pl.BlockSpec((1, LANES), lambda b, t: (0, 0)),
                  pl.BlockSpec((1, HEAD_DIM), lambda b, t: (0, 0))],
        out_specs=pl.BlockSpec((tb, GW), lambda b, t: (b * tps + t, 0)),
        out_shape=jax.ShapeDtypeStruct((bsz * seq, GW), BF16),
        scratch_shapes=[pltpu.VMEM((HEADS, HEAD_DIM, LANES), F32), pltpu.VMEM((HEADS, 1, LANES), F32)],
        compiler_params=_cparams(("parallel", "arbitrary")),
        name="mlstm",
    )(zm, zs, bif, gain)


def _gla_kernel(zg_ref, zs_ref, wa_ref, ba_ref, g_ref, o_ref, st_scr):
    @pl.when(pl.program_id(1) == 0)
    def _():
        st_scr[...] = jnp.zeros_like(st_scr)

    lg = GLA_CHUNK
    n_chunks = zg_ref.shape[0] // lg
    causal = _tril(lg)
    tril_f = causal.astype(F32)
    wa = wa_ref[...]
    ba = ba_ref[...]
    gain = g_ref[...]

    def chunk_body(ci, carry):
        r0 = pl.multiple_of(ci * lg, lg)
        ga = zs_ref[pl.ds(r0, lg), :].astype(BF16)
        la = _log_sigmoid(jnp.dot(ga, wa, preferred_element_type=F32) + ba) * (1.0 / GLA_TAU)
        bc = jnp.dot(tril_f, la, preferred_element_type=F32, precision=lax.Precision.HIGHEST)
        b_end = bc[lg - 1:lg, :]
        blk = zg_ref[pl.ds(r0, lg), :]
        q = blk[:, 0:GW].astype(F32) * (HEAD_DIM ** -0.5)
        k = blk[:, GW:2 * GW].astype(F32)
        qt = (q * jnp.exp(bc)).astype(BF16)
        kt = (k * jnp.exp(-bc)).astype(BF16)
        ks = (k * jnp.exp(b_end - bc)).astype(BF16)
        dec = jnp.exp(b_end)
        outs = []
        for h in range(HEADS):
            c0 = h * HEAD_DIM
            sl = slice(c0, c0 + HEAD_DIM)
            vh = blk[:, 2 * GW + c0:2 * GW + c0 + HEAD_DIM]
            rh = blk[:, 3 * GW + c0:3 * GW + c0 + HEAD_DIM].astype(F32)
            att = lax.dot_general(qt[:, sl], kt[:, sl], (((1,), (1,)), ((), ())), preferred_element_type=F32)
            att = jnp.where(causal, att, 0.0)
            st = st_scr[h]
            o = (jnp.dot(att.astype(BF16), vh, preferred_element_type=F32)
                 + lax.dot_general(qt[:, sl], st.astype(BF16), (((1,), (1,)), ((), ())),
                                   preferred_element_type=F32))
            on = o * lax.rsqrt(jnp.mean(o * o, axis=-1, keepdims=True) + EPS) * gain
            outs.append(on * (rh * _sigmoid(rh)))
            st_scr[h] = st * dec[:, sl] + lax.dot_general(vh, ks[:, sl], (((0,), (0,)), ((), ())),
                                                          preferred_element_type=F32)
        o_ref[pl.ds(r0, lg), :] = jnp.concatenate(outs, axis=1).astype(BF16)
        return carry

    lax.fori_loop(0, n_chunks, chunk_body, 0)


def _gla_call(zg, zs, wa, ba, gain, bsz, seq):
    tb = TB_SCAN
    tps = seq // tb
    return pl.pallas_call(
        _gla_kernel,
        grid=(bsz, tps),
        in_specs=[pl.BlockSpec((tb, ZG), lambda b, t: (b * tps + t, 0)),
                  pl.BlockSpec((tb, ZS), lambda b, t: (b * tps + t, 0)),
                  pl.BlockSpec((ZS, GW), lambda b, t: (0, 0)),
                  pl.BlockSpec((1, GW), lambda b, t: (0, 0)),
                  pl.BlockSpec((1, HEAD_DIM), lambda b, t: (0, 0))],
        out_specs=pl.BlockSpec((tb, GW), lambda b, t: (b * tps + t, 0)),
        out_shape=jax.ShapeDtypeStruct((bsz * seq, GW), BF16),
        scratch_shapes=[pltpu.VMEM((HEADS, HEAD_DIM, HEAD_DIM), F32)],
        compiler_params=_cparams(("parallel", "arbitrary")),
        name="gla",
    )(zg, zs, wa, ba, gain)


def _attn_kernel(lam_init, q_ref, k_ref, v_ref, lamp_ref, g_ref, o_ref, qm_scr, m_scr, acc_scr):
    tq = q_ref.shape[0]
    qi = pl.program_id(1)
    n_hc = 2 * HEADS
    lane = lax.broadcasted_iota(jnp.int32, (1, LANES), 1)
    for hc in range(n_hc):
        h, c = divmod(hc, 2)
        pair = h // 2
        lo = (h % 2) * HEAD_DIM + c * DIFF_DIM
        qp = q_ref[:, pair * LANES:(pair + 1) * LANES]
        qm_scr[hc] = jnp.where((lane >= lo) & (lane < lo + DIFF_DIM), qp, jnp.zeros_like(qp))
    m_scr[...] = jnp.full(m_scr.shape, NEG_INF, F32)
    acc_scr[...] = jnp.zeros_like(acc_scr)

    def kv_block(j, masked):
        r0 = pl.multiple_of(j * tq, tq)
        if masked:
            keep = _tril(tq)
        for hc in range(n_hc):
            h = hc // 2
            pair = h // 2
            kb = k_ref[pl.ds(r0, tq), pair * LANES:(pair + 1) * LANES]
            vb = v_ref[pl.ds(r0, tq), h * LANES:(h + 1) * LANES]
            s = lax.dot_general(qm_scr[hc], kb, (((1,), (1,)), ((), ())), preferred_element_type=F32)
            if masked:
                s = jnp.where(keep, s, NEG_INF)
            m_old = m_scr[hc]
            m_new = jnp.maximum(m_old, jnp.max(s, axis=-1, keepdims=True))
            alpha = jnp.exp(m_old - m_new)
            p = jnp.exp(s - m_new).astype(BF16)
            acc_scr[hc] = alpha * acc_scr[hc] + jnp.dot(p, vb, preferred_element_type=F32)
            m_scr[hc] = m_new

    def full_block(j, carry):
        kv_block(j, False)
        return carry

    lax.fori_loop(0, qi, full_block, 0)
    kv_block(qi, True)

    lamp = lamp_ref[...]
    lam = (jnp.exp(jnp.sum(lamp[0:1, :] * lamp[1:2, :], axis=-1, keepdims=True))
           - jnp.exp(jnp.sum(lamp[2:3, :] * lamp[3:4, :], axis=-1, keepdims=True)) + lam_init)
    gain = g_ref[...] * (1.0 - lam_init)
    outs = []
    for h in range(HEADS):
        a1 = acc_scr[2 * h]
        a2 = acc_scr[2 * h + 1]
        o = (a1[:, :HEAD_DIM] / a1[:, HEAD_DIM:HEAD_DIM + 1]
             - lam * (a2[:, :HEAD_DIM] / a2[:, HEAD_DIM:HEAD_DIM + 1]))
        outs.append(o * lax.rsqrt(jnp.mean(o * o, axis=-1, keepdims=True) + EPS) * gain)
    o_ref[...] = jnp.concatenate(outs, axis=1).astype(BF16)


def _attn_call(zq, zk, zv, lamp, gain, lam_init, bsz, seq):
    tq = TQ_ATTN
    nq = seq // tq
    return pl.pallas_call(
        functools.partial(_attn_kernel, lam_init),
        grid=(bsz, nq),
        in_specs=[pl.BlockSpec((tq, ZQ), lambda b, i: (b * nq + i, 0)),
                  pl.BlockSpec((seq, ZK), lambda b, i: (b, 0)),
                  pl.BlockSpec((seq, ZV), lambda b, i: (b, 0)),
                  pl.BlockSpec((4, LANES), lambda b, i: (0, 0)),
                  pl.BlockSpec((1, HEAD_DIM), lambda b, i: (0, 0))],
        out_specs=pl.BlockSpec((tq, GW), lambda b, i: (b * nq + i, 0)),
        out_shape=jax.ShapeDtypeStruct((bsz * seq, GW), BF16),
        scratch_shapes=[pltpu.VMEM((2 * HEADS, tq, LANES), BF16),
                        pltpu.VMEM((2 * HEADS, tq, 1), F32),
                        pltpu.VMEM((2 * HEADS, tq, LANES), F32)],
        compiler_params=_cparams(("parallel", "arbitrary")),
        name="diff_attn",
    )(zq, zk, zv, lamp, gain)


def _outproj_kernel(tiles_per_seq, ya_ref, zc_ref, halo_ref, yc_ref, yd_ref, x_ref, mod_ref, wo_ref, cw_ref,
                    cb_ref, lng_ref, lnb_ref, rw_ref, rb_ref, x1_ref, u2_ref, comb_ref):
    tm = x_ref.shape[0]
    mod = mod_ref[0]
    zc = zc_ref[...].astype(F32)
    halo = halo_ref[...].astype(F32)
    first = (pl.program_id(0) % tiles_per_seq) == 0
    s_cur = zc[:, GW:2 * GW] * zc[:, 2 * GW:3 * GW]
    s_halo = jnp.where(first, 0.0, halo[:, GW:2 * GW] * halo[:, 2 * GW:3 * GW])
    s_ext = jnp.concatenate([s_halo, s_cur], axis=0)
    hw = halo.shape[0]
    cw = cw_ref[...]
    conv = (cw[0:1, :] * s_ext[hw - 2:hw - 2 + tm, :] + cw[1:2, :] * s_ext[hw - 1:hw - 1 + tm, :]
            + cw[2:3, :] * s_cur + cb_ref[...])
    yb = (zc[:, 0:GW] * conv).astype(BF16)

    y = (jnp.dot(ya_ref[...], wo_ref[0:GW, :], preferred_element_type=F32)
         + jnp.dot(yb, wo_ref[GW:2 * GW, :], preferred_element_type=F32)
         + jnp.dot(yc_ref[...], wo_ref[2 * GW:3 * GW, :], preferred_element_type=F32)
         + jnp.dot(yd_ref[...], wo_ref[3 * GW:4 * GW, :], preferred_element_type=F32))
    x1 = _layer_norm(ALPHA * x_ref[...] + (1.0 + mod[2:3, :]) * y, lng_ref[...], lnb_ref[...])
    x1_ref[...] = x1
    u2 = (x1 * (1.0 + mod[4:5, :]) + mod[3:4, :]).astype(BF16)
    u2_ref[...] = u2

    logits = lax.dot_general(rw_ref[...], u2, (((1,), (1,)), ((), ())), preferred_element_type=F32)
    scores = _sigmoid(logits)
    biased = scores + rb_ref[...]
    rows = [biased[e:e + 1, :] for e in range(N_EXPERTS)]
    srow = [scores[e:e + 1, :] for e in range(N_EXPERTS)]
    gscore = []
    for g in range(N_GROUPS):
        a = rows[g * GROUP_SIZE:(g + 1) * GROUP_SIZE]
        best = None
        for i in range(GROUP_SIZE):
            for j in range(i + 1, GROUP_SIZE):
                pair_sum = a[i] + a[j]
                best = pair_sum if best is None else jnp.maximum(best, pair_sum)
        gscore.append(best)
    gmax = functools.reduce(jnp.maximum, gscore)
    gsel = jnp.full(gmax.shape, N_GROUPS, jnp.int32)
    for g in reversed(range(N_GROUPS)):
        gsel = jnp.where(gscore[g] == gmax, g, gsel)
    masked = [jnp.where(gsel == (e // GROUP_SIZE), rows[e], NEG_INF) for e in range(N_EXPERTS)]
    m1 = functools.reduce(jnp.maximum, masked)
    i1 = jnp.full(m1.shape, N_EXPERTS, jnp.int32)
    for e in reversed(range(N_EXPERTS)):
        i1 = jnp.where(masked[e] == m1, e, i1)
    masked2 = [jnp.where(i1 == e, NEG_INF, masked[e]) for e in range(N_EXPERTS)]
    m2 = functools.reduce(jnp.maximum, masked2)
    i2 = jnp.full(m2.shape, N_EXPERTS, jnp.int32)
    for e in reversed(range(N_EXPERTS)):
        i2 = jnp.where((masked2[e] == m2) & (i1 != e), e, i2)
    sel1 = functools.reduce(lambda acc, e: acc + jnp.where(i1 == e, srow[e], 0.0), range(N_EXPERTS), 0.0)
    sel2 = functools.reduce(lambda acc, e: acc + jnp.where(i2 == e, srow[e], 0.0), range(N_EXPERTS), 0.0)
    tot = sel1 + sel2
    g1 = sel1 / tot
    g2 = sel2 / tot
    comb_t = jnp.concatenate(
        [jnp.where(i1 == e, g1, 0.0) + jnp.where(i2 == e, g2, 0.0) for e in range(N_EXPERTS)], axis=0)
    comb_t = jnp.concatenate([comb_t, jnp.zeros((LANES - N_EXPERTS, tm), F32)], axis=0)
    comb_ref[...] = comb_t.T


def _outproj_call(ya, zc, yc, yd, x2, mod_l, wo, cw, cb, lng, lnb, rw, rb, seq):
    n, d = x2.shape
    tm = TM_PROJ
    tps = seq // tm
    hw = 16
    row = lambda w: pl.BlockSpec((tm, w), lambda i: (i, 0))
    full = lambda a: pl.BlockSpec(a.shape, lambda i: (0,) * a.ndim)
    return pl.pallas_call(
        functools.partial(_outproj_kernel, tps),
        grid=(n // tm,),
        in_specs=[row(GW), row(ZC),
                  pl.BlockSpec((hw, ZC), lambda i: (jnp.maximum(i * (tm // hw) - 1, 0), 0)),
                  row(GW), row(GW), row(d),
                  pl.BlockSpec((1, 6, d), lambda i: (i // tps, 0, 0)),
                  full(wo), full(cw), full(cb), full(lng), full(lnb), full(rw), full(rb)],
        out_specs=[row(d), row(d), row(LANES)],
        out_shape=[jax.ShapeDtypeStruct((n, d), F32), jax.ShapeDtypeStruct((n, d), BF16),
                   jax.ShapeDtypeStruct((n, LANES), F32)],
        compiler_params=_cparams(("parallel",)),
        name="out_proj_router",
    )(ya, zc, zc, yc, yd, x2, mod_l, wo, cw, cb, lng, lnb, rw, rb)


def _moe_kernel(u_ref, comb_ref, x1_ref, mod_ref, wg_ref, wu_ref, wd_ref, lng_ref, lnb_ref, o_ref, acc_scr):
    e = pl.program_id(1)

    @pl.when(e == 0)
    def _():
        acc_scr[...] = jnp.zeros_like(acc_scr)

    u = u_ref[...]
    gate = jnp.dot(u, wg_ref[0], preferred_element_type=F32)
    up = jnp.dot(u, wu_ref[0], preferred_element_type=F32)
    hidden = (gate * _sigmoid(gate) * up).astype(BF16)
    lane = lax.broadcasted_iota(jnp.int32, (1, LANES), 1)
    w_e = jnp.sum(jnp.where(lane == e, comb_ref[...], 0.0), axis=-1, keepdims=True)
    acc_scr[...] += w_e * jnp.dot(hidden, wd_ref[0], preferred_element_type=F32)

    @pl.when(e == pl.num_programs(1) - 1)
    def _():
        mod = mod_ref[0]
        o_ref[...] = _layer_norm(ALPHA * x1_ref[...] + (1.0 + mod[5:6, :]) * acc_scr[...], lng_ref[...],
                                 lnb_ref[...])


def _moe_call(u2, comb, x1, mod_l, wg, wu, wd, lng, lnb, seq):
    n, d = x1.shape
    tm = TM_MOE
    tps = seq // tm
    n_e, _, de = wg.shape
    return pl.pallas_call(
        _moe_kernel,
        grid=(n // tm, n_e),
        in_specs=[pl.BlockSpec((tm, d), lambda i, e: (i, 0)),
                  pl.BlockSpec((tm, LANES), lambda i, e: (i, 0)),
                  pl.BlockSpec((tm, d), lambda i, e: (i, 0)),
                  pl.BlockSpec((1, 6, d), lambda i, e: (i // tps, 0, 0)),
                  pl.BlockSpec((1, d, de), lambda i, e: (e, 0, 0)),
                  pl.BlockSpec((1, d, de), lambda i, e: (e, 0, 0)),
                  pl.BlockSpec((1, de, d), lambda i, e: (e, 0, 0)),
                  pl.BlockSpec((1, d), lambda i, e: (0, 0)),
                  pl.BlockSpec((1, d), lambda i, e: (0, 0))],
        out_specs=pl.BlockSpec((tm, d), lambda i, e: (i, 0)),
        out_shape=jax.ShapeDtypeStruct((n, d), F32),
        scratch_shapes=[pltpu.VMEM((tm, d), F32)],
        compiler_params=_cparams(("parallel", "arbitrary")),
        name="experts_ln2",
    )(u2, comb, x1, mod_l, wg, wu, wd, lng, lnb)


def _rearranged_w_in(w_in_l):
    d = w_in_l.shape[0]
    zeros = lambda w: jnp.zeros((d, w), w_in_l.dtype)
    o_gate, o_cb, o_dq, o_dk, o_dv, o_g, o_ga = 1024, 1032, 1800, 2056, 2312, 2568, 3592
    v_cols = []
    for h in range(HEADS):
        v_cols += [w_in_l[:, o_dv + h * HEAD_DIM:o_dv + (h + 1) * HEAD_DIM], zeros(LANES - HEAD_DIM)]
    small = [w_in_l[:, o_gate:o_gate + 2 * HEADS], w_in_l[:, o_ga:o_ga + GLA_RANK],
             zeros(ZS - 2 * HEADS - GLA_RANK)]
    cols = [w_in_l[:, 0:ZM], w_in_l[:, o_cb:o_cb + ZC], w_in_l[:, o_dq:o_dq + ZQ], w_in_l[:, o_dk:o_dk + ZK]]
    cols += v_cols + [w_in_l[:, o_g:o_g + ZG]] + small
    return jnp.concatenate(cols, axis=1).astype(BF16)


def _rope_tables(positions):
    inv_freq = ROPE_THETA ** (-jnp.arange(0, ROPE_DIM, 2, dtype=F32) / ROPE_DIM)
    ang = positions.astype(F32).reshape(-1, 1) * inv_freq
    cos, sin = jnp.cos(ang), jnp.sin(ang)
    d = np.arange(ZQ) % DIFF_DIM
    half = ROPE_DIM // 2
    idx = d % half
    cosf = jnp.where(d < ROPE_DIM, cos[:, idx], 1.0)
    sinf = jnp.where(d < half, -sin[:, idx], jnp.where(d < ROPE_DIM, sin[:, idx], 0.0))
    return cosf, sinf


def _pad_lanes(v, width=LANES):
    v = v.reshape(1, -1)
    return jnp.pad(v, ((0, 0), (0, width - v.shape[1])))


def kernel(x, c, positions, w_ada, b_ada, w_in, w_out, mlstm_b_i, mlstm_b_f, mlstm_norm_g, conv_w, conv_b, diff_lambda_q1, diff_lambda_k1, diff_lambda_q2, diff_lambda_k2, diff_norm_g, gla_w_a2, gla_b_a, gla_norm_g, ln1_g, ln1_b, ln2_g, ln2_b, router_w, router_bias, exp_w_gate, exp_w_up, exp_w_down):
    bsz, seq, d = x.shape
    depth = w_in.shape[0]
    n = bsz * seq
    x2 = x.reshape(n, d)
    mod = _mod_call(c, w_ada, b_ada)
    cosf, sinf = _rope_tables(positions)
    rw_t = router_w.T.astype(BF16)
    rb = router_bias.reshape(N_EXPERTS, 1).astype(F32)
    for l in range(depth):
        lam_init = 0.8 - 0.6 * math.exp(-0.3 * l)
        w_l = _rearranged_w_in(w_in[l])
        zm, zc, zq, zk, zv, zg, zs = _inproj_call(x2, mod[l], w_l, cosf, sinf, seq)
        bif = _pad_lanes(jnp.concatenate([mlstm_b_i[l], mlstm_b_f[l]]))
        ya = _mlstm_call(zm, zs, bif, mlstm_norm_g[l].reshape(1, -1), bsz, seq)
        lamp = jnp.concatenate([_pad_lanes(diff_lambda_q1[l]), _pad_lanes(diff_lambda_k1[l]),
                                _pad_lanes(diff_lambda_q2[l]), _pad_lanes(diff_lambda_k2[l])], axis=0)
        yc = _attn_call(zq, zk, zv, lamp, diff_norm_g[l].reshape(1, -1), lam_init, bsz, seq)
        wa = jnp.pad(gla_w_a2[l], ((2 * HEADS, ZS - 2 * HEADS - GLA_RANK), (0, 0))).astype(BF16)
        yd = _gla_call(zg, zs, wa, gla_b_a[l].reshape(1, -1), gla_norm_g[l].reshape(1, -1), bsz, seq)
        x1, u2, comb = _outproj_call(ya, zc, yc, yd, x2, mod[l], w_out[l].astype(BF16), conv_w[l],
                                     conv_b[l].reshape(1, -1), ln1_g[l].reshape(1, -1), ln1_b[l].reshape(1, -1),
                                     rw_t, rb, seq)
        x2 = _moe_call(u2, comb, x1, mod[l], exp_w_gate[l].astype(BF16), exp_w_up[l].astype(BF16),
                       exp_w_down[l].astype(BF16), ln2_g[l].reshape(1, -1), ln2_b[l].reshape(1, -1), seq)
    return x2.reshape(bsz, seq, d)
```

```python
import functools
import math

import numpy as np
import jax
import jax.numpy as jnp
from jax import lax
from jax.experimental import pallas as pl
from jax.experimental.pallas import tpu as pltpu

F32 = jnp.float32
BF16 = jnp.bfloat16

D_MODEL = 1024
DEPTH = 4
HEADS = 4
GW = 256
HEAD_DIM = 64
DIFF_DIM = 32
ROPE_DIM = 8
ROPE_THETA = 500000.0
CONV_WIDTH = 3
GLA_RANK = 16
GLA_TAU = 16.0
CHUNK = 64
GLA_CHUNK = 32
N_EXPERTS = 16
N_GROUPS = 4
GROUP_SIZE = 4
D_EXPERT = 512
ALPHA = (2 * DEPTH) ** 0.25
EPS = 1e-6
LANES = 128
NEG_INF = float("-inf")
LOG2_E = math.log2(math.e)

ZM, ZC, ZQ, ZK, ZV, ZG, ZS = 1024, 768, 256, 256, 512, 1024, 128
Z_OFFS = np.cumsum([0, ZM, ZC, ZQ, ZK, ZV, ZG, ZS])
Z_TOTAL = int(Z_OFFS[-1])

TM_PROJ = 512
TB_SCAN = 256
CUM_SPAN = 128
TQ_ATTN = 512
TM_MOE = 1024
VMEM_LIMIT = 56 * 1024 * 1024


def _cparams(sem):
    return pltpu.CompilerParams(dimension_semantics=sem, vmem_limit_bytes=VMEM_LIMIT)


def _log_sigmoid(x):
    return jnp.minimum(x, 0.0) - jnp.log(1.0 + jnp.exp(-jnp.abs(x)))


def _sigmoid(x):
    return 1.0 / (1.0 + jnp.exp(-x))


def _layer_norm(v, g, b):
    mu = jnp.mean(v, axis=-1, keepdims=True)
    d = v - mu
    var = jnp.mean(d * d, axis=-1, keepdims=True)
    return d * lax.rsqrt(var + EPS) * g + b


def _tril(n):
    r = lax.broadcasted_iota(jnp.int32, (n, n), 0)
    c = lax.broadcasted_iota(jnp.int32, (n, n), 1)
    return c <= r


def _mod_kernel(c_ref, w_ref, b_ref, o_ref):
    c = c_ref[...]
    c_act = (c * _sigmoid(c)).astype(BF16)
    o_ref[0, 0] = jnp.dot(c_act, w_ref[0].astype(BF16), preferred_element_type=F32) + b_ref[0, 0]


def _mod_call(c, w_ada, b_ada):
    depth, d, _ = w_ada.shape
    bsz = c.shape[0]
    b3 = b_ada.reshape(depth, 6, 1, d)
    out = pl.pallas_call(
        _mod_kernel,
        grid=(depth, 6),
        in_specs=[pl.BlockSpec((bsz, d), lambda l, j: (0, 0)),
                  pl.BlockSpec((1, d, d), lambda l, j: (l, 0, j)),
                  pl.BlockSpec((1, 1, 1, d), lambda l, j: (l, j, 0, 0))],
        out_specs=pl.BlockSpec((1, 1, bsz, d), lambda l, j: (l, j, 0, 0)),
        out_shape=jax.ShapeDtypeStruct((depth, 6, bsz, d), F32),
        compiler_params=_cparams(("arbitrary", "arbitrary")),
        name="adaln_mod",
    )(c, w_ada, b3)
    return out.transpose(0, 2, 1, 3)


def _inproj_kernel(x_ref, mod_ref, w_ref, cos_ref, sin_ref, zm_ref, zc_ref, zq_ref, zk_ref, zv_ref, zg_ref,
                   zs_ref):
    mod = mod_ref[0]
    u = (x_ref[...] * (1.0 + mod[1:2, :]) + mod[0:1, :]).astype(BF16)

    def proj(idx):
        return jnp.dot(u, w_ref[:, int(Z_OFFS[idx]):int(Z_OFFS[idx + 1])], preferred_element_type=F32)

    zm_ref[...] = proj(0).astype(BF16)
    zc_ref[...] = proj(1).astype(BF16)

    lane = lax.broadcasted_iota(jnp.int32, (1, ZQ), 1)
    first_half = (lane % ROPE_DIM) < (ROPE_DIM // 2)
    cosf = cos_ref[...]
    sinf = sin_ref[...]

    def rope(r):
        swapped = jnp.where(first_half, pltpu.roll(r, ZQ - ROPE_DIM // 2, 1), pltpu.roll(r, ROPE_DIM // 2, 1))
        return r * cosf + swapped * sinf

    zq_ref[...] = (rope(proj(2)) * (LOG2_E * DIFF_DIM ** -0.5)).astype(BF16)
    zk_ref[...] = rope(proj(3)).astype(BF16)
    lane_v = lax.broadcasted_iota(jnp.int32, (1, ZV), 1)
    ones_col = jnp.where((lane_v % LANES) == HEAD_DIM, 1.0, 0.0)
    zv_ref[...] = (proj(4) + ones_col).astype(BF16)
    zg_ref[...] = proj(5).astype(BF16)
    zs_ref[...] = proj(6)


def _inproj_call(x2, mod_l, w_l, cosf, sinf, seq):
    n, d = x2.shape
    tm = TM_PROJ
    tps = seq // tm
    row = lambda w: pl.BlockSpec((tm, w), lambda i: (i, 0))
    outs = [(ZM, BF16), (ZC, BF16), (ZQ, BF16), (ZK, BF16), (ZV, BF16), (ZG, BF16), (ZS, F32)]
    return pl.pallas_call(
        _inproj_kernel,
        grid=(n // tm,),
        in_specs=[row(d),
                  pl.BlockSpec((1, 6, d), lambda i: (i // tps, 0, 0)),
                  pl.BlockSpec((d, Z_TOTAL), lambda i: (0, 0)),
                  row(ZQ), row(ZQ)],
        out_specs=[row(w) for w, _ in outs],
        out_shape=[jax.ShapeDtypeStruct((n, w), dt) for w, dt in outs],
        compiler_params=_cparams(("parallel",)),
        name="in_proj",
    )(x2, mod_l, w_l, cosf, sinf)


def _chunk_mats(span, chunk):
    r = lax.broadcasted_iota(jnp.int32, (span, span), 0)
    c = lax.broadcasted_iota(jnp.int32, (span, span), 1)
    same = (r // chunk) == (c // chunk)
    return (same & (c <= r)).astype(F32), same.astype(F32)


def _mlstm_kernel(zm_ref, zs_ref, bif_ref, g_ref, o_ref, c_scr, m_scr):
    @pl.when(pl.program_id(1) == 0)
    def _():
        c_scr[...] = jnp.zeros_like(c_scr)
        m_scr[...] = jnp.zeros_like(m_scr)

    tb = zm_ref.shape[0]
    L = CHUNK
    nc = tb // L
    causal = _tril(L)
    cum_m, tot_m = _chunk_mats(CUM_SPAN, L)
    lane = lax.broadcasted_iota(jnp.int32, (1, LANES), 1)
    ones_col = jnp.where(lax.broadcasted_iota(jnp.int32, (1, HEAD_DIM), 1) == 0, 1.0, 0.0).astype(BF16)
    ones_blk = jnp.broadcast_to(ones_col, (L, HEAD_DIM))
    gain = g_ref[...]
    graw = zs_ref[...] + bif_ref[...]
    gp = jnp.where(lane < HEADS, graw, jnp.where(lane < 2 * HEADS, _log_sigmoid(graw), 0.0))
    cums, tots = [], []
    for i in range(tb // CUM_SPAN):
        seg = gp[i * CUM_SPAN:(i + 1) * CUM_SPAN]
        cums.append(jnp.dot(cum_m, seg, preferred_element_type=F32, precision=lax.Precision.HIGHEST))
        tots.append(jnp.dot(tot_m, seg, preferred_element_type=F32, precision=lax.Precision.HIGHEST))
    cum = jnp.concatenate(cums, axis=0)
    tot = jnp.concatenate(tots, axis=0)
    dkv = tot[:, HEADS:2 * HEADS] - cum[:, HEADS:2 * HEADS] + gp[:, 0:HEADS]
    gp_t = gp.T
    cum_t = cum.T
    osig = _sigmoid(zm_ref[:, 3 * GW:4 * GW].astype(F32))

    c_state = [c_scr[h] for h in range(HEADS)]
    m_state = [m_scr[h][:, 0:1] for h in range(HEADS)]
    rows = []
    for c in range(nc):
        rs = slice(c * L, (c + 1) * L)
        outs = []
        for h in range(HEADS):
            qh = zm_ref[rs, h * HEAD_DIM:(h + 1) * HEAD_DIM] * (HEAD_DIM ** -0.5)
            kh = zm_ref[rs, GW + h * HEAD_DIM:GW + (h + 1) * HEAD_DIM]
            vh = zm_ref[rs, 2 * GW + h * HEAD_DIM:2 * GW + (h + 1) * HEAD_DIM]
            vaug = jnp.concatenate([vh, ones_blk], axis=1)
            b_col = cum[rs, HEADS + h:HEADS + h + 1]
            b_row = cum_t[HEADS + h:HEADS + h + 1, rs]
            li_row = gp_t[h:h + 1, rs]
            dmat = jnp.where(causal, b_col - b_row + li_row, NEG_INF)
            dmax = jnp.max(dmat, axis=-1, keepdims=True)
            s = lax.dot_general(qh, kh, (((1,), (1,)), ((), ())), preferred_element_type=F32)
            p = (s * jnp.exp(dmat - dmax)).astype(BF16)
            intra = jnp.dot(p, vaug, preferred_element_type=F32)
            dk = dkv[rs, h:h + 1]
            dkmax = jnp.max(dk, axis=0, keepdims=True)
            kw = (kh.astype(F32) * jnp.exp(dk - dkmax)).astype(BF16)
            upd = lax.dot_general(kw, vaug, (((0,), (0,)), ((), ())), preferred_element_type=F32)
            m_prev = m_state[h]
            inter = b_col + m_prev
            m_row = jnp.maximum(inter, dmax)
            num = (jnp.exp(dmax - m_row) * intra
                   + jnp.exp(inter - m_row) * jnp.dot(qh, c_state[h].astype(BF16), preferred_element_type=F32))
            den = num[:, HEAD_DIM:HEAD_DIM + 1]
            hv = num[:, :HEAD_DIM] / jnp.maximum(jnp.abs(den), jnp.exp(-m_row))
            outs.append(hv * lax.rsqrt(jnp.mean(hv * hv, axis=-1, keepdims=True) + EPS) * gain)
            b_end = tot[c * L:c * L + 1, HEADS + h:HEADS + h + 1]
            m_new = jnp.maximum(b_end + m_prev, dkmax)
            c_state[h] = jnp.exp(b_end + m_prev - m_new) * c_state[h] + jnp.exp(dkmax - m_new) * upd
            m_state[h] = m_new
        rows.append(jnp.concatenate(outs, axis=1))
    for h in range(HEADS):
        c_scr[h] = c_state[h]
        m_scr[h] = jnp.broadcast_to(m_state[h], (1, LANES))
    o_ref[...] = (jnp.concatenate(rows, axis=0) * osig).astype(BF16)


def _mlstm_call(zm, zs, bif, gain, bsz, seq):
    tb = TB_SCAN
    tps = seq // tb
    return pl.pallas_call(
        _mlstm_kernel,
        grid=(bsz, tps),
        in_specs=[pl.BlockSpec((tb, ZM), lambda b, t: (b * tps + t, 0)),
                  pl.BlockSpec((tb, ZS), lambda b, t: (b * tps + t, 0)),
                  pl.BlockSpec((1, LANES), lambda b, t: (0, 0)),
                  pl.BlockSpec((1, HEAD_DIM), lambda b, t: (0, 0))],
        out_specs=pl.BlockSpec((tb, GW), lambda b, t: (b * tps + t, 0)),
        out_shape=jax.ShapeDtypeStruct((bsz * seq, GW), BF16),
        scratch_shapes=[pltpu.VMEM((HEADS, HEAD_DIM, LANES), F32), pltpu.VMEM((HEADS, 1, LANES), F32)],
        compiler_params=_cparams(("parallel", "arbitrary")),
        name="mlstm",
    )(zm, zs, bif, gain)


def _gla_kernel(zg_ref, zs_ref, wa_ref, ba_ref, g_ref, o_ref, st_scr):
    @pl.when(pl.program_id(1) == 0)
    def _():
        st_scr[...] = jnp.zeros_like(st_scr)

    tb = zg_ref.shape[0]
    lg = GLA_CHUNK
    nc = tb // lg
    causal = _tril(lg)
    cum_m, tot_m = _chunk_mats(CUM_SPAN, lg)
    gain = g_ref[...]
    la = _log_sigmoid(jnp.dot(zs_ref[...].astype(BF16), wa_ref[...], preferred_element_type=F32)
                      + ba_ref[...]) * (1.0 / GLA_TAU)
    bcs, bes = [], []
    for i in range(tb // CUM_SPAN):
        seg = la[i * CUM_SPAN:(i + 1) * CUM_SPAN]
        bcs.append(jnp.dot(cum_m, seg, preferred_element_type=F32, precision=lax.Precision.HIGHEST))
        bes.append(jnp.dot(tot_m, seg, preferred_element_type=F32, precision=lax.Precision.HIGHEST))
    bc = jnp.concatenate(bcs, axis=0)
    be = jnp.concatenate(bes, axis=0)
    q = zg_ref[:, 0:GW].astype(F32) * (HEAD_DIM ** -0.5)
    k = zg_ref[:, GW:2 * GW].astype(F32)
    qt = (q * jnp.exp(bc)).astype(BF16)
    kt = (k * jnp.exp(-bc)).astype(BF16)
    ks = (k * jnp.exp(be - bc)).astype(BF16)
    dec = jnp.exp(be)
    v = zg_ref[:, 2 * GW:3 * GW]
    r = zg_ref[:, 3 * GW:4 * GW].astype(F32)
    rgate = r * _sigmoid(r)

    st = [st_scr[h] for h in range(HEADS)]
    rows = []
    for c in range(nc):
        rs = slice(c * lg, (c + 1) * lg)
        outs = []
        for h in range(HEADS):
            ls = slice(h * HEAD_DIM, (h + 1) * HEAD_DIM)
            qh, kh, vh = qt[rs, ls], kt[rs, ls], v[rs, ls]
            att = lax.dot_general(qh, kh, (((1,), (1,)), ((), ())), preferred_element_type=F32)
            att = jnp.where(causal, att, 0.0).astype(BF16)
            upd = lax.dot_general(vh, ks[rs, ls], (((0,), (0,)), ((), ())), preferred_element_type=F32)
            o = (jnp.dot(att, vh, preferred_element_type=F32)
                 + lax.dot_general(qh, st[h].astype(BF16), (((1,), (1,)), ((), ())), preferred_element_type=F32))
            st[h] = st[h] * dec[c * lg:c * lg + 1, ls] + upd
            outs.append(o * lax.rsqrt(jnp.mean(o * o, axis=-1, keepdims=True) + EPS) * gain)
        rows.append(jnp.concatenate(outs, axis=1))
    for h in range(HEADS):
        st_scr[h] = st[h]
    o_ref[...] = (jnp.concatenate(rows, axis=0) * rgate).astype(BF16)


def _gla_call(zg, zs, wa, ba, gain, bsz, seq):
    tb = TB_SCAN
    tps = seq // tb
    return pl.pallas_call(
        _gla_kernel,
        grid=(bsz, tps),
        in_specs=[pl.BlockSpec((tb, ZG), lambda b, t: (b * tps + t, 0)),
                  pl.BlockSpec((tb, ZS), lambda b, t: (b * tps + t, 0)),
                  pl.BlockSpec((ZS, GW), lambda b, t: (0, 0)),
                  pl.BlockSpec((1, GW), lambda b, t: (0, 0)),
                  pl.BlockSpec((1, HEAD_DIM), lambda b, t: (0, 0))],
        out_specs=pl.BlockSpec((tb, GW), lambda b, t: (b * tps + t, 0)),
        out_shape=jax.ShapeDtypeStruct((bsz * seq, GW), BF16),
        scratch_shapes=[pltpu.VMEM((HEADS, HEAD_DIM, HEAD_DIM), F32)],
        compiler_params=_cparams(("parallel", "arbitrary")),
        name="gla",
    )(zg, zs, wa, ba, gain)


def _attn_kernel(lam_init, q_ref, k_ref, v_ref, lamp_ref, g_ref, o_ref, qm_scr, m_scr, acc_scr):
    tq = q_ref.shape[0]
    qi = pl.program_id(1)
    n_hc = 2 * HEADS
    lane = lax.broadcasted_iota(jnp.int32, (1, LANES), 1)
    for hc in range(n_hc):
        h, c = divmod(hc, 2)
        pair = h // 2
        lo = (h % 2) * HEAD_DIM + c * DIFF_DIM
        qp = q_ref[:, pair * LANES:(pair + 1) * LANES]
        qm_scr[hc] = jnp.where((lane >= lo) & (lane < lo + DIFF_DIM), qp, jnp.zeros_like(qp))
    m_scr[...] = jnp.full(m_scr.shape, NEG_INF, F32)
    acc_scr[...] = jnp.zeros_like(acc_scr)

    nslab = tq // LANES

    def kv_block(j, masked):
        r0 = pl.multiple_of(j * tq, tq)
        if masked:
            keep = _tril(tq)
        for hc in range(n_hc):
            h = hc // 2
            pair = h // 2
            kb = k_ref[pl.ds(r0, tq), pair * LANES:(pair + 1) * LANES]
            vb = v_ref[pl.ds(r0, tq), h * LANES:(h + 1) * LANES]
            s = lax.dot_general(qm_scr[hc], kb, (((1,), (1,)), ((), ())), preferred_element_type=F32)
            if masked:
                s = jnp.where(keep, s, NEG_INF)
            slabs = [s[:, i * LANES:(i + 1) * LANES] for i in range(nslab)]
            mx = functools.reduce(jnp.maximum, slabs)
            m_old = m_scr[hc]
            m_new = jnp.maximum(m_old, jnp.broadcast_to(jnp.max(mx, axis=-1, keepdims=True), (tq, LANES)))
            alpha = jnp.exp2(m_old - m_new)
            p = jnp.concatenate([jnp.exp2(sl - m_new) for sl in slabs], axis=1).astype(BF16)
            acc_scr[hc] = alpha * acc_scr[hc] + jnp.dot(p, vb, preferred_element_type=F32)
            m_scr[hc] = m_new

    def full_block(j, carry):
        kv_block(j, False)
        return carry

    lax.fori_loop(0, qi, full_block, 0)
    kv_block(qi, True)

    lamp = lamp_ref[...]
    lam = (jnp.exp(jnp.sum(lamp[0:1, :] * lamp[1:2, :], axis=-1, keepdims=True))
           - jnp.exp(jnp.sum(lamp[2:3, :] * lamp[3:4, :], axis=-1, keepdims=True)) + lam_init)
    gain = g_ref[...] * (1.0 - lam_init)
    outs = []
    for h in range(HEADS):
        a1 = acc_scr[2 * h]
        a2 = acc_scr[2 * h + 1]
        o = (a1[:, :HEAD_DIM] / a1[:, HEAD_DIM:HEAD_DIM + 1]
             - lam * (a2[:, :HEAD_DIM] / a2[:, HEAD_DIM:HEAD_DIM + 1]))
        outs.append(o * lax.rsqrt(jnp.mean(o * o, axis=-1, keepdims=True) + EPS) * gain)
    o_ref[...] = jnp.concatenate(outs, axis=1).astype(BF16)


def _attn_call(zq, zk, zv, lamp, gain, lam_init, bsz, seq):
    tq = TQ_ATTN
    nq = seq // tq
    return pl.pallas_call(
        functools.partial(_attn_kernel, lam_init),
        grid=(bsz, nq),
        in_specs=[pl.BlockSpec((tq, ZQ), lambda b, i: (b * nq + i, 0)),
                  pl.BlockSpec((seq, ZK), lambda b, i: (b, 0)),
                  pl.BlockSpec((seq, ZV), lambda b, i: (b, 0)),
                  pl.BlockSpec((4, LANES), lambda b, i: (0, 0)),
                  pl.BlockSpec((1, HEAD_DIM), lambda b, i: (0, 0))],
        out_specs=pl.BlockSpec((tq, GW), lambda b, i: (b * nq + i, 0)),
        out_shape=jax.ShapeDtypeStruct((bsz * seq, GW), BF16),
        scratch_shapes=[pltpu.VMEM((2 * HEADS, tq, LANES), BF16),
                        pltpu.VMEM((2 * HEADS, tq, LANES), F32),
                        pltpu.VMEM((2 * HEADS, tq, LANES), F32)],
        compiler_params=_cparams(("parallel", "arbitrary")),
        name="diff_attn",
    )(zq, zk, zv, lamp, gain)


def _outproj_kernel(tiles_per_seq, ya_ref, zc_ref, halo_ref, yc_ref, yd_ref, x_ref, mod_ref, wo_ref, cw_ref,
                    cb_ref, lng_ref, lnb_ref, rw_ref, rb_ref, x1_ref, u2_ref, comb_ref):
    tm = x_ref.shape[0]
    mod = mod_ref[0]
    zc = zc_ref[...].astype(F32)
    halo = halo_ref[...].astype(F32)
    first = (pl.program_id(0) % tiles_per_seq) == 0
    s_cur = zc[:, GW:2 * GW] * zc[:, 2 * GW:3 * GW]
    s_halo = jnp.where(first, 0.0, halo[:, GW:2 * GW] * halo[:, 2 * GW:3 * GW])
    s_ext = jnp.concatenate([s_halo, s_cur], axis=0)
    hw = halo.shape[0]
    cw = cw_ref[...]
    conv = (cw[0:1, :] * s_ext[hw - 2:hw - 2 + tm, :] + cw[1:2, :] * s_ext[hw - 1:hw - 1 + tm, :]
            + cw[2:3, :] * s_cur + cb_ref[...])
    yb = (zc[:, 0:GW] * conv).astype(BF16)

    y = (jnp.dot(ya_ref[...], wo_ref[0:GW, :], preferred_element_type=F32)
         + jnp.dot(yb, wo_ref[GW:2 * GW, :], preferred_element_type=F32)
         + jnp.dot(yc_ref[...], wo_ref[2 * GW:3 * GW, :], preferred_element_type=F32)
         + jnp.dot(yd_ref[...], wo_ref[3 * GW:4 * GW, :], preferred_element_type=F32))
    x1 = _layer_norm(ALPHA * x_ref[...] + (1.0 + mod[2:3, :]) * y, lng_ref[...], lnb_ref[...])
    x1_ref[...] = x1
    u2 = (x1 * (1.0 + mod[4:5, :]) + mod[3:4, :]).astype(BF16)
    u2_ref[...] = u2

    logits = lax.dot_general(rw_ref[...], u2, (((1,), (1,)), ((), ())), preferred_element_type=F32)
    scores = _sigmoid(logits)
    biased = scores + rb_ref[...]
    rows = [biased[e:e + 1, :] for e in range(N_EXPERTS)]
    srow = [scores[e:e + 1, :] for e in range(N_EXPERTS)]
    gscore = []
    for g in range(N_GROUPS):
        a = rows[g * GROUP_SIZE:(g + 1) * GROUP_SIZE]
        best = None
        for i in range(GROUP_SIZE):
            for j in range(i + 1, GROUP_SIZE):
                pair_sum = a[i] + a[j]
                best = pair_sum if best is None else jnp.maximum(best, pair_sum)
        gscore.append(best)
    gmax = functools.reduce(jnp.maximum, gscore)
    gsel = jnp.full(gmax.shape, N_GROUPS, jnp.int32)
    for g in reversed(range(N_GROUPS)):
        gsel = jnp.where(gscore[g] == gmax, g, gsel)
    masked = [jnp.where(gsel == (e // GROUP_SIZE), rows[e], NEG_INF) for e in range(N_EXPERTS)]
    m1 = functools.reduce(jnp.maximum, masked)
    i1 = jnp.full(m1.shape, N_EXPERTS, jnp.int32)
    for e in reversed(range(N_EXPERTS)):
        i1 = jnp.where(masked[e] == m1, e, i1)
    masked2 = [jnp.where(i1 == e, NEG_INF, masked[e]) for e in range(N_EXPERTS)]
    m2 = functools.reduce(jnp.maximum, masked2)
    i2 = jnp.full(m2.shape, N_EXPERTS, jnp.int32)
    for e in reversed(range(N_EXPERTS)):
        i2 = jnp.where((masked2[e] == m2) & (i1 != e), e, i2)
    sel1 = functools.reduce(lambda acc, e: acc + jnp.where(i1 == e, srow[e], 0.0), range(N_EXPERTS), 0.0)
    sel2 = functools.reduce(lambda acc, e: acc + jnp.where(i2 == e, srow[e], 0.0), range(N_EXPERTS), 0.0)
    tot = sel1 + sel2
    g1 = sel1 / tot
    g2 = sel2 / tot
    comb_t = jnp.concatenate(
        [jnp.where(i1 == e, g1, 0.0) + jnp.where(i2 == e, g2, 0.0) for e in range(N_EXPERTS)], axis=0)
    comb_t = jnp.concatenate([comb_t, jnp.zeros((LANES - N_EXPERTS, tm), F32)], axis=0)
    comb_ref[...] = comb_t.T


def _outproj_call(ya, zc, yc, yd, x2, mod_l, wo, cw, cb, lng, lnb, rw, rb, seq):
    n, d = x2.shape
    tm = TM_PROJ
    tps = seq // tm
    hw = 16
    row = lambda w: pl.BlockSpec((tm, w), lambda i: (i, 0))
    full = lambda a: pl.BlockSpec(a.shape, lambda i: (0,) * a.ndim)
    return pl.pallas_call(
        functools.partial(_outproj_kernel, tps),
        grid=(n // tm,),
        in_specs=[row(GW), row(ZC),
                  pl.BlockSpec((hw, ZC), lambda i: (jnp.maximum(i * (tm // hw) - 1, 0), 0)),
                  row(GW), row(GW), row(d),
                  pl.BlockSpec((1, 6, d), lambda i: (i // tps, 0, 0)),
                  full(wo), full(cw), full(cb), full(lng), full(lnb), full(rw), full(rb)],
        out_specs=[row(d), row(d), row(LANES)],
        out_shape=[jax.ShapeDtypeStruct((n, d), F32), jax.ShapeDtypeStruct((n, d), BF16),
                   jax.ShapeDtypeStruct((n, LANES), F32)],
        compiler_params=_cparams(("parallel",)),
        name="out_proj_router",
    )(ya, zc, zc, yc, yd, x2, mod_l, wo, cw, cb, lng, lnb, rw, rb)


def _moe_kernel(u_ref, comb_ref, x1_ref, mod_ref, wg_ref, wu_ref, wd_ref, lng_ref, lnb_ref, o_ref, acc_scr):
    e = pl.program_id(1)

    @pl.when(e == 0)
    def _():
        acc_scr[...] = jnp.zeros_like(acc_scr)

    u = u_ref[...]
    gate = jnp.dot(u, wg_ref[0], preferred_element_type=F32)
    up = jnp.dot(u, wu_ref[0], preferred_element_type=F32)
    hidden = (gate * _sigmoid(gate) * up).astype(BF16)
    lane = lax.broadcasted_iota(jnp.int32, (1, LANES), 1)
    w_e = jnp.sum(jnp.where(lane == e, comb_ref[...], 0.0), axis=-1, keepdims=True)
    acc_scr[...] += w_e * jnp.dot(hidden, wd_ref[0], preferred_element_type=F32)

    @pl.when(e == pl.num_programs(1) - 1)
    def _():
        mod = mod_ref[0]
        o_ref[...] = _layer_norm(ALPHA * x1_ref[...] + (1.0 + mod[5:6, :]) * acc_scr[...], lng_ref[...],
                                 lnb_ref[...])


def _moe_call(u2, comb, x1, mod_l, wg, wu, wd, lng, lnb, seq):
    n, d = x1.shape
    tm = TM_MOE
    tps = seq // tm
    n_e, _, de = wg.shape
    return pl.pallas_call(
        _moe_kernel,
        grid=(n // tm, n_e),
        in_specs=[pl.BlockSpec((tm, d), lambda i, e: (i, 0)),
                  pl.BlockSpec((tm, LANES), lambda i, e: (i, 0)),
                  pl.BlockSpec((tm, d), lambda i, e: (i, 0)),
                  pl.BlockSpec((1, 6, d), lambda i, e: (i // tps, 0, 0)),
                  pl.BlockSpec((1, d, de), lambda i, e: (e, 0, 0)),
                  pl.BlockSpec((1, d, de), lambda i, e: (e, 0, 0)),
                  pl.BlockSpec((1, de, d), lambda i, e: (e, 0, 0)),
                  pl.BlockSpec((1, d), lambda i, e: (0, 0)),
                  pl.BlockSpec((1, d), lambda i, e: (0, 0))],
        out_specs=pl.BlockSpec((tm, d), lambda i, e: (i, 0)),
        out_shape=jax.ShapeDtypeStruct((n, d), F32),
        scratch_shapes=[pltpu.VMEM((tm, d), F32)],
        compiler_params=_cparams(("parallel", "arbitrary")),
        name="experts_ln2",
    )(u2, comb, x1, mod_l, wg, wu, wd, lng, lnb)


def _rearranged_w_in(w_in_l):
    d = w_in_l.shape[0]
    zeros = lambda w: jnp.zeros((d, w), w_in_l.dtype)
    o_gate, o_cb, o_dq, o_dk, o_dv, o_g, o_ga = 1024, 1032, 1800, 2056, 2312, 2568, 3592
    v_cols = []
    for h in range(HEADS):
        v_cols += [w_in_l[:, o_dv + h * HEAD_DIM:o_dv + (h + 1) * HEAD_DIM], zeros(LANES - HEAD_DIM)]
    small = [w_in_l[:, o_gate:o_gate + 2 * HEADS], w_in_l[:, o_ga:o_ga + GLA_RANK],
             zeros(ZS - 2 * HEADS - GLA_RANK)]
    cols = [w_in_l[:, 0:ZM], w_in_l[:, o_cb:o_cb + ZC], w_in_l[:, o_dq:o_dq + ZQ], w_in_l[:, o_dk:o_dk + ZK]]
    cols += v_cols + [w_in_l[:, o_g:o_g + ZG]] + small
    return jnp.concatenate(cols, axis=1).astype(BF16)


def _rope_tables(positions):
    inv_freq = ROPE_THETA ** (-jnp.arange(0, ROPE_DIM, 2, dtype=F32) / ROPE_DIM)
    ang = positions.astype(F32).reshape(-1, 1) * inv_freq
    cos, sin = jnp.cos(ang), jnp.sin(ang)
    d = np.arange(ZQ) % DIFF_DIM
    half = ROPE_DIM // 2
    idx = d % half
    cosf = jnp.where(d < ROPE_DIM, cos[:, idx], 1.0)
    sinf = jnp.where(d < half, -sin[:, idx], jnp.where(d < ROPE_DIM, sin[:, idx], 0.0))
    return cosf, sinf


def _pad_lanes(v, width=LANES):
    v = v.reshape(1, -1)
    return jnp.pad(v, ((0, 0), (0, width - v.shape[1])))


def kernel(x, c, positions, w_ada, b_ada, w_in, w_out, mlstm_b_i, mlstm_b_f, mlstm_norm_g, conv_w, conv_b, diff_lambda_q1, diff_lambda_k1, diff_lambda_q2, diff_lambda_k2, diff_norm_g, gla_w_a2, gla_b_a, gla_norm_g, ln1_g, ln1_b, ln2_g, ln2_b, router_w, router_bias, exp_w_gate, exp_w_up, exp_w_down):
    bsz, seq, d = x.shape
    depth = w_in.shape[0]
    n = bsz * seq
    x2 = x.reshape(n, d)
    mod = _mod_call(c, w_ada, b_ada)
    cosf, sinf = _rope_tables(positions)
    rw_t = router_w.T.astype(BF16)
    rb = router_bias.reshape(N_EXPERTS, 1).astype(F32)
    for l in range(depth):
        lam_init = 0.8 - 0.6 * math.exp(-0.3 * l)
        w_l = _rearranged_w_in(w_in[l])
        zm, zc, zq, zk, zv, zg, zs = _inproj_call(x2, mod[l], w_l, cosf, sinf, seq)
        bif = _pad_lanes(jnp.concatenate([mlstm_b_i[l], mlstm_b_f[l]]))
        ya = _mlstm_call(zm, zs, bif, mlstm_norm_g[l].reshape(1, -1), bsz, seq)
        lamp = jnp.concatenate([_pad_lanes(diff_lambda_q1[l]), _pad_lanes(diff_lambda_k1[l]),
                                _pad_lanes(diff_lambda_q2[l]), _pad_lanes(diff_lambda_k2[l])], axis=0)
        yc = _attn_call(zq, zk, zv, lamp, diff_norm_g[l].reshape(1, -1), lam_init, bsz, seq)
        wa = jnp.pad(gla_w_a2[l], ((2 * HEADS, ZS - 2 * HEADS - GLA_RANK), (0, 0))).astype(BF16)
        yd = _gla_call(zg, zs, wa, gla_b_a[l].reshape(1, -1), gla_norm_g[l].reshape(1, -1), bsz, seq)
        x1, u2, comb = _outproj_call(ya, zc, yc, yd, x2, mod[l], w_out[l].astype(BF16), conv_w[l],
                                     conv_b[l].reshape(1, -1), ln1_g[l].reshape(1, -1), ln1_b[l].reshape(1, -1),
                                     rw_t, rb, seq)
        x2 = _moe_call(u2, comb, x1, mod[l], exp_w_gate[l].astype(BF16), exp_w_up[l].astype(BF16),
                       exp_w_down[l].astype(BF16), ln2_g[l].reshape(1, -1), ln2_b[l].reshape(1, -1), seq)
    return x2.reshape(bsz, seq, d)
```

```python
import functools
import math

import numpy as np
import jax
import jax.numpy as jnp
from jax import lax
from jax.experimental import pallas as pl
from jax.experimental.pallas import tpu as pltpu

F32 = jnp.float32
BF16 = jnp.bfloat16

D_MODEL = 1024
DEPTH = 4
HEADS = 4
GW = 256
HEAD_DIM = 64
DIFF_DIM = 32
ROPE_DIM = 8
ROPE_THETA = 500000.0
CONV_WIDTH = 3
GLA_RANK = 16
GLA_TAU = 16.0
CHUNK = 64
GLA_CHUNK = 32
N_EXPERTS = 16
N_GROUPS = 4
GROUP_SIZE = 4
D_EXPERT = 512
ALPHA = (2 * DEPTH) ** 0.25
EPS = 1e-6
LANES = 128
NEG_INF = float("-inf")
LOG2_E = math.log2(math.e)

ZM, ZC, ZQ, ZK, ZV, ZG, ZS = 1024, 768, 256, 256, 512, 1024, 128
Z_OFFS = np.cumsum([0, ZM, ZC, ZQ, ZK, ZV, ZG, ZS])
Z_TOTAL = int(Z_OFFS[-1])

TM_PROJ = 512
TB_SCAN = 256
CUM_SPAN = 128
TQ_ATTN = 512
TM_MOE = 512
DISPATCH_BLOCK = 256
SEG_ALIGN = 8
SEG_PIECE_BITS = 6
DISPATCH_SLOTS = 2 * DISPATCH_BLOCK + N_EXPERTS * SEG_ALIGN
assert SEG_ALIGN << (SEG_PIECE_BITS - 1) >= DISPATCH_BLOCK and SEG_ALIGN << SEG_PIECE_BITS >= TM_MOE
VMEM_LIMIT = 56 * 1024 * 1024


def _cparams(sem):
    return pltpu.CompilerParams(dimension_semantics=sem, vmem_limit_bytes=VMEM_LIMIT)


def _log_sigmoid(x):
    return jnp.minimum(x, 0.0) - jnp.log(1.0 + jnp.exp(-jnp.abs(x)))


def _sigmoid(x):
    return 1.0 / (1.0 + jnp.exp(-x))


def _layer_norm(v, g, b):
    mu = jnp.mean(v, axis=-1, keepdims=True)
    d = v - mu
    var = jnp.mean(d * d, axis=-1, keepdims=True)
    return d * lax.rsqrt(var + EPS) * g + b


def _tril(n):
    r = lax.broadcasted_iota(jnp.int32, (n, n), 0)
    c = lax.broadcasted_iota(jnp.int32, (n, n), 1)
    return c <= r


def _mod_kernel(c_ref, w_ref, b_ref, o_ref):
    c = c_ref[...]
    c_act = (c * _sigmoid(c)).astype(BF16)
    o_ref[0, 0] = jnp.dot(c_act, w_ref[0].astype(BF16), preferred_element_type=F32) + b_ref[0, 0]


def _mod_call(c, w_ada, b_ada):
    depth, d, _ = w_ada.shape
    bsz = c.shape[0]
    b3 = b_ada.reshape(depth, 6, 1, d)
    out = pl.pallas_call(
        _mod_kernel,
        grid=(depth, 6),
        in_specs=[pl.BlockSpec((bsz, d), lambda l, j: (0, 0)),
                  pl.BlockSpec((1, d, d), lambda l, j: (l, 0, j)),
                  pl.BlockSpec((1, 1, 1, d), lambda l, j: (l, j, 0, 0))],
        out_specs=pl.BlockSpec((1, 1, bsz, d), lambda l, j: (l, j, 0, 0)),
        out_shape=jax.ShapeDtypeStruct((depth, 6, bsz, d), F32),
        compiler_params=_cparams(("arbitrary", "arbitrary")),
        name="adaln_mod",
    )(c, w_ada, b3)
    return out.transpose(0, 2, 1, 3)


def _inproj_kernel(x_ref, mod_ref, w_ref, cos_ref, sin_ref, zm_ref, zc_ref, zq_ref, zk_ref, zv_ref, zg_ref,
                   zs_ref):
    mod = mod_ref[0]
    u = (x_ref[...] * (1.0 + mod[1:2, :]) + mod[0:1, :]).astype(BF16)

    def proj(idx):
        return jnp.dot(u, w_ref[:, int(Z_OFFS[idx]):int(Z_OFFS[idx + 1])], preferred_element_type=F32)

    zm_ref[...] = proj(0).astype(BF16)
    zc_ref[...] = proj(1).astype(BF16)

    lane = lax.broadcasted_iota(jnp.int32, (1, ZQ), 1)
    first_half = (lane % ROPE_DIM) < (ROPE_DIM // 2)
    cosf = cos_ref[...]
    sinf = sin_ref[...]

    def rope(r):
        swapped = jnp.where(first_half, pltpu.roll(r, ZQ - ROPE_DIM // 2, 1), pltpu.roll(r, ROPE_DIM // 2, 1))
        return r * cosf + swapped * sinf

    zq_ref[...] = (rope(proj(2)) * (LOG2_E * DIFF_DIM ** -0.5)).astype(BF16)
    zk_ref[...] = rope(proj(3)).astype(BF16)
    lane_v = lax.broadcasted_iota(jnp.int32, (1, ZV), 1)
    ones_col = jnp.where((lane_v % LANES) == HEAD_DIM, 1.0, 0.0)
    zv_ref[...] = (proj(4) + ones_col).astype(BF16)
    zg_ref[...] = proj(5).astype(BF16)
    zs_ref[...] = proj(6)


def _inproj_call(x2, mod_l, w_l, cosf, sinf, seq):
    n, d = x2.shape
    tm = TM_PROJ
    tps = seq // tm
    row = lambda w: pl.BlockSpec((tm, w), lambda i: (i, 0))
    outs = [(ZM, BF16), (ZC, BF16), (ZQ, BF16), (ZK, BF16), (ZV, BF16), (ZG, BF16), (ZS, F32)]
    return pl.pallas_call(
        _inproj_kernel,
        grid=(n // tm,),
        in_specs=[row(d),
                  pl.BlockSpec((1, 6, d), lambda i: (i // tps, 0, 0)),
                  pl.BlockSpec((d, Z_TOTAL), lambda i: (0, 0)),
                  row(ZQ), row(ZQ)],
        out_specs=[row(w) for w, _ in outs],
        out_shape=[jax.ShapeDtypeStruct((n, w), dt) for w, dt in outs],
        compiler_params=_cparams(("parallel",)),
        name="in_proj",
    )(x2, mod_l, w_l, cosf, sinf)


def _chunk_mats(span, chunk):
    r = lax.broadcasted_iota(jnp.int32, (span, span), 0)
    c = lax.broadcasted_iota(jnp.int32, (span, span), 1)
    same = (r // chunk) == (c // chunk)
    return (same & (c <= r)).astype(F32), same.astype(F32)


def _mlstm_kernel(zm_ref, zs_ref, bif_ref, g_ref, o_ref, c_scr, m_scr):
    @pl.when(pl.program_id(1) == 0)
    def _():
        c_scr[...] = jnp.zeros_like(c_scr)
        m_scr[...] = jnp.zeros_like(m_scr)

    tb = zm_ref.shape[0]
    L = CHUNK
    nc = tb // L
    causal = _tril(L)
    cum_m, tot_m = _chunk_mats(CUM_SPAN, L)
    lane = lax.broadcasted_iota(jnp.int32, (1, LANES), 1)
    ones_col = jnp.where(lax.broadcasted_iota(jnp.int32, (1, HEAD_DIM), 1) == 0, 1.0, 0.0).astype(BF16)
    ones_blk = jnp.broadcast_to(ones_col, (L, HEAD_DIM))
    gain = g_ref[...]
    graw = zs_ref[...] + bif_ref[...]
    gp = jnp.where(lane < HEADS, graw, jnp.where(lane < 2 * HEADS, _log_sigmoid(graw), 0.0))
    cums, tots = [], []
    for i in range(tb // CUM_SPAN):
        seg = gp[i * CUM_SPAN:(i + 1) * CUM_SPAN]
        cums.append(jnp.dot(cum_m, seg, preferred_element_type=F32, precision=lax.Precision.HIGHEST))
        tots.append(jnp.dot(tot_m, seg, preferred_element_type=F32, precision=lax.Precision.HIGHEST))
    cum = jnp.concatenate(cums, axis=0)
    tot = jnp.concatenate(tots, axis=0)
    dkv = tot[:, HEADS:2 * HEADS] - cum[:, HEADS:2 * HEADS] + gp[:, 0:HEADS]
    gp_t = gp.T
    cum_t = cum.T
    osig = _sigmoid(zm_ref[:, 3 * GW:4 * GW].astype(F32))

    c_state = [c_scr[h] for h in range(HEADS)]
    m_state = [m_scr[h][:, 0:1] for h in range(HEADS)]
    rows = []
    for c in range(nc):
        rs = slice(c * L, (c + 1) * L)
        outs = []
        for h in range(HEADS):
            qh = zm_ref[rs, h * HEAD_DIM:(h + 1) * HEAD_DIM] * (HEAD_DIM ** -0.5)
            kh = zm_ref[rs, GW + h * HEAD_DIM:GW + (h + 1) * HEAD_DIM]
            vh = zm_ref[rs, 2 * GW + h * HEAD_DIM:2 * GW + (h + 1) * HEAD_DIM]
            vaug = jnp.concatenate([vh, ones_blk], axis=1)
            b_col = cum[rs, HEADS + h:HEADS + h + 1]
            b_row = cum_t[HEADS + h:HEADS + h + 1, rs]
            li_row = gp_t[h:h + 1, rs]
            dmat = jnp.where(causal, b_col - b_row + li_row, NEG_INF)
            dmax = jnp.max(dmat, axis=-1, keepdims=True)
            s = lax.dot_general(qh, kh, (((1,), (1,)), ((), ())), preferred_element_type=F32)
            p = (s * jnp.exp(dmat - dmax)).astype(BF16)
            intra = jnp.dot(p, vaug, preferred_element_type=F32)
            dk = dkv[rs, h:h + 1]
            dkmax = jnp.max(dk, axis=0, keepdims=True)
            kw = (kh.astype(F32) * jnp.exp(dk - dkmax)).astype(BF16)
            upd = lax.dot_general(kw, vaug, (((0,), (0,)), ((), ())), preferred_element_type=F32)
            m_prev = m_state[h]
            inter = b_col + m_prev
            m_row = jnp.maximum(inter, dmax)
            num = (jnp.exp(dmax - m_row) * intra
                   + jnp.exp(inter - m_row) * jnp.dot(qh, c_state[h].astype(BF16), preferred_element_type=F32))
            den = num[:, HEAD_DIM:HEAD_DIM + 1]
            hv = num[:, :HEAD_DIM] / jnp.maximum(jnp.abs(den), jnp.exp(-m_row))
            outs.append(hv * lax.rsqrt(jnp.mean(hv * hv, axis=-1, keepdims=True) + EPS) * gain)
            b_end = tot[c * L:c * L + 1, HEADS + h:HEADS + h + 1]
            m_new = jnp.maximum(b_end + m_prev, dkmax)
            c_state[h] = jnp.exp(b_end + m_prev - m_new) * c_state[h] + jnp.exp(dkmax - m_new) * upd
            m_state[h] = m_new
        rows.append(jnp.concatenate(outs, axis=1))
    for h in range(HEADS):
        c_scr[h] = c_state[h]
        m_scr[h] = jnp.broadcast_to(m_state[h], (1, LANES))
    o_ref[...] = (jnp.concatenate(rows, axis=0) * osig).astype(BF16)


def _mlstm_call(zm, zs, bif, gain, bsz, seq):
    tb = TB_SCAN
    tps = seq // tb
    return pl.pallas_call(
        _mlstm_kernel,
        grid=(bsz, tps),
        in_specs=[pl.BlockSpec((tb, ZM), lambda b, t: (b * tps + t, 0)),
                  pl.BlockSpec((tb, ZS), lambda b, t: (b * tps + t, 0)),
                  pl.BlockSpec((1, LANES), lambda b, t: (0, 0)),
                  pl.BlockSpec((1, HEAD_DIM), lambda b, t: (0, 0))],
        out_specs=pl.BlockSpec((tb, GW), lambda b, t: (b * tps + t, 0)),
        out_shape=jax.ShapeDtypeStruct((bsz * seq, GW), BF16),
        scratch_shapes=[pltpu.VMEM((HEADS, HEAD_DIM, LANES), F32), pltpu.VMEM((HEADS, 1, LANES), F32)],
        compiler_params=_cparams(("parallel", "arbitrary")),
        name="mlstm",
    )(zm, zs, bif, gain)


def _gla_kernel(zg_ref, zs_ref, wa_ref, ba_ref, g_ref, o_ref, st_scr):
    @pl.when(pl.program_id(1) == 0)
    def _():
        st_scr[...] = jnp.zeros_like(st_scr)

    tb = zg_ref.shape[0]
    lg = GLA_CHUNK
    nc = tb // lg
    causal = _tril(lg)
    cum_m, tot_m = _chunk_mats(CUM_SPAN, lg)
    gain = g_ref[...]
    la = _log_sigmoid(jnp.dot(zs_ref[...].astype(BF16), wa_ref[...], preferred_element_type=F32)
                      + ba_ref[...]) * (1.0 / GLA_TAU)
    bcs, bes = [], []
    for i in range(tb // CUM_SPAN):
        seg = la[i * CUM_SPAN:(i + 1) * CUM_SPAN]
        bcs.append(jnp.dot(cum_m, seg, preferred_element_type=F32, precision=lax.Precision.HIGHEST))
        bes.append(jnp.dot(tot_m, seg, preferred_element_type=F32, precision=lax.Precision.HIGHEST))
    bc = jnp.concatenate(bcs, axis=0)
    be = jnp.concatenate(bes, axis=0)
    q = zg_ref[:, 0:GW].astype(F32) * (HEAD_DIM ** -0.5)
    k = zg_ref[:, GW:2 * GW].astype(F32)
    qt = (q * jnp.exp(bc)).astype(BF16)
    kt = (k * jnp.exp(-bc)).astype(BF16)
    ks = (k * jnp.exp(be - bc)).astype(BF16)
    dec = jnp.exp(be)
    v = zg_ref[:, 2 * GW:3 * GW]
    r = zg_ref[:, 3 * GW:4 * GW].astype(F32)
    rgate = r * _sigmoid(r)

    st = [st_scr[h] for h in range(HEADS)]
    rows = []
    for c in range(nc):
        rs = slice(c * lg, (c + 1) * lg)
        outs = []
        for h in range(HEADS):
            ls = slice(h * HEAD_DIM, (h + 1) * HEAD_DIM)
            qh, kh, vh = qt[rs, ls], kt[rs, ls], v[rs, ls]
            att = lax.dot_general(qh, kh, (((1,), (1,)), ((), ())), preferred_element_type=F32)
            att = jnp.where(causal, att, 0.0).astype(BF16)
            upd = lax.dot_general(vh, ks[rs, ls], (((0,), (0,)), ((), ())), preferred_element_type=F32)
            o = (jnp.dot(att, vh, preferred_element_type=F32)
                 + lax.dot_general(qh, st[h].astype(BF16), (((1,), (1,)), ((), ())), preferred_element_type=F32))
            st[h] = st[h] * dec[c * lg:c * lg + 1, ls] + upd
            outs.append(o * lax.rsqrt(jnp.mean(o * o, axis=-1, keepdims=True) + EPS) * gain)
        rows.append(jnp.concatenate(outs, axis=1))
    for h in range(HEADS):
        st_scr[h] = st[h]
    o_ref[...] = (jnp.concatenate(rows, axis=0) * rgate).astype(BF16)


def _gla_call(zg, zs, wa, ba, gain, bsz, seq):
    tb = TB_SCAN
    tps = seq // tb
    return pl.pallas_call(
        _gla_kernel,
        grid=(bsz, tps),
        in_specs=[pl.BlockSpec((tb, ZG), lambda b, t: (b * tps + t, 0)),
                  pl.BlockSpec((tb, ZS), lambda b, t: (b * tps + t, 0)),
                  pl.BlockSpec((ZS, GW), lambda b, t: (0, 0)),
                  pl.BlockSpec((1, GW), lambda b, t: (0, 0)),
                  pl.BlockSpec((1, HEAD_DIM), lambda b, t: (0, 0))],
        out_specs=pl.BlockSpec((tb, GW), lambda b, t: (b * tps + t, 0)),
        out_shape=jax.ShapeDtypeStruct((bsz * seq, GW), BF16),
        scratch_shapes=[pltpu.VMEM((HEADS, HEAD_DIM, HEAD_DIM), F32)],
        compiler_params=_cparams(("parallel", "arbitrary")),
        name="gla",
    )(zg, zs, wa, ba, gain)


def _attn_kernel(lam_init, q_ref, k_ref, v_ref, lamp_ref, g_ref, o_ref, qm_scr, m_scr, acc_scr):
    tq = q_ref.shape[0]
    qi = pl.program_id(1)
    n_hc = 2 * HEADS
    lane = lax.broadcasted_iota(jnp.int32, (1, LANES), 1)
    for hc in range(n_hc):
        h, c = divmod(hc, 2)
        pair = h // 2
        lo = (h % 2) * HEAD_DIM + c * DIFF_DIM
        qp = q_ref[:, pair * LANES:(pair + 1) * LANES]
        qm_scr[hc] = jnp.where((lane >= lo) & (lane < lo + DIFF_DIM), qp, jnp.zeros_like(qp))
    m_scr[...] = jnp.full(m_scr.shape, NEG_INF, F32)
    acc_scr[...] = jnp.zeros_like(acc_scr)

    nslab = tq // LANES

    def kv_block(j, masked):
        r0 = pl.multiple_of(j * tq, tq)
        if masked:
            keep = _tril(tq)
        for hc in range(n_hc):
            h = hc // 2
            pair = h // 2
            kb = k_ref[pl.ds(r0, tq), pair * LANES:(pair + 1) * LANES]
            vb = v_ref[pl.ds(r0, tq), h * LANES:(h + 1) * LANES]
            s = lax.dot_general(qm_scr[hc], kb, (((1,), (1,)), ((), ())), preferred_element_type=F32)
            if masked:
                s = jnp.where(keep, s, NEG_INF)
            slabs = [s[:, i * LANES:(i + 1) * LANES] for i in range(nslab)]
            mx = functools.reduce(jnp.maximum, slabs)
            m_old = m_scr[hc]
            m_new = jnp.maximum(m_old, jnp.broadcast_to(jnp.max(mx, axis=-1, keepdims=True), (tq, LANES)))
            alpha = jnp.exp2(m_old - m_new)
            p = jnp.concatenate([jnp.exp2(sl - m_new) for sl in slabs], axis=1).astype(BF16)
            acc_scr[hc] = alpha * acc_scr[hc] + jnp.dot(p, vb, preferred_element_type=F32)
            m_scr[hc] = m_new

    def full_block(j, carry):
        kv_block(j, False)
        return carry

    lax.fori_loop(0, qi, full_block, 0)
    kv_block(qi, True)

    lamp = lamp_ref[...]
    lam = (jnp.exp(jnp.sum(lamp[0:1, :] * lamp[1:2, :], axis=-1, keepdims=True))
           - jnp.exp(jnp.sum(lamp[2:3, :] * lamp[3:4, :], axis=-1, keepdims=True)) + lam_init)
    gain = g_ref[...] * (1.0 - lam_init)
    outs = []
    for h in range(HEADS):
        a1 = acc_scr[2 * h]
        a2 = acc_scr[2 * h + 1]
        o = (a1[:, :HEAD_DIM] / a1[:, HEAD_DIM:HEAD_DIM + 1]
             - lam * (a2[:, :HEAD_DIM] / a2[:, HEAD_DIM:HEAD_DIM + 1]))
        outs.append(o * lax.rsqrt(jnp.mean(o * o, axis=-1, keepdims=True) + EPS) * gain)
    o_ref[...] = jnp.concatenate(outs, axis=1).astype(BF16)


def _attn_call(zq, zk, zv, lamp, gain, lam_init, bsz, seq):
    tq = TQ_ATTN
    nq = seq // tq
    return pl.pallas_call(
        functools.partial(_attn_kernel, lam_init),
        grid=(bsz, nq),
        in_specs=[pl.BlockSpec((tq, ZQ), lambda b, i: (b * nq + i, 0)),
                  pl.BlockSpec((seq, ZK), lambda b, i: (b, 0)),
                  pl.BlockSpec((seq, ZV), lambda b, i: (b, 0)),
                  pl.BlockSpec((4, LANES), lambda b, i: (0, 0)),
                  pl.BlockSpec((1, HEAD_DIM), lambda b, i: (0, 0))],
        out_specs=pl.BlockSpec((tq, GW), lambda b, i: (b * nq + i, 0)),
        out_shape=jax.ShapeDtypeStruct((bsz * seq, GW), BF16),
        scratch_shapes=[pltpu.VMEM((2 * HEADS, tq, LANES), BF16),
                        pltpu.VMEM((2 * HEADS, tq, LANES), F32),
                        pltpu.VMEM((2 * HEADS, tq, LANES), F32)],
        compiler_params=_cparams(("parallel", "arbitrary")),
        name="diff_attn",
    )(zq, zk, zv, lamp, gain)


def _outproj_kernel(tiles_per_seq, ya_ref, zc_ref, halo_ref, yc_ref, yd_ref, x_ref, mod_ref, wo_ref, cw_ref,
                    cb_ref, lng_ref, lnb_ref, rw_ref, rb_ref, x1_ref, u2_ref, rt_ref, rtc_ref, cnt_ref):
    tm = x_ref.shape[0]
    mod = mod_ref[0]
    zc = zc_ref[...].astype(F32)
    halo = halo_ref[...].astype(F32)
    first = (pl.program_id(0) % tiles_per_seq) == 0
    s_cur = zc[:, GW:2 * GW] * zc[:, 2 * GW:3 * GW]
    s_halo = jnp.where(first, 0.0, halo[:, GW:2 * GW] * halo[:, 2 * GW:3 * GW])
    s_ext = jnp.concatenate([s_halo, s_cur], axis=0)
    hw = halo.shape[0]
    cw = cw_ref[...]
    conv = (cw[0:1, :] * s_ext[hw - 2:hw - 2 + tm, :] + cw[1:2, :] * s_ext[hw - 1:hw - 1 + tm, :]
            + cw[2:3, :] * s_cur + cb_ref[...])
    yb = (zc[:, 0:GW] * conv).astype(BF16)

    y = (jnp.dot(ya_ref[...], wo_ref[0:GW, :], preferred_element_type=F32)
         + jnp.dot(yb, wo_ref[GW:2 * GW, :], preferred_element_type=F32)
         + jnp.dot(yc_ref[...], wo_ref[2 * GW:3 * GW, :], preferred_element_type=F32)
         + jnp.dot(yd_ref[...], wo_ref[3 * GW:4 * GW, :], preferred_element_type=F32))
    x1 = _layer_norm(ALPHA * x_ref[...] + (1.0 + mod[2:3, :]) * y, lng_ref[...], lnb_ref[...])
    x1_ref[...] = x1
    u2 = (x1 * (1.0 + mod[4:5, :]) + mod[3:4, :]).astype(BF16)
    u2_ref[...] = u2

    logits = lax.dot_general(rw_ref[...], u2, (((1,), (1,)), ((), ())), preferred_element_type=F32)
    scores = _sigmoid(logits)
    biased = scores + rb_ref[...]
    rows = [biased[e:e + 1, :] for e in range(N_EXPERTS)]
    srow = [scores[e:e + 1, :] for e in range(N_EXPERTS)]
    gscore = []
    for g in range(N_GROUPS):
        a = rows[g * GROUP_SIZE:(g + 1) * GROUP_SIZE]
        best = None
        for i in range(GROUP_SIZE):
            for j in range(i + 1, GROUP_SIZE):
                pair_sum = a[i] + a[j]
                best = pair_sum if best is None else jnp.maximum(best, pair_sum)
        gscore.append(best)
    gmax = functools.reduce(jnp.maximum, gscore)
    gsel = jnp.full(gmax.shape, N_GROUPS, jnp.int32)
    for g in reversed(range(N_GROUPS)):
        gsel = jnp.where(gscore[g] == gmax, g, gsel)
    masked = [jnp.where(gsel == (e // GROUP_SIZE), rows[e], NEG_INF) for e in range(N_EXPERTS)]
    m1 = functools.reduce(jnp.maximum, masked)
    i1 = jnp.full(m1.shape, N_EXPERTS, jnp.int32)
    for e in reversed(range(N_EXPERTS)):
        i1 = jnp.where(masked[e] == m1, e, i1)
    masked2 = [jnp.where(i1 == e, NEG_INF, masked[e]) for e in range(N_EXPERTS)]
    m2 = functools.reduce(jnp.maximum, masked2)
    i2 = jnp.full(m2.shape, N_EXPERTS, jnp.int32)
    for e in reversed(range(N_EXPERTS)):
        i2 = jnp.where((masked2[e] == m2) & (i1 != e), e, i2)
    sel1 = functools.reduce(lambda acc, e: acc + jnp.where(i1 == e, srow[e], 0.0), range(N_EXPERTS), 0.0)
    sel2 = functools.reduce(lambda acc, e: acc + jnp.where(i2 == e, srow[e], 0.0), range(N_EXPERTS), 0.0)
    tot = sel1 + sel2
    g1 = sel1 / tot
    g2 = sel2 / tot
    route = jnp.concatenate([i1.astype(F32), i2.astype(F32), g1, g2, jnp.zeros((4, tm), F32)], axis=0)
    rt_ref[0] = route
    rtc_ref[...] = jnp.concatenate([route, jnp.zeros((LANES - 8, tm), F32)], axis=0).T
    sub = lax.broadcasted_iota(jnp.int32, (N_EXPERTS, tm), 0)
    onehot = jnp.where((i1 == sub) | (i2 == sub), 1.0, 0.0)
    lane = lax.broadcasted_iota(jnp.int32, (1, LANES), 1)
    cnt = jnp.zeros((N_EXPERTS, LANES), F32)
    for j in range(tm // DISPATCH_BLOCK):
        cj = jnp.sum(onehot[:, j * DISPATCH_BLOCK:(j + 1) * DISPATCH_BLOCK], axis=-1, keepdims=True)
        cnt = jnp.where(lane == j, cj, cnt)
    cnt_ref[0] = cnt


def _outproj_call(ya, zc, yc, yd, x2, mod_l, wo, cw, cb, lng, lnb, rw, rb, seq):
    n, d = x2.shape
    tm = TM_PROJ
    tps = seq // tm
    hw = 16
    row = lambda w: pl.BlockSpec((tm, w), lambda i: (i, 0))
    full = lambda a: pl.BlockSpec(a.shape, lambda i: (0,) * a.ndim)
    return pl.pallas_call(
        functools.partial(_outproj_kernel, tps),
        grid=(n // tm,),
        in_specs=[row(GW), row(ZC),
                  pl.BlockSpec((hw, ZC), lambda i: (jnp.maximum(i * (tm // hw) - 1, 0), 0)),
                  row(GW), row(GW), row(d),
                  pl.BlockSpec((1, 6, d), lambda i: (i // tps, 0, 0)),
                  full(wo), full(cw), full(cb), full(lng), full(lnb), full(rw), full(rb)],
        out_specs=[row(d), row(d), pl.BlockSpec((1, 8, tm), lambda i: (i, 0, 0)), row(LANES),
                   pl.BlockSpec((1, N_EXPERTS, LANES), lambda i: (i, 0, 0))],
        out_shape=[jax.ShapeDtypeStruct((n, d), F32), jax.ShapeDtypeStruct((n, d), BF16),
                   jax.ShapeDtypeStruct((n // tm, 8, tm), F32), jax.ShapeDtypeStruct((n, LANES), F32),
                   jax.ShapeDtypeStruct((n // tm, N_EXPERTS, LANES), F32)],
        compiler_params=_cparams(("parallel",)),
        name="out_proj_router",
    )(ya, zc, zc, yc, yd, x2, mod_l, wo, cw, cb, lng, lnb, rw, rb)


def _segment_pieces(count, fn):
    for b in reversed(range(SEG_PIECE_BITS)):
        size = SEG_ALIGN << b
        shift = SEG_ALIGN.bit_length() - 1 + b
        hi = lax.shift_left(lax.shift_right_logical(count, shift + 1), shift + 1)

        @pl.when((lax.shift_right_logical(count, shift) & 1) == 1)
        def _():
            fn(hi, size)


def _slot_positions(onehot_bf16, strict_lower_bf16, transposed):
    if transposed:
        return jnp.dot(strict_lower_bf16, onehot_bf16, preferred_element_type=F32)
    return lax.dot_general(onehot_bf16, strict_lower_bf16, (((1,), (1,)), ((), ())), preferred_element_type=F32)


def _dispatch_kernel(off_s, pc_s, gb_s, gs_s, gl_s, tail_s, u_ref, rt_ref, xs_ref, slab, zbuf, sem):
    blk = pl.program_id(0)
    tb = u_ref.shape[0]
    n_slots = slab.shape[0]
    e1 = rt_ref[0, 0:1, :]
    e2 = rt_ref[0, 1:2, :]
    sub_e = lax.broadcasted_iota(jnp.int32, (N_EXPERTS, tb), 0).astype(F32)
    hit1 = e1 == sub_e
    hit2 = e2 == sub_e
    onehot = jnp.where(hit1 | hit2, 1.0, 0.0).astype(BF16)
    r = lax.broadcasted_iota(jnp.int32, (tb, tb), 0)
    c = lax.broadcasted_iota(jnp.int32, (tb, tb), 1)
    lower = jnp.where(c < r, 1.0, 0.0).astype(BF16)
    rank = _slot_positions(onehot, lower, transposed=False)
    sub1 = lax.broadcasted_iota(jnp.int32, (N_EXPERTS, 1), 0)
    offv = jnp.zeros((N_EXPERTS, 1), F32)
    for e in range(N_EXPERTS):
        offv = jnp.where(sub1 == e, off_s[blk * N_EXPERTS + e].astype(F32), offv)
    pos = offv + rank
    pos1 = jnp.sum(jnp.where(hit1, pos, 0.0), axis=0, keepdims=True)
    pos2 = jnp.sum(jnp.where(hit2, pos, 0.0), axis=0, keepdims=True)
    slot = lax.broadcasted_iota(jnp.int32, (n_slots, tb), 0).astype(F32)
    perm = jnp.where((slot == pos1) | (slot == pos2), 1.0, 0.0).astype(BF16)
    slab[...] = jnp.dot(perm, u_ref[...], preferred_element_type=F32)

    @pl.when(blk == 0)
    def _():
        zbuf[...] = jnp.zeros_like(zbuf)

    def copies(go):
        for e in range(N_EXPERTS):
            idx = blk * N_EXPERTS + e
            src0, dst0 = off_s[idx], gb_s[idx]

            def piece(hi, size, src0=src0, dst0=dst0):
                go(pltpu.make_async_copy(slab.at[pl.ds(pl.multiple_of(src0 + hi, SEG_ALIGN), size)],
                                         xs_ref.at[pl.ds(pl.multiple_of(dst0 + hi, SEG_ALIGN), size)], sem))
            _segment_pieces(pc_s[idx], piece)

        @pl.when(blk == 0)
        def _():
            for e in range(N_EXPERTS):
                dst0 = gs_s[e]

                def piece(hi, size, dst0=dst0):
                    go(pltpu.make_async_copy(zbuf.at[pl.ds(0, size)],
                                             xs_ref.at[pl.ds(pl.multiple_of(dst0 + hi, SEG_ALIGN), size)], sem))
                _segment_pieces(gl_s[e], piece)

            zrows = zbuf.shape[0]

            def tail_chunk(i, carry):
                go(pltpu.make_async_copy(zbuf, xs_ref.at[pl.ds(pl.multiple_of(tail_s[0] + i * zrows, zrows), zrows)],
                                         sem))
                return carry
            lax.fori_loop(0, tail_s[1], tail_chunk, 0)

    copies(lambda cp: cp.start())
    copies(lambda cp: cp.wait())


def _dispatch_call(meta, u2, rt, n_rows):
    n, d = u2.shape
    tb = DISPATCH_BLOCK
    per_tile = rt.shape[2] // tb
    grid_spec = pltpu.PrefetchScalarGridSpec(
        num_scalar_prefetch=6,
        grid=(n // tb,),
        in_specs=[pl.BlockSpec((tb, d), lambda i, *_: (i, 0)),
                  pl.BlockSpec((1, 8, tb), lambda i, *_: (i // per_tile, 0, i % per_tile))],
        out_specs=pl.BlockSpec(memory_space=pl.ANY),
        scratch_shapes=[pltpu.VMEM((DISPATCH_SLOTS, d), F32), pltpu.VMEM((SEG_ALIGN << (SEG_PIECE_BITS - 1), d), F32),
                        pltpu.SemaphoreType.DMA],
    )
    return pl.pallas_call(
        _dispatch_kernel,
        grid_spec=grid_spec,
        out_shape=jax.ShapeDtypeStruct((n_rows, d), F32),
        compiler_params=_cparams(("arbitrary",)),
        name="moe_dispatch",
    )(meta["off"], meta["pc"], meta["gbase"], meta["gap_start"], meta["gap_len"], meta["tail"], u2, rt)


def _experts_kernel(te_s, na_s, x_ref, wg_ref, wu_ref, wd_ref, y_ref):
    active = pl.program_id(0) < na_s[0]

    @pl.when(active)
    def _():
        x = x_ref[...].astype(BF16)
        gate = jnp.dot(x, wg_ref[0], preferred_element_type=F32)
        up = jnp.dot(x, wu_ref[0], preferred_element_type=F32)
        hidden = (gate * _sigmoid(gate) * up).astype(BF16)
        y_ref[...] = jnp.dot(hidden, wd_ref[0], preferred_element_type=F32)

    @pl.when(jnp.logical_not(active))
    def _():
        y_ref[...] = jnp.zeros_like(y_ref)


def _experts_call(meta, xs, wg, wu, wd):
    n_rows, d = xs.shape
    tm = TM_MOE
    _, _, de = wg.shape
    tile = lambda i, te, na: jnp.minimum(i, na[0] - 1)
    grid_spec = pltpu.PrefetchScalarGridSpec(
        num_scalar_prefetch=2,
        grid=(n_rows // tm,),
        in_specs=[pl.BlockSpec((tm, d), lambda i, te, na: (tile(i, te, na), 0)),
                  pl.BlockSpec((1, d, de), lambda i, te, na: (te[tile(i, te, na)], 0, 0)),
                  pl.BlockSpec((1, d, de), lambda i, te, na: (te[tile(i, te, na)], 0, 0)),
                  pl.BlockSpec((1, de, d), lambda i, te, na: (te[tile(i, te, na)], 0, 0))],
        out_specs=pl.BlockSpec((tm, d), lambda i, te, na: (i, 0)),
    )
    return pl.pallas_call(
        _experts_kernel,
        grid_spec=grid_spec,
        out_shape=jax.ShapeDtypeStruct((n_rows, d), F32),
        compiler_params=_cparams(("arbitrary",)),
        name="experts",
    )(meta["tile_expert"], meta["n_active"], xs, wg, wu, wd)


def _combine_kernel(off_s, pc_s, gb_s, rtc_ref, x1_ref, mod_ref, lng_ref, lnb_ref, ys_ref, o_ref, slab, sem):
    blk = pl.program_id(0)
    tb = x1_ref.shape[0]
    n_slots = slab.shape[0]

    @pl.when(blk == 0)
    def _():
        slab[...] = jnp.zeros_like(slab)

    def copies(go):
        for e in range(N_EXPERTS):
            idx = blk * N_EXPERTS + e
            dst0, src0 = off_s[idx], gb_s[idx]

            def piece(hi, size, src0=src0, dst0=dst0):
                go(pltpu.make_async_copy(ys_ref.at[pl.ds(pl.multiple_of(src0 + hi, SEG_ALIGN), size)],
                                         slab.at[pl.ds(pl.multiple_of(dst0 + hi, SEG_ALIGN), size)], sem))
            _segment_pieces(pc_s[idx], piece)

    copies(lambda cp: cp.start())

    rtc = rtc_ref[...]
    e1, e2, g1, g2 = rtc[:, 0:1], rtc[:, 1:2], rtc[:, 2:3], rtc[:, 3:4]
    lane_e = lax.broadcasted_iota(jnp.int32, (1, LANES), 1).astype(F32)
    hit1 = e1 == lane_e
    hit2 = e2 == lane_e
    onehot = jnp.where(hit1 | hit2, 1.0, 0.0).astype(BF16)
    r = lax.broadcasted_iota(jnp.int32, (tb, tb), 0)
    c = lax.broadcasted_iota(jnp.int32, (tb, tb), 1)
    lower = jnp.where(c < r, 1.0, 0.0).astype(BF16)
    rank = _slot_positions(onehot, lower, transposed=True)
    lane1 = lax.broadcasted_iota(jnp.int32, (1, LANES), 1)
    offv = jnp.zeros((1, LANES), F32)
    for e in range(N_EXPERTS):
        offv = jnp.where(lane1 == e, off_s[blk * N_EXPERTS + e].astype(F32), offv)
    pos = offv + rank
    pos1 = jnp.sum(jnp.where(hit1, pos, 0.0), axis=-1, keepdims=True)
    pos2 = jnp.sum(jnp.where(hit2, pos, 0.0), axis=-1, keepdims=True)
    slot = lax.broadcasted_iota(jnp.int32, (1, n_slots), 1).astype(F32)
    sel1 = jnp.where(slot == pos1, 1.0, 0.0).astype(BF16)
    sel2 = jnp.where(slot == pos2, 1.0, 0.0).astype(BF16)

    copies(lambda cp: cp.wait())
    ys = slab[...].astype(BF16)
    y = (g1 * jnp.dot(sel1, ys, preferred_element_type=F32) + g2 * jnp.dot(sel2, ys, preferred_element_type=F32))
    mod = mod_ref[0]
    o_ref[...] = _layer_norm(ALPHA * x1_ref[...] + (1.0 + mod[5:6, :]) * y, lng_ref[...], lnb_ref[...])


def _combine_call(meta, ys, rtc, x1, mod_l, lng, lnb, seq):
    n, d = x1.shape
    tb = DISPATCH_BLOCK
    bps = seq // tb
    grid_spec = pltpu.PrefetchScalarGridSpec(
        num_scalar_prefetch=3,
        grid=(n // tb,),
        in_specs=[pl.BlockSpec((tb, LANES), lambda i, *_: (i, 0)),
                  pl.BlockSpec((tb, d), lambda i, *_: (i, 0)),
                  pl.BlockSpec((1, 6, d), lambda i, *_: (i // bps, 0, 0)),
                  pl.BlockSpec((1, d), lambda i, *_: (0, 0)),
                  pl.BlockSpec((1, d), lambda i, *_: (0, 0)),
                  pl.BlockSpec(memory_space=pl.ANY)],
        out_specs=pl.BlockSpec((tb, d), lambda i, *_: (i, 0)),
        scratch_shapes=[pltpu.VMEM((DISPATCH_SLOTS, d), F32), pltpu.SemaphoreType.DMA],
    )
    return pl.pallas_call(
        _combine_kernel,
        grid_spec=grid_spec,
        out_shape=jax.ShapeDtypeStruct((n, d), F32),
        compiler_params=_cparams(("arbitrary",)),
        name="moe_combine_ln2",
    )(meta["off"], meta["pc"], meta["gbase"], rtc, x1, mod_l, lng, lnb, ys)


def _sorted_rows_bound(n_tokens):
    n_blocks = n_tokens // DISPATCH_BLOCK
    worst = 2 * n_tokens + n_blocks * N_EXPERTS * (SEG_ALIGN - 1) + N_EXPERTS * (TM_MOE - 1)
    return -(-worst // TM_MOE) * TM_MOE


def _route_meta(cnt, n_rows):
    pc = (cnt + SEG_ALIGN - 1) // SEG_ALIGN * SEG_ALIGN
    off = jnp.cumsum(pc, axis=1) - pc
    tot = pc.sum(axis=0)
    reg = (tot + TM_MOE - 1) // TM_MOE * TM_MOE
    reg_end = jnp.cumsum(reg)
    ebase = reg_end - reg
    gbase = ebase[None, :] + jnp.cumsum(pc, axis=0) - pc
    tiles = jnp.arange(n_rows // TM_MOE, dtype=jnp.int32)
    tile_expert = jnp.minimum(jnp.searchsorted(reg_end // TM_MOE, tiles, side="right"), N_EXPERTS - 1)
    i32 = lambda a: a.astype(jnp.int32).reshape(-1)
    used = reg_end[-1:]
    tail_chunk = SEG_ALIGN << (SEG_PIECE_BITS - 1)
    return {"off": i32(off), "pc": i32(pc), "gbase": i32(gbase), "gap_start": i32(ebase + tot),
            "gap_len": i32(reg - tot), "tail": i32(jnp.concatenate([used, (n_rows - used) // tail_chunk])),
            "tile_expert": i32(tile_expert), "n_active": i32(used // TM_MOE)}


def _rearranged_w_in(w_in_l):
    d = w_in_l.shape[0]
    zeros = lambda w: jnp.zeros((d, w), w_in_l.dtype)
    o_gate, o_cb, o_dq, o_dk, o_dv, o_g, o_ga = 1024, 1032, 1800, 2056, 2312, 2568, 3592
    v_cols = []
    for h in range(HEADS):
        v_cols += [w_in_l[:, o_dv + h * HEAD_DIM:o_dv + (h + 1) * HEAD_DIM], zeros(LANES - HEAD_DIM)]
    small = [w_in_l[:, o_gate:o_gate + 2 * HEADS], w_in_l[:, o_ga:o_ga + GLA_RANK],
             zeros(ZS - 2 * HEADS - GLA_RANK)]
    cols = [w_in_l[:, 0:ZM], w_in_l[:, o_cb:o_cb + ZC], w_in_l[:, o_dq:o_dq + ZQ], w_in_l[:, o_dk:o_dk + ZK]]
    cols += v_cols + [w_in_l[:, o_g:o_g + ZG]] + small
    return jnp.concatenate(cols, axis=1).astype(BF16)


def _rope_tables(positions):
    inv_freq = ROPE_THETA ** (-jnp.arange(0, ROPE_DIM, 2, dtype=F32) / ROPE_DIM)
    ang = positions.astype(F32).reshape(-1, 1) * inv_freq
    cos, sin = jnp.cos(ang), jnp.sin(ang)
    d = np.arange(ZQ) % DIFF_DIM
    half = ROPE_DIM // 2
    idx = d % half
    cosf = jnp.where(d < ROPE_DIM, cos[:, idx], 1.0)
    sinf = jnp.where(d < half, -sin[:, idx], jnp.where(d < ROPE_DIM, sin[:, idx], 0.0))
    return cosf, sinf


def _pad_lanes(v, width=LANES):
    v = v.reshape(1, -1)
    return jnp.pad(v, ((0, 0), (0, width - v.shape[1])))


def kernel(x, c, positions, w_ada, b_ada, w_in, w_out, mlstm_b_i, mlstm_b_f, mlstm_norm_g, conv_w, conv_b, diff_lambda_q1, diff_lambda_k1, diff_lambda_q2, diff_lambda_k2, diff_norm_g, gla_w_a2, gla_b_a, gla_norm_g, ln1_g, ln1_b, ln2_g, ln2_b, router_w, router_bias, exp_w_gate, exp_w_up, exp_w_down):
    bsz, seq, d = x.shape
    depth = w_in.shape[0]
    n = bsz * seq
    x2 = x.reshape(n, d)
    mod = _mod_call(c, w_ada, b_ada)
    cosf, sinf = _rope_tables(positions)
    rw_t = router_w.T.astype(BF16)
    rb = router_bias.reshape(N_EXPERTS, 1).astype(F32)
    n_sorted = _sorted_rows_bound(n)
    for l in range(depth):
        lam_init = 0.8 - 0.6 * math.exp(-0.3 * l)
        w_l = _rearranged_w_in(w_in[l])
        zm, zc, zq, zk, zv, zg, zs = _inproj_call(x2, mod[l], w_l, cosf, sinf, seq)
        bif = _pad_lanes(jnp.concatenate([mlstm_b_i[l], mlstm_b_f[l]]))
        ya = _mlstm_call(zm, zs, bif, mlstm_norm_g[l].reshape(1, -1), bsz, seq)
        lamp = jnp.concatenate([_pad_lanes(diff_lambda_q1[l]), _pad_lanes(diff_lambda_k1[l]),
                                _pad_lanes(diff_lambda_q2[l]), _pad_lanes(diff_lambda_k2[l])], axis=0)
        yc = _attn_call(zq, zk, zv, lamp, diff_norm_g[l].reshape(1, -1), lam_init, bsz, seq)
        wa = jnp.pad(gla_w_a2[l], ((2 * HEADS, ZS - 2 * HEADS - GLA_RANK), (0, 0))).astype(BF16)
        yd = _gla_call(zg, zs, wa, gla_b_a[l].reshape(1, -1), gla_norm_g[l].reshape(1, -1), bsz, seq)
        x1, u2, rt, rtc, cnt = _outproj_call(ya, zc, yc, yd, x2, mod[l], w_out[l].astype(BF16), conv_w[l],
                                             conv_b[l].reshape(1, -1), ln1_g[l].reshape(1, -1),
                                             ln1_b[l].reshape(1, -1), rw_t, rb, seq)
        per_tile = TM_PROJ // DISPATCH_BLOCK
        cnt_blk = cnt[:, :, :per_tile].transpose(0, 2, 1).reshape(-1, N_EXPERTS).astype(jnp.int32)
        meta = _route_meta(cnt_blk, n_sorted)
        xs = _dispatch_call(meta, u2, rt, n_sorted)
        ys = _experts_call(meta, xs, exp_w_gate[l].astype(BF16), exp_w_up[l].astype(BF16),
                           exp_w_down[l].astype(BF16))
        x2 = _combine_call(meta, ys, rtc, x1, mod[l], ln2_g[l].reshape(1, -1), ln2_b[l].reshape(1, -1), seq)
    return x2.reshape(bsz, seq, d)
```

```python
import functools
import math

import numpy as np
import jax
import jax.numpy as jnp
from jax import lax
from jax.experimental import pallas as pl
from jax.experimental.pallas import tpu as pltpu

F32 = jnp.float32
BF16 = jnp.bfloat16

D_MODEL = 1024
DEPTH = 4
HEADS = 4
GW = 256
HEAD_DIM = 64
DIFF_DIM = 32
ROPE_DIM = 8
ROPE_THETA = 500000.0
CONV_WIDTH = 3
GLA_RANK = 16
GLA_TAU = 16.0
CHUNK = 64
GLA_CHUNK = 64
N_EXPERTS = 16
N_GROUPS = 4
GROUP_SIZE = 4
D_EXPERT = 512
ALPHA = (2 * DEPTH) ** 0.25
EPS = 1e-6
LANES = 128
NEG_INF = float("-inf")
LOG2_E = math.log2(math.e)

ZM, ZC, ZQ, ZK, ZV, ZG, ZS = 1024, 768, 256, 256, 512, 1024, 128
Z_OFFS = np.cumsum([0, ZM, ZC, ZQ, ZK, ZV, ZG, ZS])
Z_TOTAL = int(Z_OFFS[-1])

TM_PROJ = 512
TB_MLSTM = 256
TB_GLA = 512
CUM_SPAN = 128
TQ_ATTN = 512
TM_MOE = 512
DISPATCH_BLOCK = 256
SEG_ALIGN = 8
SEG_PIECE_BITS = 6
DISPATCH_SLOTS = 2 * DISPATCH_BLOCK + N_EXPERTS * SEG_ALIGN
assert SEG_ALIGN << (SEG_PIECE_BITS - 1) >= DISPATCH_BLOCK and SEG_ALIGN << SEG_PIECE_BITS >= TM_MOE
VMEM_LIMIT = 56 * 1024 * 1024


def _cparams(sem):
    return pltpu.CompilerParams(dimension_semantics=sem, vmem_limit_bytes=VMEM_LIMIT)


def _log_sigmoid(x):
    return jnp.minimum(x, 0.0) - jnp.log(1.0 + jnp.exp(-jnp.abs(x)))


def _sigmoid(x):
    return 1.0 / (1.0 + jnp.exp(-x))


def _layer_norm(v, g, b):
    mu = jnp.mean(v, axis=-1, keepdims=True)
    d = v - mu
    var = jnp.mean(d * d, axis=-1, keepdims=True)
    return d * lax.rsqrt(var + EPS) * g + b


def _tril(n):
    r = lax.broadcasted_iota(jnp.int32, (n, n), 0)
    c = lax.broadcasted_iota(jnp.int32, (n, n), 1)
    return c <= r


def _mod_kernel(c_ref, w_ref, b_ref, o_ref):
    c = c_ref[...]
    c_act = (c * _sigmoid(c)).astype(BF16)
    o_ref[0, 0] = jnp.dot(c_act, w_ref[0].astype(BF16), preferred_element_type=F32) + b_ref[0, 0]


def _mod_call(c, w_ada, b_ada):
    depth, d, _ = w_ada.shape
    bsz = c.shape[0]
    b3 = b_ada.reshape(depth, 6, 1, d)
    out = pl.pallas_call(
        _mod_kernel,
        grid=(depth, 6),
        in_specs=[pl.BlockSpec((bsz, d), lambda l, j: (0, 0)),
                  pl.BlockSpec((1, d, d), lambda l, j: (l, 0, j)),
                  pl.BlockSpec((1, 1, 1, d), lambda l, j: (l, j, 0, 0))],
        out_specs=pl.BlockSpec((1, 1, bsz, d), lambda l, j: (l, j, 0, 0)),
        out_shape=jax.ShapeDtypeStruct((depth, 6, bsz, d), F32),
        compiler_params=_cparams(("arbitrary", "arbitrary")),
        name="adaln_mod",
    )(c, w_ada, b3)
    return out.transpose(0, 2, 1, 3)


def _inproj_kernel(x_ref, mod_ref, w_ref, cos_ref, sin_ref, zm_ref, zc_ref, zq_ref, zk_ref, zv_ref, zg_ref,
                   zs_ref):
    mod = mod_ref[0]
    u = (x_ref[...] * (1.0 + mod[1:2, :]) + mod[0:1, :]).astype(BF16)

    def proj(idx):
        return jnp.dot(u, w_ref[:, int(Z_OFFS[idx]):int(Z_OFFS[idx + 1])], preferred_element_type=F32)

    zm_ref[...] = proj(0).astype(BF16)
    zc_ref[...] = proj(1).astype(BF16)

    lane = lax.broadcasted_iota(jnp.int32, (1, ZQ), 1)
    first_half = (lane % ROPE_DIM) < (ROPE_DIM // 2)
    cosf = cos_ref[...]
    sinf = sin_ref[...]

    def rope(r):
        swapped = jnp.where(first_half, pltpu.roll(r, ZQ - ROPE_DIM // 2, 1), pltpu.roll(r, ROPE_DIM // 2, 1))
        return r * cosf + swapped * sinf

    zq_ref[...] = (rope(proj(2)) * (LOG2_E * DIFF_DIM ** -0.5)).astype(BF16)
    zk_ref[...] = rope(proj(3)).astype(BF16)
    lane_v = lax.broadcasted_iota(jnp.int32, (1, ZV), 1)
    ones_col = jnp.where((lane_v % LANES) == HEAD_DIM, 1.0, 0.0)
    zv_ref[...] = (proj(4) + ones_col).astype(BF16)
    zg_ref[...] = proj(5).astype(BF16)
    zs_ref[...] = proj(6)


def _inproj_call(x2, mod_l, w_l, cosf, sinf, seq):
    n, d = x2.shape
    tm = TM_PROJ
    tps = seq // tm
    row = lambda w: pl.BlockSpec((tm, w), lambda i: (i, 0))
    outs = [(ZM, BF16), (ZC, BF16), (ZQ, BF16), (ZK, BF16), (ZV, BF16), (ZG, BF16), (ZS, F32)]
    return pl.pallas_call(
        _inproj_kernel,
        grid=(n // tm,),
        in_specs=[row(d),
                  pl.BlockSpec((1, 6, d), lambda i: (i // tps, 0, 0)),
                  pl.BlockSpec((d, Z_TOTAL), lambda i: (0, 0)),
                  row(ZQ), row(ZQ)],
        out_specs=[row(w) for w, _ in outs],
        out_shape=[jax.ShapeDtypeStruct((n, w), dt) for w, dt in outs],
        compiler_params=_cparams(("parallel",)),
        name="in_proj",
    )(x2, mod_l, w_l, cosf, sinf)


NT_DIMS = (((1,), (1,)), ((), ()))
TN_DIMS = (((0,), (0,)), ((), ()))


def _head_of(idx):
    return idx // HEAD_DIM


def _cumsum_matrix(span, chunk):
    r = lax.broadcasted_iota(jnp.int32, (span, span), 0)
    c = lax.broadcasted_iota(jnp.int32, (span, span), 1)
    return jnp.where(((r // chunk) == (c // chunk)) & (c <= r), 1.0, 0.0).astype(BF16)


def _bf16_terms(x, n_terms):
    terms, rest = [], x
    for _ in range(n_terms):
        t = rest.astype(BF16)
        terms.append(t)
        rest = rest - t.astype(F32)
    return terms


def _exact_matmul(lhs_f32, rhs01, n_terms=3):
    return functools.reduce(lambda a, b: a + b, [jnp.dot(t, rhs01, preferred_element_type=F32)
                                                 for t in reversed(_bf16_terms(lhs_f32, n_terms))])


def _chunk_cumsum(x, chunk):
    cm = _cumsum_matrix(CUM_SPAN, chunk)
    outs = []
    for i in range(x.shape[0] // CUM_SPAN):
        terms = _bf16_terms(x[i * CUM_SPAN:(i + 1) * CUM_SPAN], 3)
        outs.append(functools.reduce(lambda a, b: a + b,
                                     [jnp.dot(cm, t, preferred_element_type=F32) for t in reversed(terms)]))
    return jnp.concatenate(outs, axis=0)


def _rows_broadcast(x, chunk, row):
    n = x.shape[0] // chunk
    return jnp.concatenate([jnp.broadcast_to(x[c * chunk + row:c * chunk + row + 1, :], (chunk, x.shape[1]))
                            for c in range(n)], axis=0)


def _head_masks(chunk):
    lane_head = _head_of(lax.broadcasted_iota(jnp.int32, (1, GW), 1))
    row_head = lax.broadcasted_iota(jnp.int32, (HEADS * chunk, 1), 0) // chunk
    stack_keep = row_head == lane_head
    rr = lax.broadcasted_iota(jnp.int32, (HEADS * chunk, chunk), 0) % chunk
    cc = lax.broadcasted_iota(jnp.int32, (HEADS * chunk, chunk), 1)
    block_diag = (_head_of(lax.broadcasted_iota(jnp.int32, (GW, GW), 0))
                  == _head_of(lax.broadcasted_iota(jnp.int32, (GW, GW), 1)))
    return stack_keep, cc <= rr, block_diag


def _unstack_heads(x4, keep, chunk):
    x4 = jnp.where(keep, x4, 0.0)
    return functools.reduce(lambda a, b: a + b, [x4[h * chunk:(h + 1) * chunk] for h in range(HEADS)])


def _head_rms_norm(x, block_diag):
    seg = jnp.where(block_diag, 1.0, 0.0).astype(BF16)
    return x * lax.rsqrt(_exact_matmul(x * x, seg, 2) * (1.0 / HEAD_DIM) + EPS)


def _mlstm_kernel(zm_ref, zs_ref, bif_ref, g_ref, o_ref, ct_scr, nm_scr, ni_scr, up_scr):
    @pl.when(pl.program_id(1) == 0)
    def _():
        ct_scr[...] = jnp.zeros_like(ct_scr)
        nm_scr[...] = jnp.zeros_like(nm_scr)

    tb = zm_ref.shape[0]
    L = CHUNK
    nc = tb // L
    lane = lax.broadcasted_iota(jnp.int32, (1, LANES), 1)
    graw = zs_ref[...] + bif_ref[...]
    gp = jnp.where(lane < HEADS, graw, jnp.where(lane < 2 * HEADS, _log_sigmoid(graw), 0.0))
    cum = _chunk_cumsum(gp, L)

    def expand(x, first_lane):
        sel = jnp.where((lax.broadcasted_iota(jnp.int32, (LANES, GW), 0) - first_lane)
                        == _head_of(lax.broadcasted_iota(jnp.int32, (LANES, GW), 1)), 1.0, 0.0).astype(BF16)
        return _exact_matmul(x, sel)

    bx = expand(cum, HEADS)
    lix = expand(gp, 0)
    bex = _rows_broadcast(bx, L, L - 1)
    dkx = bex - bx + lix
    dkmax = [jnp.max(dkx[c * L:(c + 1) * L], axis=0, keepdims=True) for c in range(nc)]
    kw = (zm_ref[:, GW:2 * GW].astype(F32)
          * jnp.exp(dkx - jnp.concatenate([jnp.broadcast_to(d, (L, GW)) for d in dkmax], axis=0))).astype(BF16)
    rowv_t = (cum - pltpu.roll(gp, HEADS, 1)).T
    stack_keep, causal4, block_diag = _head_masks(L)
    stack_keep2 = jnp.concatenate([stack_keep, stack_keep], axis=1)
    ones_blk = jnp.ones((L, GW), BF16)

    dmaxx = []
    for c in range(nc):
        rs = slice(c * L, (c + 1) * L)
        qc = zm_ref[rs, 0:GW] * (HEAD_DIM ** -0.5)
        kc = zm_ref[rs, GW:2 * GW]
        vc = zm_ref[rs, 2 * GW:3 * GW]
        q4 = jnp.where(stack_keep, jnp.concatenate([qc] * HEADS, axis=0), jnp.zeros((), BF16))
        s4 = lax.dot_general(q4, kc, NT_DIMS, preferred_element_type=F32)
        bcol4 = jnp.concatenate([cum[rs, HEADS + h:HEADS + h + 1] for h in range(HEADS)], axis=0)
        brow4 = jnp.concatenate([jnp.broadcast_to(rowv_t[HEADS + h:HEADS + h + 1, rs], (L, L))
                                 for h in range(HEADS)], axis=0)
        d4 = jnp.where(causal4, bcol4 - brow4, NEG_INF)
        dmax4 = jnp.max(d4, axis=-1, keepdims=True)
        p4 = (s4 * jnp.exp(d4 - dmax4)).astype(BF16)
        i4 = jnp.dot(p4, jnp.concatenate([vc, ones_blk], axis=1), preferred_element_type=F32)
        ni_scr[rs, :] = _unstack_heads(i4, stack_keep2, L)
        dmaxx.append(_unstack_heads(jnp.broadcast_to(dmax4, (HEADS * L, GW)), stack_keep, L))
        up = lax.dot_general(vc, kw[rs], TN_DIMS, preferred_element_type=F32)
        up_scr[c] = jnp.where(block_diag, up, 0.0)

    ct = ct_scr[...]
    n_row = nm_scr[0:1, :]
    m_prev = nm_scr[1:2, :]
    hv_rows = []
    for c in range(nc):
        rs = slice(c * L, (c + 1) * L)
        qc = zm_ref[rs, 0:GW] * (HEAD_DIM ** -0.5)
        nmat = jnp.where(block_diag, jnp.broadcast_to(n_row, (GW, GW)), 0.0).astype(BF16)
        num_inter = lax.dot_general(qc, ct.astype(BF16), NT_DIMS, preferred_element_type=F32)
        den_inter = lax.dot_general(qc, nmat, NT_DIMS, preferred_element_type=F32)
        inter = bx[rs] + m_prev
        m_row = jnp.maximum(inter, dmaxx[c])
        e1 = jnp.exp(dmaxx[c] - m_row)
        e2 = jnp.exp(inter - m_row)
        num = e1 * ni_scr[rs, 0:GW] + e2 * num_inter
        den = e1 * ni_scr[rs, GW:2 * GW] + e2 * den_inter
        hv_rows.append(num / jnp.maximum(jnp.abs(den), jnp.exp(-m_row)))
        b_end = bex[c * L:c * L + 1, :]
        m_new = jnp.maximum(b_end + m_prev, dkmax[c])
        decay = jnp.exp(b_end + m_prev - m_new)
        f2 = jnp.exp(dkmax[c] - m_new)
        ct = ct * decay + up_scr[c] * f2
        n_row = n_row * decay + f2 * jnp.sum(kw[rs].astype(F32), axis=0, keepdims=True)
        m_prev = m_new
    ct_scr[...] = ct
    nm_scr[0:1, :] = n_row
    nm_scr[1:2, :] = m_prev

    hn = _head_rms_norm(jnp.concatenate(hv_rows, axis=0), block_diag)
    o_ref[...] = (hn * g_ref[...] * _sigmoid(zm_ref[:, 3 * GW:4 * GW].astype(F32))).astype(BF16)


def _mlstm_call(zm, zs, bif, gain, bsz, seq):
    tb = TB_MLSTM
    tps = seq // tb
    return pl.pallas_call(
        _mlstm_kernel,
        grid=(bsz, tps),
        in_specs=[pl.BlockSpec((tb, ZM), lambda b, t: (b * tps + t, 0)),
                  pl.BlockSpec((tb, ZS), lambda b, t: (b * tps + t, 0)),
                  pl.BlockSpec((1, LANES), lambda b, t: (0, 0)),
                  pl.BlockSpec((1, GW), lambda b, t: (0, 0))],
        out_specs=pl.BlockSpec((tb, GW), lambda b, t: (b * tps + t, 0)),
        out_shape=jax.ShapeDtypeStruct((bsz * seq, GW), BF16),
        scratch_shapes=[pltpu.VMEM((GW, GW), F32),
                        pltpu.VMEM((8, GW), F32),
                        pltpu.VMEM((tb, 2 * GW), F32),
                        pltpu.VMEM((tb // CHUNK, GW, GW), F32)],
        compiler_params=_cparams(("parallel", "arbitrary")),
        name="mlstm",
    )(zm, zs, bif, jnp.tile(gain, (1, HEADS)))


def _gla_kernel(zg_ref, zs_ref, wa_ref, ba_ref, g_ref, o_ref, st_scr, oi_scr, up_scr):
    @pl.when(pl.program_id(1) == 0)
    def _():
        st_scr[...] = jnp.zeros_like(st_scr)

    tb = zg_ref.shape[0]
    lg = GLA_CHUNK
    nc = tb // lg
    la = _log_sigmoid(jnp.dot(zs_ref[...].astype(BF16), wa_ref[...], preferred_element_type=F32)
                      + ba_ref[...]) * (1.0 / GLA_TAU)
    bc = _chunk_cumsum(la, lg)
    be = _rows_broadcast(bc, lg, lg - 1)
    bm = _rows_broadcast(bc, lg, lg // 2)
    q = zg_ref[:, 0:GW].astype(F32) * (HEAD_DIM ** -0.5)
    k = zg_ref[:, GW:2 * GW].astype(F32)
    qm = (q * jnp.exp(bc - bm)).astype(BF16)
    km = (k * jnp.exp(bm - bc)).astype(BF16)
    qt = (q * jnp.exp(bc)).astype(BF16)
    ks = (k * jnp.exp(be - bc)).astype(BF16)
    dec = jnp.exp(be)
    stack_keep, causal4, block_diag = _head_masks(lg)

    for c in range(nc):
        rs = slice(c * lg, (c + 1) * lg)
        vc = zg_ref[rs, 2 * GW:3 * GW]
        q4 = jnp.where(stack_keep, jnp.concatenate([qm[rs]] * HEADS, axis=0), jnp.zeros((), BF16))
        att4 = lax.dot_general(q4, km[rs], NT_DIMS, preferred_element_type=F32)
        att4 = jnp.where(causal4, att4, 0.0).astype(BF16)
        oi_scr[rs, :] = _unstack_heads(jnp.dot(att4, vc, preferred_element_type=F32), stack_keep, lg)
        up = lax.dot_general(vc, ks[rs], TN_DIMS, preferred_element_type=F32)
        up_scr[c] = jnp.where(block_diag, up, 0.0)

    st = st_scr[...]
    inter = []
    for c in range(nc):
        rs = slice(c * lg, (c + 1) * lg)
        inter.append(lax.dot_general(qt[rs], st.astype(BF16), NT_DIMS, preferred_element_type=F32))
        st = st * dec[c * lg:c * lg + 1, :] + up_scr[c]
    st_scr[...] = st

    on = _head_rms_norm(oi_scr[...] + jnp.concatenate(inter, axis=0), block_diag)
    r = zg_ref[:, 3 * GW:4 * GW].astype(F32)
    o_ref[...] = (on * g_ref[...] * (r * _sigmoid(r))).astype(BF16)


def _gla_call(zg, zs, wa, ba, gain, bsz, seq):
    tb = TB_GLA
    tps = seq // tb
    return pl.pallas_call(
        _gla_kernel,
        grid=(bsz, tps),
        in_specs=[pl.BlockSpec((tb, ZG), lambda b, t: (b * tps + t, 0)),
                  pl.BlockSpec((tb, ZS), lambda b, t: (b * tps + t, 0)),
                  pl.BlockSpec((ZS, GW), lambda b, t: (0, 0)),
                  pl.BlockSpec((1, GW), lambda b, t: (0, 0)),
                  pl.BlockSpec((1, GW), lambda b, t: (0, 0))],
        out_specs=pl.BlockSpec((tb, GW), lambda b, t: (b * tps + t, 0)),
        out_shape=jax.ShapeDtypeStruct((bsz * seq, GW), BF16),
        scratch_shapes=[pltpu.VMEM((GW, GW), F32),
                        pltpu.VMEM((tb, GW), F32),
                        pltpu.VMEM((tb // GLA_CHUNK, GW, GW), F32)],
        compiler_params=_cparams(("parallel", "arbitrary")),
        name="gla",
    )(zg, zs, wa, ba, jnp.tile(gain, (1, HEADS)))


def _attn_kernel(lam_init, q_ref, k_ref, v_ref, lamp_ref, g_ref, o_ref, qm_scr, m_scr, acc_scr):
    tq = q_ref.shape[0]
    qi = pl.program_id(1)
    n_hc = 2 * HEADS
    lane = lax.broadcasted_iota(jnp.int32, (1, LANES), 1)
    for hc in range(n_hc):
        h, c = divmod(hc, 2)
        pair = h // 2
        lo = (h % 2) * HEAD_DIM + c * DIFF_DIM
        qp = q_ref[:, pair * LANES:(pair + 1) * LANES]
        qm_scr[hc] = jnp.where((lane >= lo) & (lane < lo + DIFF_DIM), qp, jnp.zeros_like(qp))
    m_scr[...] = jnp.full(m_scr.shape, NEG_INF, F32)
    acc_scr[...] = jnp.zeros_like(acc_scr)

    nslab = tq // LANES

    def kv_block(j, masked):
        r0 = pl.multiple_of(j * tq, tq)
        if masked:
            keep = _tril(tq)
        for hc in range(n_hc):
            h = hc // 2
            pair = h // 2
            kb = k_ref[pl.ds(r0, tq), pair * LANES:(pair + 1) * LANES]
            vb = v_ref[pl.ds(r0, tq), h * LANES:(h + 1) * LANES]
            s = lax.dot_general(qm_scr[hc], kb, (((1,), (1,)), ((), ())), preferred_element_type=F32)
            if masked:
                s = jnp.where(keep, s, NEG_INF)
            slabs = [s[:, i * LANES:(i + 1) * LANES] for i in range(nslab)]
            mx = functools.reduce(jnp.maximum, slabs)
            m_old = m_scr[hc]
            m_new = jnp.maximum(m_old, jnp.broadcast_to(jnp.max(mx, axis=-1, keepdims=True), (tq, LANES)))
            alpha = jnp.exp2(m_old - m_new)
            p = jnp.concatenate([jnp.exp2(sl - m_new) for sl in slabs], axis=1).astype(BF16)
            acc_scr[hc] = alpha * acc_scr[hc] + jnp.dot(p, vb, preferred_element_type=F32)
            m_scr[hc] = m_new

    def full_block(j, carry):
        kv_block(j, False)
        return carry

    lax.fori_loop(0, qi, full_block, 0)
    kv_block(qi, True)

    lamp = lamp_ref[...]
    lam = (jnp.exp(jnp.sum(lamp[0:1, :] * lamp[1:2, :], axis=-1, keepdims=True))
           - jnp.exp(jnp.sum(lamp[2:3, :] * lamp[3:4, :], axis=-1, keepdims=True)) + lam_init)
    gain = g_ref[...] * (1.0 - lam_init)
    outs = []
    for h in range(HEADS):
        a1 = acc_scr[2 * h]
        a2 = acc_scr[2 * h + 1]
        o = (a1[:, :HEAD_DIM] / a1[:, HEAD_DIM:HEAD_DIM + 1]
             - lam * (a2[:, :HEAD_DIM] / a2[:, HEAD_DIM:HEAD_DIM + 1]))
        outs.append(o * lax.rsqrt(jnp.mean(o * o, axis=-1, keepdims=True) + EPS) * gain)
    o_ref[...] = jnp.concatenate(outs, axis=1).astype(BF16)


def _attn_call(zq, zk, zv, lamp, gain, lam_init, bsz, seq):
    tq = TQ_ATTN
    nq = seq // tq
    return pl.pallas_call(
        functools.partial(_attn_kernel, lam_init),
        grid=(bsz, nq),
        in_specs=[pl.BlockSpec((tq, ZQ), lambda b, i: (b * nq + i, 0)),
                  pl.BlockSpec((seq, ZK), lambda b, i: (b, 0)),
                  pl.BlockSpec((seq, ZV), lambda b, i: (b, 0)),
                  pl.BlockSpec((4, LANES), lambda b, i: (0, 0)),
                  pl.BlockSpec((1, HEAD_DIM), lambda b, i: (0, 0))],
        out_specs=pl.BlockSpec((tq, GW), lambda b, i: (b * nq + i, 0)),
        out_shape=jax.ShapeDtypeStruct((bsz * seq, GW), BF16),
        scratch_shapes=[pltpu.VMEM((2 * HEADS, tq, LANES), BF16),
                        pltpu.VMEM((2 * HEADS, tq, LANES), F32),
                        pltpu.VMEM((2 * HEADS, tq, LANES), F32)],
        compiler_params=_cparams(("parallel", "arbitrary")),
        name="diff_attn",
    )(zq, zk, zv, lamp, gain)


def _outproj_kernel(tiles_per_seq, ya_ref, zc_ref, halo_ref, yc_ref, yd_ref, x_ref, mod_ref, wo_ref, cw_ref,
                    cb_ref, lng_ref, lnb_ref, rw_ref, rb_ref, x1_ref, u2_ref, rt_ref, rtc_ref, cnt_ref):
    tm = x_ref.shape[0]
    mod = mod_ref[0]
    zc = zc_ref[...].astype(F32)
    halo = halo_ref[...].astype(F32)
    first = (pl.program_id(0) % tiles_per_seq) == 0
    s_cur = zc[:, GW:2 * GW] * zc[:, 2 * GW:3 * GW]
    s_halo = jnp.where(first, 0.0, halo[:, GW:2 * GW] * halo[:, 2 * GW:3 * GW])
    s_ext = jnp.concatenate([s_halo, s_cur], axis=0)
    hw = halo.shape[0]
    cw = cw_ref[...]
    conv = (cw[0:1, :] * s_ext[hw - 2:hw - 2 + tm, :] + cw[1:2, :] * s_ext[hw - 1:hw - 1 + tm, :]
            + cw[2:3, :] * s_cur + cb_ref[...])
    yb = (zc[:, 0:GW] * conv).astype(BF16)

    y = (jnp.dot(ya_ref[...], wo_ref[0:GW, :], preferred_element_type=F32)
         + jnp.dot(yb, wo_ref[GW:2 * GW, :], preferred_element_type=F32)
         + jnp.dot(yc_ref[...], wo_ref[2 * GW:3 * GW, :], preferred_element_type=F32)
         + jnp.dot(yd_ref[...], wo_ref[3 * GW:4 * GW, :], preferred_element_type=F32))
    x1 = _layer_norm(ALPHA * x_ref[...] + (1.0 + mod[2:3, :]) * y, lng_ref[...], lnb_ref[...])
    x1_ref[...] = x1
    u2 = (x1 * (1.0 + mod[4:5, :]) + mod[3:4, :]).astype(BF16)
    u2_ref[...] = u2

    logits = lax.dot_general(rw_ref[...], u2, (((1,), (1,)), ((), ())), preferred_element_type=F32)
    scores = _sigmoid(logits)
    biased = scores + rb_ref[...]
    rows = [biased[e:e + 1, :] for e in range(N_EXPERTS)]
    srow = [scores[e:e + 1, :] for e in range(N_EXPERTS)]
    gscore = []
    for g in range(N_GROUPS):
        a = rows[g * GROUP_SIZE:(g + 1) * GROUP_SIZE]
        best = None
        for i in range(GROUP_SIZE):
            for j in range(i + 1, GROUP_SIZE):
                pair_sum = a[i] + a[j]
                best = pair_sum if best is None else jnp.maximum(best, pair_sum)
        gscore.append(best)
    gmax = functools.reduce(jnp.maximum, gscore)
    gsel = jnp.full(gmax.shape, N_GROUPS, jnp.int32)
    for g in reversed(range(N_GROUPS)):
        gsel = jnp.where(gscore[g] == gmax, g, gsel)
    masked = [jnp.where(gsel == (e // GROUP_SIZE), rows[e], NEG_INF) for e in range(N_EXPERTS)]
    m1 = functools.reduce(jnp.maximum, masked)
    i1 = jnp.full(m1.shape, N_EXPERTS, jnp.int32)
    for e in reversed(range(N_EXPERTS)):
        i1 = jnp.where(masked[e] == m1, e, i1)
    masked2 = [jnp.where(i1 == e, NEG_INF, masked[e]) for e in range(N_EXPERTS)]
    m2 = functools.reduce(jnp.maximum, masked2)
    i2 = jnp.full(m2.shape, N_EXPERTS, jnp.int32)
    for e in reversed(range(N_EXPERTS)):
        i2 = jnp.where((masked2[e] == m2) & (i1 != e), e, i2)
    sel1 = functools.reduce(lambda acc, e: acc + jnp.where(i1 == e, srow[e], 0.0), range(N_EXPERTS), 0.0)
    sel2 = functools.reduce(lambda acc, e: acc + jnp.where(i2 == e, srow[e], 0.0), range(N_EXPERTS), 0.0)
    tot = sel1 + sel2
    g1 = sel1 / tot
    g2 = sel2 / tot
    route = jnp.concatenate([i1.astype(F32), i2.astype(F32), g1, g2, jnp.zeros((4, tm), F32)], axis=0)
    rt_ref[0] = route
    rtc_ref[...] = jnp.concatenate([route, jnp.zeros((LANES - 8, tm), F32)], axis=0).T
    sub = lax.broadcasted_iota(jnp.int32, (N_EXPERTS, tm), 0)
    onehot = jnp.where((i1 == sub) | (i2 == sub), 1.0, 0.0)
    lane = lax.broadcasted_iota(jnp.int32, (1, LANES), 1)
    cnt = jnp.zeros((N_EXPERTS, LANES), F32)
    for j in range(tm // DISPATCH_BLOCK):
        cj = jnp.sum(onehot[:, j * DISPATCH_BLOCK:(j + 1) * DISPATCH_BLOCK], axis=-1, keepdims=True)
        cnt = jnp.where(lane == j, cj, cnt)
    cnt_ref[0] = cnt


def _outproj_call(ya, zc, yc, yd, x2, mod_l, wo, cw, cb, lng, lnb, rw, rb, seq):
    n, d = x2.shape
    tm = TM_PROJ
    tps = seq // tm
    hw = 16
    row = lambda w: pl.BlockSpec((tm, w), lambda i: (i, 0))
    full = lambda a: pl.BlockSpec(a.shape, lambda i: (0,) * a.ndim)
    return pl.pallas_call(
        functools.partial(_outproj_kernel, tps),
        grid=(n // tm,),
        in_specs=[row(GW), row(ZC),
                  pl.BlockSpec((hw, ZC), lambda i: (jnp.maximum(i * (tm // hw) - 1, 0), 0)),
                  row(GW), row(GW), row(d),
                  pl.BlockSpec((1, 6, d), lambda i: (i // tps, 0, 0)),
                  full(wo), full(cw), full(cb), full(lng), full(lnb), full(rw), full(rb)],
        out_specs=[row(d), row(d), pl.BlockSpec((1, 8, tm), lambda i: (i, 0, 0)), row(LANES),
                   pl.BlockSpec((1, N_EXPERTS, LANES), lambda i: (i, 0, 0))],
        out_shape=[jax.ShapeDtypeStruct((n, d), F32), jax.ShapeDtypeStruct((n, d), BF16),
                   jax.ShapeDtypeStruct((n // tm, 8, tm), F32), jax.ShapeDtypeStruct((n, LANES), F32),
                   jax.ShapeDtypeStruct((n // tm, N_EXPERTS, LANES), F32)],
        compiler_params=_cparams(("parallel",)),
        name="out_proj_router",
    )(ya, zc, zc, yc, yd, x2, mod_l, wo, cw, cb, lng, lnb, rw, rb)


def _segment_pieces(count, fn):
    for b in reversed(range(SEG_PIECE_BITS)):
        size = SEG_ALIGN << b
        shift = SEG_ALIGN.bit_length() - 1 + b
        hi = lax.shift_left(lax.shift_right_logical(count, shift + 1), shift + 1)

        @pl.when((lax.shift_right_logical(count, shift) & 1) == 1)
        def _():
            fn(hi, size)


def _slot_positions(onehot_bf16, strict_lower_bf16, transposed):
    if transposed:
        return jnp.dot(strict_lower_bf16, onehot_bf16, preferred_element_type=F32)
    return lax.dot_general(onehot_bf16, strict_lower_bf16, (((1,), (1,)), ((), ())), preferred_element_type=F32)


def _dispatch_kernel(off_s, pc_s, gb_s, gs_s, gl_s, tail_s, u_ref, rt_ref, xs_ref, slab, zbuf, sem):
    blk = pl.program_id(0)
    tb = u_ref.shape[0]
    n_slots = slab.shape[1]
    e1 = rt_ref[0, 0:1, :]
    e2 = rt_ref[0, 1:2, :]
    sub_e = lax.broadcasted_iota(jnp.int32, (N_EXPERTS, tb), 0).astype(F32)
    hit1 = e1 == sub_e
    hit2 = e2 == sub_e
    onehot = jnp.where(hit1 | hit2, 1.0, 0.0).astype(BF16)
    r = lax.broadcasted_iota(jnp.int32, (tb, tb), 0)
    c = lax.broadcasted_iota(jnp.int32, (tb, tb), 1)
    lower = jnp.where(c < r, 1.0, 0.0).astype(BF16)
    rank = _slot_positions(onehot, lower, transposed=False)
    sub1 = lax.broadcasted_iota(jnp.int32, (N_EXPERTS, 1), 0)
    offv = jnp.zeros((N_EXPERTS, 1), F32)
    for e in range(N_EXPERTS):
        offv = jnp.where(sub1 == e, off_s[blk * N_EXPERTS + e].astype(F32), offv)
    pos = offv + rank
    pos1 = jnp.sum(jnp.where(hit1, pos, 0.0), axis=0, keepdims=True)
    pos2 = jnp.sum(jnp.where(hit2, pos, 0.0), axis=0, keepdims=True)
    slot = lax.broadcasted_iota(jnp.int32, (n_slots, tb), 0).astype(F32)
    perm = jnp.where((slot == pos1) | (slot == pos2), 1.0, 0.0).astype(BF16)
    cur = blk % 2
    slab[cur] = jnp.dot(perm, u_ref[...], preferred_element_type=F32)

    @pl.when(blk == 0)
    def _():
        zbuf[...] = jnp.zeros_like(zbuf)

    def copies(b, slot, go):
        for e in range(N_EXPERTS):
            idx = b * N_EXPERTS + e
            src0, dst0 = off_s[idx], gb_s[idx]

            def piece(hi, size, src0=src0, dst0=dst0):
                go(pltpu.make_async_copy(slab.at[slot, pl.ds(pl.multiple_of(src0 + hi, SEG_ALIGN), size)],
                                         xs_ref.at[pl.ds(pl.multiple_of(dst0 + hi, SEG_ALIGN), size)],
                                         sem.at[slot]))
            _segment_pieces(pc_s[idx], piece)

        @pl.when(b == 0)
        def _():
            for e in range(N_EXPERTS):
                dst0 = gs_s[e]

                def piece(hi, size, dst0=dst0):
                    go(pltpu.make_async_copy(zbuf.at[pl.ds(0, size)],
                                             xs_ref.at[pl.ds(pl.multiple_of(dst0 + hi, SEG_ALIGN), size)],
                                             sem.at[slot]))
                _segment_pieces(gl_s[e], piece)

            zrows = zbuf.shape[0]

            def tail_chunk(i, carry):
                go(pltpu.make_async_copy(zbuf, xs_ref.at[pl.ds(pl.multiple_of(tail_s[0] + i * zrows, zrows), zrows)],
                                         sem.at[slot]))
                return carry
            lax.fori_loop(0, tail_s[1], tail_chunk, 0)

    copies(blk, cur, lambda cp: cp.start())

    @pl.when(blk > 0)
    def _():
        copies(blk - 1, 1 - cur, lambda cp: cp.wait())

    @pl.when(blk == pl.num_programs(0) - 1)
    def _():
        copies(blk, cur, lambda cp: cp.wait())


def _dispatch_call(meta, u2, rt, n_rows):
    n, d = u2.shape
    tb = DISPATCH_BLOCK
    per_tile = rt.shape[2] // tb
    grid_spec = pltpu.PrefetchScalarGridSpec(
        num_scalar_prefetch=6,
        grid=(n // tb,),
        in_specs=[pl.BlockSpec((tb, d), lambda i, *_: (i, 0)),
                  pl.BlockSpec((1, 8, tb), lambda i, *_: (i // per_tile, 0, i % per_tile))],
        out_specs=pl.BlockSpec(memory_space=pl.ANY),
        scratch_shapes=[pltpu.VMEM((2, DISPATCH_SLOTS, d), F32),
                        pltpu.VMEM((SEG_ALIGN << (SEG_PIECE_BITS - 1), d), F32),
                        pltpu.SemaphoreType.DMA((2,))],
    )
    return pl.pallas_call(
        _dispatch_kernel,
        grid_spec=grid_spec,
        out_shape=jax.ShapeDtypeStruct((n_rows, d), F32),
        compiler_params=_cparams(("arbitrary",)),
        name="moe_dispatch",
    )(meta["off"], meta["pc"], meta["gbase"], meta["gap_start"], meta["gap_len"], meta["tail"], u2, rt)


def _experts_kernel(te_s, na_s, x_ref, wg_ref, wu_ref, wd_ref, y_ref):
    active = pl.program_id(0) < na_s[0]

    @pl.when(active)
    def _():
        x = x_ref[...].astype(BF16)
        gate = jnp.dot(x, wg_ref[0], preferred_element_type=F32)
        up = jnp.dot(x, wu_ref[0], preferred_element_type=F32)
        hidden = (gate * _sigmoid(gate) * up).astype(BF16)
        y_ref[...] = jnp.dot(hidden, wd_ref[0], preferred_element_type=F32)

    @pl.when(jnp.logical_not(active))
    def _():
        y_ref[...] = jnp.zeros_like(y_ref)


def _experts_call(meta, xs, wg, wu, wd):
    n_rows, d = xs.shape
    tm = TM_MOE
    _, _, de = wg.shape
    tile = lambda i, te, na: jnp.minimum(i, na[0] - 1)
    grid_spec = pltpu.PrefetchScalarGridSpec(
        num_scalar_prefetch=2,
        grid=(n_rows // tm,),
        in_specs=[pl.BlockSpec((tm, d), lambda i, te, na: (tile(i, te, na), 0)),
                  pl.BlockSpec((1, d, de), lambda i, te, na: (te[tile(i, te, na)], 0, 0)),
                  pl.BlockSpec((1, d, de), lambda i, te, na: (te[tile(i, te, na)], 0, 0)),
                  pl.BlockSpec((1, de, d), lambda i, te, na: (te[tile(i, te, na)], 0, 0))],
        out_specs=pl.BlockSpec((tm, d), lambda i, te, na: (i, 0)),
    )
    return pl.pallas_call(
        _experts_kernel,
        grid_spec=grid_spec,
        out_shape=jax.ShapeDtypeStruct((n_rows, d), F32),
        compiler_params=_cparams(("arbitrary",)),
        name="experts",
    )(meta["tile_expert"], meta["n_active"], xs, wg, wu, wd)


def _combine_kernel(off_s, pc_s, gb_s, rtc_ref, x1_ref, mod_ref, lng_ref, lnb_ref, ys_ref, o_ref, slab, sem):
    blk = pl.program_id(0)
    tb = x1_ref.shape[0]
    n_slots = slab.shape[1]
    cur = blk % 2

    def copies(b, slot, go):
        for e in range(N_EXPERTS):
            idx = b * N_EXPERTS + e
            dst0, src0 = off_s[idx], gb_s[idx]

            def piece(hi, size, src0=src0, dst0=dst0):
                go(pltpu.make_async_copy(ys_ref.at[pl.ds(pl.multiple_of(src0 + hi, SEG_ALIGN), size)],
                                         slab.at[slot, pl.ds(pl.multiple_of(dst0 + hi, SEG_ALIGN), size)],
                                         sem.at[slot]))
            _segment_pieces(pc_s[idx], piece)

    @pl.when(blk == 0)
    def _():
        slab[...] = jnp.zeros_like(slab)
        copies(blk, cur, lambda cp: cp.start())

    @pl.when(blk + 1 < pl.num_programs(0))
    def _():
        copies(blk + 1, 1 - cur, lambda cp: cp.start())

    rtc = rtc_ref[...]
    e1, e2, g1, g2 = rtc[:, 0:1], rtc[:, 1:2], rtc[:, 2:3], rtc[:, 3:4]
    lane_e = lax.broadcasted_iota(jnp.int32, (1, LANES), 1).astype(F32)
    hit1 = e1 == lane_e
    hit2 = e2 == lane_e
    onehot = jnp.where(hit1 | hit2, 1.0, 0.0).astype(BF16)
    r = lax.broadcasted_iota(jnp.int32, (tb, tb), 0)
    c = lax.broadcasted_iota(jnp.int32, (tb, tb), 1)
    lower = jnp.where(c < r, 1.0, 0.0).astype(BF16)
    rank = _slot_positions(onehot, lower, transposed=True)
    lane1 = lax.broadcasted_iota(jnp.int32, (1, LANES), 1)
    offv = jnp.zeros((1, LANES), F32)
    for e in range(N_EXPERTS):
        offv = jnp.where(lane1 == e, off_s[blk * N_EXPERTS + e].astype(F32), offv)
    pos = offv + rank
    pos1 = jnp.sum(jnp.where(hit1, pos, 0.0), axis=-1, keepdims=True)
    pos2 = jnp.sum(jnp.where(hit2, pos, 0.0), axis=-1, keepdims=True)
    slot = lax.broadcasted_iota(jnp.int32, (1, n_slots), 1).astype(F32)
    sel1 = jnp.where(slot == pos1, 1.0, 0.0).astype(BF16)
    sel2 = jnp.where(slot == pos2, 1.0, 0.0).astype(BF16)

    copies(blk, cur, lambda cp: cp.wait())
    ys = slab[cur].astype(BF16)
    y = (g1 * jnp.dot(sel1, ys, preferred_element_type=F32) + g2 * jnp.dot(sel2, ys, preferred_element_type=F32))
    mod = mod_ref[0]
    o_ref[...] = _layer_norm(ALPHA * x1_ref[...] + (1.0 + mod[5:6, :]) * y, lng_ref[...], lnb_ref[...])


def _combine_call(meta, ys, rtc, x1, mod_l, lng, lnb, seq):
    n, d = x1.shape
    tb = DISPATCH_BLOCK
    bps = seq // tb
    grid_spec = pltpu.PrefetchScalarGridSpec(
        num_scalar_prefetch=3,
        grid=(n // tb,),
        in_specs=[pl.BlockSpec((tb, LANES), lambda i, *_: (i, 0)),
                  pl.BlockSpec((tb, d), lambda i, *_: (i, 0)),
                  pl.BlockSpec((1, 6, d), lambda i, *_: (i // bps, 0, 0)),
                  pl.BlockSpec((1, d), lambda i, *_: (0, 0)),
                  pl.BlockSpec((1, d), lambda i, *_: (0, 0)),
                  pl.BlockSpec(memory_space=pl.ANY)],
        out_specs=pl.BlockSpec((tb, d), lambda i, *_: (i, 0)),
        scratch_shapes=[pltpu.VMEM((2, DISPATCH_SLOTS, d), F32), pltpu.SemaphoreType.DMA((2,))],
    )
    return pl.pallas_call(
        _combine_kernel,
        grid_spec=grid_spec,
        out_shape=jax.ShapeDtypeStruct((n, d), F32),
        compiler_params=_cparams(("arbitrary",)),
        name="moe_combine_ln2",
    )(meta["off"], meta["pc"], meta["gbase"], rtc, x1, mod_l, lng, lnb, ys)


def _sorted_rows_bound(n_tokens):
    n_blocks = n_tokens // DISPATCH_BLOCK
    worst = 2 * n_tokens + n_blocks * N_EXPERTS * (SEG_ALIGN - 1) + N_EXPERTS * (TM_MOE - 1)
    return -(-worst // TM_MOE) * TM_MOE


def _route_meta(cnt, n_rows):
    pc = (cnt + SEG_ALIGN - 1) // SEG_ALIGN * SEG_ALIGN
    off = jnp.cumsum(pc, axis=1) - pc
    tot = pc.sum(axis=0)
    reg = (tot + TM_MOE - 1) // TM_MOE * TM_MOE
    reg_end = jnp.cumsum(reg)
    ebase = reg_end - reg
    gbase = ebase[None, :] + jnp.cumsum(pc, axis=0) - pc
    tiles = jnp.arange(n_rows // TM_MOE, dtype=jnp.int32)
    tile_expert = jnp.minimum(jnp.sum(tiles[:, None] >= (reg_end // TM_MOE)[None, :], axis=1), N_EXPERTS - 1)
    i32 = lambda a: a.astype(jnp.int32).reshape(-1)
    used = reg_end[-1:]
    tail_chunk = SEG_ALIGN << (SEG_PIECE_BITS - 1)
    return {"off": i32(off), "pc": i32(pc), "gbase": i32(gbase), "gap_start": i32(ebase + tot),
            "gap_len": i32(reg - tot), "tail": i32(jnp.concatenate([used, (n_rows - used) // tail_chunk])),
            "tile_expert": i32(tile_expert), "n_active": i32(used // TM_MOE)}


def _rearranged_w_in(w_in_l):
    d = w_in_l.shape[0]
    zeros = lambda w: jnp.zeros((d, w), w_in_l.dtype)
    o_gate, o_cb, o_dq, o_dk, o_dv, o_g, o_ga = 1024, 1032, 1800, 2056, 2312, 2568, 3592
    v_cols = []
    for h in range(HEADS):
        v_cols += [w_in_l[:, o_dv + h * HEAD_DIM:o_dv + (h + 1) * HEAD_DIM], zeros(LANES - HEAD_DIM)]
    small = [w_in_l[:, o_gate:o_gate + 2 * HEADS], w_in_l[:, o_ga:o_ga + GLA_RANK],
             zeros(ZS - 2 * HEADS - GLA_RANK)]
    cols = [w_in_l[:, 0:ZM], w_in_l[:, o_cb:o_cb + ZC], w_in_l[:, o_dq:o_dq + ZQ], w_in_l[:, o_dk:o_dk + ZK]]
    cols += v_cols + [w_in_l[:, o_g:o_g + ZG]] + small
    return jnp.concatenate(cols, axis=1).astype(BF16)


def _rope_tables(positions):
    inv_freq = ROPE_THETA ** (-jnp.arange(0, ROPE_DIM, 2, dtype=F32) / ROPE_DIM)
    ang = positions.astype(F32).reshape(-1, 1) * inv_freq
    cos, sin = jnp.cos(ang), jnp.sin(ang)
    d = np.arange(ZQ) % DIFF_DIM
    half = ROPE_DIM // 2
    idx = d % half
    cosf = jnp.where(d < ROPE_DIM, cos[:, idx], 1.0)
    sinf = jnp.where(d < half, -sin[:, idx], jnp.where(d < ROPE_DIM, sin[:, idx], 0.0))
    return cosf, sinf


def _pad_lanes(v, width=LANES):
    v = v.reshape(1, -1)
    return jnp.pad(v, ((0, 0), (0, width - v.shape[1])))


def kernel(x, c, positions, w_ada, b_ada, w_in, w_out, mlstm_b_i, mlstm_b_f, mlstm_norm_g, conv_w, conv_b, diff_lambda_q1, diff_lambda_k1, diff_lambda_q2, diff_lambda_k2, diff_norm_g, gla_w_a2, gla_b_a, gla_norm_g, ln1_g, ln1_b, ln2_g, ln2_b, router_w, router_bias, exp_w_gate, exp_w_up, exp_w_down):
    bsz, seq, d = x.shape
    depth = w_in.shape[0]
    n = bsz * seq
    x2 = x.reshape(n, d)
    mod = _mod_call(c, w_ada, b_ada)
    cosf, sinf = _rope_tables(positions)
    rw_t = router_w.T.astype(BF16)
    rb = router_bias.reshape(N_EXPERTS, 1).astype(F32)
    n_sorted = _sorted_rows_bound(n)
    for l in range(depth):
        lam_init = 0.8 - 0.6 * math.exp(-0.3 * l)
        w_l = _rearranged_w_in(w_in[l])
        zm, zc, zq, zk, zv, zg, zs = _inproj_call(x2, mod[l], w_l, cosf, sinf, seq)
        bif = _pad_lanes(jnp.concatenate([mlstm_b_i[l], mlstm_b_f[l]]))
        ya = _mlstm_call(zm, zs, bif, mlstm_norm_g[l].reshape(1, -1), bsz, seq)
        lamp = jnp.concatenate([_pad_lanes(diff_lambda_q1[l]), _pad_lanes(diff_lambda_k1[l]),
                                _pad_lanes(diff_lambda_q2[l]), _pad_lanes(diff_lambda_k2[l])], axis=0)
        yc = _attn_call(zq, zk, zv, lamp, diff_norm_g[l].reshape(1, -1), lam_init, bsz, seq)
        wa = jnp.pad(gla_w_a2[l], ((2 * HEADS, ZS - 2 * HEADS - GLA_RANK), (0, 0))).astype(BF16)
        yd = _gla_call(zg, zs, wa, gla_b_a[l].reshape(1, -1), gla_norm_g[l].reshape(1, -1), bsz, seq)
        x1, u2, rt, rtc, cnt = _outproj_call(ya, zc, yc, yd, x2, mod[l], w_out[l].astype(BF16), conv_w[l],
                                             conv_b[l].reshape(1, -1), ln1_g[l].reshape(1, -1),
                                             ln1_b[l].reshape(1, -1), rw_t, rb, seq)
        per_tile = TM_PROJ // DISPATCH_BLOCK
        cnt_blk = cnt[:, :, :per_tile].transpose(0, 2, 1).reshape(-1, N_EXPERTS).astype(jnp.int32)
        meta = _route_meta(cnt_blk, n_sorted)
        xs = _dispatch_call(meta, u2, rt, n_sorted)
        ys = _experts_call(meta, xs, exp_w_gate[l].astype(BF16), exp_w_up[l].astype(BF16),
                           exp_w_down[l].astype(BF16))
        x2 = _combine_call(meta, ys, rtc, x1, mod[l], ln2_g[l].reshape(1, -1), ln2_b[l].reshape(1, -1), seq)
    return x2.reshape(bsz, seq, d)
```

```python
import functools
import math

import numpy as np
import jax
import jax.numpy as jnp
from jax import lax
from jax.experimental import pallas as pl
from jax.experimental.pallas import tpu as pltpu

F32 = jnp.float32
BF16 = jnp.bfloat16

D_MODEL = 1024
DEPTH = 4
HEADS = 4
GW = 256
HEAD_DIM = 64
DIFF_DIM = 32
ROPE_DIM = 8
ROPE_THETA = 500000.0
CONV_WIDTH = 3
GLA_RANK = 16
GLA_TAU = 16.0
CHUNK = 64
GLA_CHUNK = 64
N_EXPERTS = 16
N_GROUPS = 4
GROUP_SIZE = 4
D_EXPERT = 512
ALPHA = (2 * DEPTH) ** 0.25
EPS = 1e-6
LANES = 128
NEG_INF = float("-inf")
LOG2_E = math.log2(math.e)

ZM, ZC, ZQ, ZK, ZV, ZG, ZS = 1024, 768, 256, 256, 512, 1024, 128
Z_OFFS = np.cumsum([0, ZM, ZC, ZQ, ZK, ZV, ZG, ZS])
Z_TOTAL = int(Z_OFFS[-1])

TM_PROJ = 512
TB_MLSTM = 256
TB_GLA = 512
CUM_SPAN = 128
TQ_ATTN = 512
TM_MOE = 512
DISPATCH_BLOCK = 256
SEG_ALIGN = 8
SEG_PIECE_BITS = 6
DISPATCH_SLOTS = 2 * DISPATCH_BLOCK + N_EXPERTS * SEG_ALIGN
DISPATCH_GROUPS = DISPATCH_SLOTS // SEG_ALIGN
assert SEG_ALIGN << (SEG_PIECE_BITS - 1) >= DISPATCH_BLOCK and SEG_ALIGN << SEG_PIECE_BITS >= TM_MOE
VMEM_LIMIT = 56 * 1024 * 1024


def _cparams(sem):
    return pltpu.CompilerParams(dimension_semantics=sem, vmem_limit_bytes=VMEM_LIMIT)


def _log_sigmoid(x):
    return jnp.minimum(x, 0.0) - jnp.log(1.0 + jnp.exp(-jnp.abs(x)))


def _sigmoid(x):
    return 1.0 / (1.0 + jnp.exp(-x))


def _layer_norm(v, g, b):
    mu = jnp.mean(v, axis=-1, keepdims=True)
    d = v - mu
    var = jnp.mean(d * d, axis=-1, keepdims=True)
    return d * lax.rsqrt(var + EPS) * g + b


def _tril(n):
    r = lax.broadcasted_iota(jnp.int32, (n, n), 0)
    c = lax.broadcasted_iota(jnp.int32, (n, n), 1)
    return c <= r


def _mod_kernel(c_ref, w_ref, b_ref, o_ref):
    c = c_ref[...]
    c_act = (c * _sigmoid(c)).astype(BF16)
    o_ref[0, 0] = jnp.dot(c_act, w_ref[0].astype(BF16), preferred_element_type=F32) + b_ref[0, 0]


def _mod_call(c, w_ada, b_ada):
    depth, d, _ = w_ada.shape
    bsz = c.shape[0]
    b3 = b_ada.reshape(depth, 6, 1, d)
    out = pl.pallas_call(
        _mod_kernel,
        grid=(depth, 6),
        in_specs=[pl.BlockSpec((bsz, d), lambda l, j: (0, 0)),
                  pl.BlockSpec((1, d, d), lambda l, j: (l, 0, j)),
                  pl.BlockSpec((1, 1, 1, d), lambda l, j: (l, j, 0, 0))],
        out_specs=pl.BlockSpec((1, 1, bsz, d), lambda l, j: (l, j, 0, 0)),
        out_shape=jax.ShapeDtypeStruct((depth, 6, bsz, d), F32),
        compiler_params=_cparams(("arbitrary", "arbitrary")),
        name="adaln_mod",
    )(c, w_ada, b3)
    return out.transpose(0, 2, 1, 3)


def _inproj_kernel(x_ref, mod_ref, w_ref, cos_ref, sin_ref, zm_ref, zc_ref, zq_ref, zk_ref, zv_ref, zg_ref,
                   zs_ref):
    mod = mod_ref[0]
    u = (x_ref[...] * (1.0 + mod[1:2, :]) + mod[0:1, :]).astype(BF16)

    def proj(idx):
        return jnp.dot(u, w_ref[:, int(Z_OFFS[idx]):int(Z_OFFS[idx + 1])], preferred_element_type=F32)

    zm_ref[...] = proj(0).astype(BF16)
    zc_ref[...] = proj(1).astype(BF16)

    lane = lax.broadcasted_iota(jnp.int32, (1, ZQ), 1)
    first_half = (lane % ROPE_DIM) < (ROPE_DIM // 2)
    cosf = cos_ref[...]
    sinf = sin_ref[...]

    def rope(r):
        swapped = jnp.where(first_half, pltpu.roll(r, ZQ - ROPE_DIM // 2, 1), pltpu.roll(r, ROPE_DIM // 2, 1))
        return r * cosf + swapped * sinf

    zq_ref[...] = (rope(proj(2)) * (LOG2_E * DIFF_DIM ** -0.5)).astype(BF16)
    zk_ref[...] = rope(proj(3)).astype(BF16)
    lane_v = lax.broadcasted_iota(jnp.int32, (1, ZV), 1)
    ones_col = jnp.where((lane_v % LANES) == HEAD_DIM, 1.0, 0.0)
    zv_ref[...] = (proj(4) + ones_col).astype(BF16)
    zg_ref[...] = proj(5).astype(BF16)
    zs_ref[...] = proj(6)


def _inproj_call(x2, mod_l, w_l, cosf, sinf, seq):
    n, d = x2.shape
    tm = TM_PROJ
    tps = seq // tm
    row = lambda w: pl.BlockSpec((tm, w), lambda i: (i, 0))
    outs = [(ZM, BF16), (ZC, BF16), (ZQ, BF16), (ZK, BF16), (ZV, BF16), (ZG, BF16), (ZS, F32)]
    return pl.pallas_call(
        _inproj_kernel,
        grid=(n // tm,),
        in_specs=[row(d),
                  pl.BlockSpec((1, 6, d), lambda i: (i // tps, 0, 0)),
                  pl.BlockSpec((d, Z_TOTAL), lambda i: (0, 0)),
                  row(ZQ), row(ZQ)],
        out_specs=[row(w) for w, _ in outs],
        out_shape=[jax.ShapeDtypeStruct((n, w), dt) for w, dt in outs],
        compiler_params=_cparams(("parallel",)),
        name="in_proj",
    )(x2, mod_l, w_l, cosf, sinf)


NT_DIMS = (((1,), (1,)), ((), ()))
TN_DIMS = (((0,), (0,)), ((), ()))


def _head_of(idx):
    return idx // HEAD_DIM


def _cumsum_matrix(span, chunk):
    r = lax.broadcasted_iota(jnp.int32, (span, span), 0)
    c = lax.broadcasted_iota(jnp.int32, (span, span), 1)
    return jnp.where(((r // chunk) == (c // chunk)) & (c <= r), 1.0, 0.0).astype(BF16)


def _bf16_terms(x, n_terms):
    terms, rest = [], x
    for _ in range(n_terms):
        t = rest.astype(BF16)
        terms.append(t)
        rest = rest - t.astype(F32)
    return terms


def _exact_matmul(lhs_f32, rhs01, n_terms=3):
    return functools.reduce(lambda a, b: a + b, [jnp.dot(t, rhs01, preferred_element_type=F32)
                                                 for t in reversed(_bf16_terms(lhs_f32, n_terms))])


def _chunk_cumsum(x, chunk):
    cm = _cumsum_matrix(CUM_SPAN, chunk)
    outs = []
    for i in range(x.shape[0] // CUM_SPAN):
        terms = _bf16_terms(x[i * CUM_SPAN:(i + 1) * CUM_SPAN], 3)
        outs.append(functools.reduce(lambda a, b: a + b,
                                     [jnp.dot(cm, t, preferred_element_type=F32) for t in reversed(terms)]))
    return jnp.concatenate(outs, axis=0)


def _rows_broadcast(x, chunk, row):
    n = x.shape[0] // chunk
    return jnp.concatenate([jnp.broadcast_to(x[c * chunk + row:c * chunk + row + 1, :], (chunk, x.shape[1]))
                            for c in range(n)], axis=0)


def _head_masks(chunk):
    lane_head = _head_of(lax.broadcasted_iota(jnp.int32, (1, GW), 1))
    row_head = lax.broadcasted_iota(jnp.int32, (HEADS * chunk, 1), 0) // chunk
    stack_keep = row_head == lane_head
    rr = lax.broadcasted_iota(jnp.int32, (HEADS * chunk, chunk), 0) % chunk
    cc = lax.broadcasted_iota(jnp.int32, (HEADS * chunk, chunk), 1)
    block_diag = (_head_of(lax.broadcasted_iota(jnp.int32, (GW, GW), 0))
                  == _head_of(lax.broadcasted_iota(jnp.int32, (GW, GW), 1)))
    return stack_keep, cc <= rr, block_diag


def _unstack_heads(x4, keep, chunk):
    x4 = jnp.where(keep, x4, 0.0)
    return functools.reduce(lambda a, b: a + b, [x4[h * chunk:(h + 1) * chunk] for h in range(HEADS)])


def _head_rms_norm(x, block_diag):
    seg = jnp.where(block_diag, 1.0, 0.0).astype(BF16)
    return x * lax.rsqrt(_exact_matmul(x * x, seg, 2) * (1.0 / HEAD_DIM) + EPS)


def _mlstm_kernel(zm_ref, zs_ref, bif_ref, g_ref, o_ref, ct_scr, nm_scr, ni_scr, up_scr):
    @pl.when(pl.program_id(1) == 0)
    def _():
        ct_scr[...] = jnp.zeros_like(ct_scr)
        nm_scr[...] = jnp.zeros_like(nm_scr)

    tb = zm_ref.shape[0]
    L = CHUNK
    nc = tb // L
    lane = lax.broadcasted_iota(jnp.int32, (1, LANES), 1)
    graw = zs_ref[...] + bif_ref[...]
    gp = jnp.where(lane < HEADS, graw, jnp.where(lane < 2 * HEADS, _log_sigmoid(graw), 0.0))
    cum = _chunk_cumsum(gp, L)

    def expand(x, first_lane):
        sel = jnp.where((lax.broadcasted_iota(jnp.int32, (LANES, GW), 0) - first_lane)
                        == _head_of(lax.broadcasted_iota(jnp.int32, (LANES, GW), 1)), 1.0, 0.0).astype(BF16)
        return _exact_matmul(x, sel)

    bx = expand(cum, HEADS)
    lix = expand(gp, 0)
    bex = _rows_broadcast(bx, L, L - 1)
    dkx = bex - bx + lix
    dkmax = [jnp.max(dkx[c * L:(c + 1) * L], axis=0, keepdims=True) for c in range(nc)]
    kw = (zm_ref[:, GW:2 * GW].astype(F32)
          * jnp.exp(dkx - jnp.concatenate([jnp.broadcast_to(d, (L, GW)) for d in dkmax], axis=0))).astype(BF16)
    rowv_t = (cum - pltpu.roll(gp, HEADS, 1)).T
    stack_keep, causal4, block_diag = _head_masks(L)
    stack_keep2 = jnp.concatenate([stack_keep, stack_keep], axis=1)
    ones_blk = jnp.ones((L, GW), BF16)

    dmaxx = []
    for c in range(nc):
        rs = slice(c * L, (c + 1) * L)
        qc = zm_ref[rs, 0:GW] * (HEAD_DIM ** -0.5)
        kc = zm_ref[rs, GW:2 * GW]
        vc = zm_ref[rs, 2 * GW:3 * GW]
        q4 = jnp.where(stack_keep, jnp.concatenate([qc] * HEADS, axis=0), jnp.zeros((), BF16))
        s4 = lax.dot_general(q4, kc, NT_DIMS, preferred_element_type=F32)
        bcol4 = jnp.concatenate([cum[rs, HEADS + h:HEADS + h + 1] for h in range(HEADS)], axis=0)
        brow4 = jnp.concatenate([jnp.broadcast_to(rowv_t[HEADS + h:HEADS + h + 1, rs], (L, L))
                                 for h in range(HEADS)], axis=0)
        d4 = jnp.where(causal4, bcol4 - brow4, NEG_INF)
        dmax4 = jnp.max(d4, axis=-1, keepdims=True)
        p4 = (s4 * jnp.exp(d4 - dmax4)).astype(BF16)
        i4 = jnp.dot(p4, jnp.concatenate([vc, ones_blk], axis=1), preferred_element_type=F32)
        ni_scr[rs, :] = _unstack_heads(i4, stack_keep2, L)
        dmaxx.append(_unstack_heads(jnp.broadcast_to(dmax4, (HEADS * L, GW)), stack_keep, L))
        up = lax.dot_general(vc, kw[rs], TN_DIMS, preferred_element_type=F32)
        up_scr[c] = jnp.where(block_diag, up, 0.0)

    ct = ct_scr[...]
    n_row = nm_scr[0:1, :]
    m_prev = nm_scr[1:2, :]
    hv_rows = []
    for c in range(nc):
        rs = slice(c * L, (c + 1) * L)
        qc = zm_ref[rs, 0:GW] * (HEAD_DIM ** -0.5)
        nmat = jnp.where(block_diag, jnp.broadcast_to(n_row, (GW, GW)), 0.0).astype(BF16)
        num_inter = lax.dot_general(qc, ct.astype(BF16), NT_DIMS, preferred_element_type=F32)
        den_inter = lax.dot_general(qc, nmat, NT_DIMS, preferred_element_type=F32)
        inter = bx[rs] + m_prev
        m_row = jnp.maximum(inter, dmaxx[c])
        e1 = jnp.exp(dmaxx[c] - m_row)
        e2 = jnp.exp(inter - m_row)
        num = e1 * ni_scr[rs, 0:GW] + e2 * num_inter
        den = e1 * ni_scr[rs, GW:2 * GW] + e2 * den_inter
        hv_rows.append(num / jnp.maximum(jnp.abs(den), jnp.exp(-m_row)))
        b_end = bex[c * L:c * L + 1, :]
        m_new = jnp.maximum(b_end + m_prev, dkmax[c])
        decay = jnp.exp(b_end + m_prev - m_new)
        f2 = jnp.exp(dkmax[c] - m_new)
        ct = ct * decay + up_scr[c] * f2
        n_row = n_row * decay + f2 * jnp.sum(kw[rs].astype(F32), axis=0, keepdims=True)
        m_prev = m_new
    ct_scr[...] = ct
    nm_scr[0:1, :] = n_row
    nm_scr[1:2, :] = m_prev

    hn = _head_rms_norm(jnp.concatenate(hv_rows, axis=0), block_diag)
    o_ref[...] = (hn * g_ref[...] * _sigmoid(zm_ref[:, 3 * GW:4 * GW].astype(F32))).astype(BF16)


def _mlstm_call(zm, zs, bif, gain, bsz, seq):
    tb = TB_MLSTM
    tps = seq // tb
    return pl.pallas_call(
        _mlstm_kernel,
        grid=(bsz, tps),
        in_specs=[pl.BlockSpec((tb, ZM), lambda b, t: (b * tps + t, 0)),
                  pl.BlockSpec((tb, ZS), lambda b, t: (b * tps + t, 0)),
                  pl.BlockSpec((1, LANES), lambda b, t: (0, 0)),
                  pl.BlockSpec((1, GW), lambda b, t: (0, 0))],
        out_specs=pl.BlockSpec((tb, GW), lambda b, t: (b * tps + t, 0)),
        out_shape=jax.ShapeDtypeStruct((bsz * seq, GW), BF16),
        scratch_shapes=[pltpu.VMEM((GW, GW), F32),
                        pltpu.VMEM((8, GW), F32),
                        pltpu.VMEM((tb, 2 * GW), F32),
                        pltpu.VMEM((tb // CHUNK, GW, GW), F32)],
        compiler_params=_cparams(("parallel", "arbitrary")),
        name="mlstm",
    )(zm, zs, bif, jnp.tile(gain, (1, HEADS)))


def _gla_kernel(zg_ref, zs_ref, wa_ref, ba_ref, g_ref, o_ref, st_scr, oi_scr, up_scr):
    @pl.when(pl.program_id(1) == 0)
    def _():
        st_scr[...] = jnp.zeros_like(st_scr)

    tb = zg_ref.shape[0]
    lg = GLA_CHUNK
    nc = tb // lg
    la = _log_sigmoid(jnp.dot(zs_ref[...].astype(BF16), wa_ref[...], preferred_element_type=F32)
                      + ba_ref[...]) * (1.0 / GLA_TAU)
    bc = _chunk_cumsum(la, lg)
    be = _rows_broadcast(bc, lg, lg - 1)
    bm = _rows_broadcast(bc, lg, lg // 2)
    q = zg_ref[:, 0:GW].astype(F32) * (HEAD_DIM ** -0.5)
    k = zg_ref[:, GW:2 * GW].astype(F32)
    qm = (q * jnp.exp(bc - bm)).astype(BF16)
    km = (k * jnp.exp(bm - bc)).astype(BF16)
    qt = (q * jnp.exp(bc)).astype(BF16)
    ks = (k * jnp.exp(be - bc)).astype(BF16)
    dec = jnp.exp(be)
    stack_keep, causal4, block_diag = _head_masks(lg)

    for c in range(nc):
        rs = slice(c * lg, (c + 1) * lg)
        vc = zg_ref[rs, 2 * GW:3 * GW]
        q4 = jnp.where(stack_keep, jnp.concatenate([qm[rs]] * HEADS, axis=0), jnp.zeros((), BF16))
        att4 = lax.dot_general(q4, km[rs], NT_DIMS, preferred_element_type=F32)
        att4 = jnp.where(causal4, att4, 0.0).astype(BF16)
        oi_scr[rs, :] = _unstack_heads(jnp.dot(att4, vc, preferred_element_type=F32), stack_keep, lg)
        up = lax.dot_general(vc, ks[rs], TN_DIMS, preferred_element_type=F32)
        up_scr[c] = jnp.where(block_diag, up, 0.0)

    st = st_scr[...]
    inter = []
    for c in range(nc):
        rs = slice(c * lg, (c + 1) * lg)
        inter.append(lax.dot_general(qt[rs], st.astype(BF16), NT_DIMS, preferred_element_type=F32))
        st = st * dec[c * lg:c * lg + 1, :] + up_scr[c]
    st_scr[...] = st

    on = _head_rms_norm(oi_scr[...] + jnp.concatenate(inter, axis=0), block_diag)
    r = zg_ref[:, 3 * GW:4 * GW].astype(F32)
    o_ref[...] = (on * g_ref[...] * (r * _sigmoid(r))).astype(BF16)


def _gla_call(zg, zs, wa, ba, gain, bsz, seq):
    tb = TB_GLA
    tps = seq // tb
    return pl.pallas_call(
        _gla_kernel,
        grid=(bsz, tps),
        in_specs=[pl.BlockSpec((tb, ZG), lambda b, t: (b * tps + t, 0)),
                  pl.BlockSpec((tb, ZS), lambda b, t: (b * tps + t, 0)),
                  pl.BlockSpec((ZS, GW), lambda b, t: (0, 0)),
                  pl.BlockSpec((1, GW), lambda b, t: (0, 0)),
                  pl.BlockSpec((1, GW), lambda b, t: (0, 0))],
        out_specs=pl.BlockSpec((tb, GW), lambda b, t: (b * tps + t, 0)),
        out_shape=jax.ShapeDtypeStruct((bsz * seq, GW), BF16),
        scratch_shapes=[pltpu.VMEM((GW, GW), F32),
                        pltpu.VMEM((tb, GW), F32),
                        pltpu.VMEM((tb // GLA_CHUNK, GW, GW), F32)],
        compiler_params=_cparams(("parallel", "arbitrary")),
        name="gla",
    )(zg, zs, wa, ba, jnp.tile(gain, (1, HEADS)))


def _attn_kernel(lam_init, q_ref, k_ref, v_ref, lamp_ref, g_ref, o_ref, qm_scr, m_scr, acc_scr):
    tq = q_ref.shape[0]
    qi = pl.program_id(1)
    n_hc = 2 * HEADS
    lane = lax.broadcasted_iota(jnp.int32, (1, LANES), 1)
    for hc in range(n_hc):
        h, c = divmod(hc, 2)
        pair = h // 2
        lo = (h % 2) * HEAD_DIM + c * DIFF_DIM
        qp = q_ref[:, pair * LANES:(pair + 1) * LANES]
        qm_scr[hc] = jnp.where((lane >= lo) & (lane < lo + DIFF_DIM), qp, jnp.zeros_like(qp))
    m_scr[...] = jnp.full(m_scr.shape, NEG_INF, F32)
    acc_scr[...] = jnp.zeros_like(acc_scr)

    def kv_block(j, keep):
        key0 = pl.multiple_of(j * tq, tq)
        for hc in range(n_hc):
            h = hc // 2
            pair = h // 2
            kb = k_ref[pl.ds(key0, tq), pair * LANES:(pair + 1) * LANES]
            vb = v_ref[pl.ds(key0, tq), h * LANES:(h + 1) * LANES]
            s = lax.dot_general(qm_scr[hc], kb, NT_DIMS, preferred_element_type=F32)
            if keep is not None:
                s = jnp.where(keep, s, NEG_INF)
            slabs = [s[:, i * LANES:(i + 1) * LANES] for i in range(tq // LANES)]
            mx = functools.reduce(jnp.maximum, slabs)
            m_old = m_scr[hc]
            m_new = jnp.maximum(m_old, jnp.broadcast_to(jnp.max(mx, axis=-1, keepdims=True), (tq, LANES)))
            alpha = jnp.exp2(m_old - m_new)
            p = jnp.concatenate([jnp.exp2(sl - m_new) for sl in slabs], axis=1).astype(BF16)
            acc_scr[hc] = alpha * acc_scr[hc] + jnp.dot(p, vb, preferred_element_type=F32)
            m_scr[hc] = m_new

    def full_block(j, carry):
        kv_block(j, None)
        return carry

    lax.fori_loop(0, qi, full_block, 0)
    kv_block(qi, _tril(tq))

    lamp = lamp_ref[...]
    lam = (jnp.exp(jnp.sum(lamp[0:1, :] * lamp[1:2, :], axis=-1, keepdims=True))
           - jnp.exp(jnp.sum(lamp[2:3, :] * lamp[3:4, :], axis=-1, keepdims=True)) + lam_init)
    gain = g_ref[...] * (1.0 - lam_init)
    outs = []
    for h in range(HEADS):
        a1 = acc_scr[2 * h]
        a2 = acc_scr[2 * h + 1]
        o = (a1[:, :HEAD_DIM] / a1[:, HEAD_DIM:HEAD_DIM + 1]
             - lam * (a2[:, :HEAD_DIM] / a2[:, HEAD_DIM:HEAD_DIM + 1]))
        outs.append(o * lax.rsqrt(jnp.mean(o * o, axis=-1, keepdims=True) + EPS) * gain)
    o_ref[...] = jnp.concatenate(outs, axis=1).astype(BF16)


def _attn_call(zq, zk, zv, lamp, gain, lam_init, bsz, seq):
    tq = TQ_ATTN
    nq = seq // tq
    return pl.pallas_call(
        functools.partial(_attn_kernel, lam_init),
        grid=(bsz, nq),
        in_specs=[pl.BlockSpec((tq, ZQ), lambda b, i: (b * nq + i, 0)),
                  pl.BlockSpec((seq, ZK), lambda b, i: (b, 0)),
                  pl.BlockSpec((seq, ZV), lambda b, i: (b, 0)),
                  pl.BlockSpec((4, LANES), lambda b, i: (0, 0)),
                  pl.BlockSpec((1, HEAD_DIM), lambda b, i: (0, 0))],
        out_specs=pl.BlockSpec((tq, GW), lambda b, i: (b * nq + i, 0)),
        out_shape=jax.ShapeDtypeStruct((bsz * seq, GW), BF16),
        scratch_shapes=[pltpu.VMEM((2 * HEADS, tq, LANES), BF16),
                        pltpu.VMEM((2 * HEADS, tq, LANES), F32),
                        pltpu.VMEM((2 * HEADS, tq, LANES), F32)],
        compiler_params=_cparams(("parallel", "arbitrary")),
        name="diff_attn",
    )(zq, zk, zv, lamp, gain)


def _outproj_kernel(tiles_per_seq, ya_ref, zc_ref, halo_ref, yc_ref, yd_ref, x_ref, mod_ref, wo_ref, cw_ref,
                    cb_ref, lng_ref, lnb_ref, rw_ref, rb_ref, x1_ref, u2_ref, rt_ref, rtc_ref, cnt_ref):
    tm = x_ref.shape[0]
    mod = mod_ref[0]
    zc = zc_ref[...].astype(F32)
    halo = halo_ref[...].astype(F32)
    first = (pl.program_id(0) % tiles_per_seq) == 0
    s_cur = zc[:, GW:2 * GW] * zc[:, 2 * GW:3 * GW]
    s_halo = jnp.where(first, 0.0, halo[:, GW:2 * GW] * halo[:, 2 * GW:3 * GW])
    s_ext = jnp.concatenate([s_halo, s_cur], axis=0)
    hw = halo.shape[0]
    cw = cw_ref[...]
    conv = (cw[0:1, :] * s_ext[hw - 2:hw - 2 + tm, :] + cw[1:2, :] * s_ext[hw - 1:hw - 1 + tm, :]
            + cw[2:3, :] * s_cur + cb_ref[...])
    yb = (zc[:, 0:GW] * conv).astype(BF16)

    y = (jnp.dot(ya_ref[...], wo_ref[0:GW, :], preferred_element_type=F32)
         + jnp.dot(yb, wo_ref[GW:2 * GW, :], preferred_element_type=F32)
         + jnp.dot(yc_ref[...], wo_ref[2 * GW:3 * GW, :], preferred_element_type=F32)
         + jnp.dot(yd_ref[...], wo_ref[3 * GW:4 * GW, :], preferred_element_type=F32))
    x1 = _layer_norm(ALPHA * x_ref[...] + (1.0 + mod[2:3, :]) * y, lng_ref[...], lnb_ref[...])
    x1_ref[...] = x1
    u2 = (x1 * (1.0 + mod[4:5, :]) + mod[3:4, :]).astype(BF16)
    u2_ref[...] = u2

    logits = lax.dot_general(rw_ref[...], u2, (((1,), (1,)), ((), ())), preferred_element_type=F32)
    scores = _sigmoid(logits)
    biased = scores + rb_ref[...]
    rows = [biased[e:e + 1, :] for e in range(N_EXPERTS)]
    srow = [scores[e:e + 1, :] for e in range(N_EXPERTS)]
    gscore = []
    for g in range(N_GROUPS):
        a = rows[g * GROUP_SIZE:(g + 1) * GROUP_SIZE]
        best = None
        for i in range(GROUP_SIZE):
            for j in range(i + 1, GROUP_SIZE):
                pair_sum = a[i] + a[j]
                best = pair_sum if best is None else jnp.maximum(best, pair_sum)
        gscore.append(best)
    gmax = functools.reduce(jnp.maximum, gscore)
    gsel = jnp.full(gmax.shape, N_GROUPS, jnp.int32)
    for g in reversed(range(N_GROUPS)):
        gsel = jnp.where(gscore[g] == gmax, g, gsel)
    masked = [jnp.where(gsel == (e // GROUP_SIZE), rows[e], NEG_INF) for e in range(N_EXPERTS)]
    m1 = functools.reduce(jnp.maximum, masked)
    i1 = jnp.full(m1.shape, N_EXPERTS, jnp.int32)
    for e in reversed(range(N_EXPERTS)):
        i1 = jnp.where(masked[e] == m1, e, i1)
    masked2 = [jnp.where(i1 == e, NEG_INF, masked[e]) for e in range(N_EXPERTS)]
    m2 = functools.reduce(jnp.maximum, masked2)
    i2 = jnp.full(m2.shape, N_EXPERTS, jnp.int32)
    for e in reversed(range(N_EXPERTS)):
        i2 = jnp.where((masked2[e] == m2) & (i1 != e), e, i2)
    sel1 = functools.reduce(lambda acc, e: acc + jnp.where(i1 == e, srow[e], 0.0), range(N_EXPERTS), 0.0)
    sel2 = functools.reduce(lambda acc, e: acc + jnp.where(i2 == e, srow[e], 0.0), range(N_EXPERTS), 0.0)
    tot = sel1 + sel2
    g1 = sel1 / tot
    g2 = sel2 / tot
    route = jnp.concatenate([i1.astype(F32), i2.astype(F32), g1, g2, jnp.zeros((4, tm), F32)], axis=0)
    rt_ref[0] = route
    rtc_ref[...] = jnp.concatenate([route, jnp.zeros((LANES - 8, tm), F32)], axis=0).T
    sub = lax.broadcasted_iota(jnp.int32, (N_EXPERTS, tm), 0)
    onehot = jnp.where((i1 == sub) | (i2 == sub), 1.0, 0.0)
    lane = lax.broadcasted_iota(jnp.int32, (1, LANES), 1)
    cnt = jnp.zeros((N_EXPERTS, LANES), F32)
    for j in range(tm // DISPATCH_BLOCK):
        cj = jnp.sum(onehot[:, j * DISPATCH_BLOCK:(j + 1) * DISPATCH_BLOCK], axis=-1, keepdims=True)
        cnt = jnp.where(lane == j, cj, cnt)
    cnt_ref[0] = cnt


def _outproj_call(ya, zc, yc, yd, x2, mod_l, wo, cw, cb, lng, lnb, rw, rb, seq):
    n, d = x2.shape
    tm = TM_PROJ
    tps = seq // tm
    hw = 16
    row = lambda w: pl.BlockSpec((tm, w), lambda i: (i, 0))
    full = lambda a: pl.BlockSpec(a.shape, lambda i: (0,) * a.ndim)
    return pl.pallas_call(
        functools.partial(_outproj_kernel, tps),
        grid=(n // tm,),
        in_specs=[row(GW), row(ZC),
                  pl.BlockSpec((hw, ZC), lambda i: (jnp.maximum(i * (tm // hw) - 1, 0), 0)),
                  row(GW), row(GW), row(d),
                  pl.BlockSpec((1, 6, d), lambda i: (i // tps, 0, 0)),
                  full(wo), full(cw), full(cb), full(lng), full(lnb), full(rw), full(rb)],
        out_specs=[row(d), row(d), pl.BlockSpec((1, 8, tm), lambda i: (i, 0, 0)), row(LANES),
                   pl.BlockSpec((1, N_EXPERTS, LANES), lambda i: (i, 0, 0))],
        out_shape=[jax.ShapeDtypeStruct((n, d), F32), jax.ShapeDtypeStruct((n, d), BF16),
                   jax.ShapeDtypeStruct((n // tm, 8, tm), F32), jax.ShapeDtypeStruct((n, LANES), F32),
                   jax.ShapeDtypeStruct((n // tm, N_EXPERTS, LANES), F32)],
        compiler_params=_cparams(("parallel",)),
        name="out_proj_router",
    )(ya, zc, zc, yc, yd, x2, mod_l, wo, cw, cb, lng, lnb, rw, rb)


def _segment_pieces(count, fn):
    for b in reversed(range(SEG_PIECE_BITS)):
        size = SEG_ALIGN << b
        shift = SEG_ALIGN.bit_length() - 1 + b
        hi = lax.shift_left(lax.shift_right_logical(count, shift + 1), shift + 1)

        @pl.when((lax.shift_right_logical(count, shift) & 1) == 1)
        def _():
            fn(hi, size)


def _slot_positions(onehot_bf16, strict_lower_bf16, transposed):
    if transposed:
        return jnp.dot(strict_lower_bf16, onehot_bf16, preferred_element_type=F32)
    return lax.dot_general(onehot_bf16, strict_lower_bf16, (((1,), (1,)), ((), ())), preferred_element_type=F32)


def _dispatch_kernel(off_s, ng_s, gd_s, gs_s, gl_s, tail_s, u_ref, rt_ref, xs_ref, slab, zbuf, sem):
    blk = pl.program_id(0)
    tb = u_ref.shape[0]
    n_slots = slab.shape[1]
    e1 = rt_ref[0, 0:1, :]
    e2 = rt_ref[0, 1:2, :]
    sub_e = lax.broadcasted_iota(jnp.int32, (N_EXPERTS, tb), 0).astype(F32)
    hit1 = e1 == sub_e
    hit2 = e2 == sub_e
    onehot = jnp.where(hit1 | hit2, 1.0, 0.0).astype(BF16)
    r = lax.broadcasted_iota(jnp.int32, (tb, tb), 0)
    c = lax.broadcasted_iota(jnp.int32, (tb, tb), 1)
    lower = jnp.where(c < r, 1.0, 0.0).astype(BF16)
    rank = _slot_positions(onehot, lower, transposed=False)
    sub1 = lax.broadcasted_iota(jnp.int32, (N_EXPERTS, 1), 0)
    offv = jnp.zeros((N_EXPERTS, 1), F32)
    for e in range(N_EXPERTS):
        offv = jnp.where(sub1 == e, off_s[blk * N_EXPERTS + e].astype(F32), offv)
    pos = offv + rank
    pos1 = jnp.sum(jnp.where(hit1, pos, 0.0), axis=0, keepdims=True)
    pos2 = jnp.sum(jnp.where(hit2, pos, 0.0), axis=0, keepdims=True)
    slot = lax.broadcasted_iota(jnp.int32, (n_slots, tb), 0).astype(F32)
    perm = jnp.where((slot == pos1) | (slot == pos2), 1.0, 0.0).astype(BF16)
    cur = blk % 2
    slab[cur] = jnp.dot(perm, u_ref[...], preferred_element_type=F32)

    @pl.when(blk == 0)
    def _():
        zbuf[...] = jnp.zeros_like(zbuf)

    def copies(b, slot, go):
        def group(g, carry):
            dst = pl.multiple_of(gd_s[b * DISPATCH_GROUPS + g], SEG_ALIGN)
            go(pltpu.make_async_copy(slab.at[slot, pl.ds(pl.multiple_of(g * SEG_ALIGN, SEG_ALIGN), SEG_ALIGN)],
                                     xs_ref.at[pl.ds(dst, SEG_ALIGN)], sem.at[slot]))
            return carry
        lax.fori_loop(0, ng_s[b], group, 0)

        @pl.when(b == 0)
        def _():
            for e in range(N_EXPERTS):
                dst0 = gs_s[e]

                def piece(hi, size, dst0=dst0):
                    go(pltpu.make_async_copy(zbuf.at[pl.ds(0, size)],
                                             xs_ref.at[pl.ds(pl.multiple_of(dst0 + hi, SEG_ALIGN), size)],
                                             sem.at[slot]))
                _segment_pieces(gl_s[e], piece)

            zrows = zbuf.shape[0]

            def tail_chunk(i, carry):
                go(pltpu.make_async_copy(zbuf, xs_ref.at[pl.ds(pl.multiple_of(tail_s[0] + i * zrows, zrows), zrows)],
                                         sem.at[slot]))
                return carry
            lax.fori_loop(0, tail_s[1], tail_chunk, 0)

    copies(blk, cur, lambda cp: cp.start())

    @pl.when(blk > 0)
    def _():
        copies(blk - 1, 1 - cur, lambda cp: cp.wait())

    @pl.when(blk == pl.num_programs(0) - 1)
    def _():
        copies(blk, cur, lambda cp: cp.wait())


def _dispatch_call(meta, u2, rt, n_rows):
    n, d = u2.shape
    tb = DISPATCH_BLOCK
    per_tile = rt.shape[2] // tb
    grid_spec = pltpu.PrefetchScalarGridSpec(
        num_scalar_prefetch=6,
        grid=(n // tb,),
        in_specs=[pl.BlockSpec((tb, d), lambda i, *_: (i, 0)),
                  pl.BlockSpec((1, 8, tb), lambda i, *_: (i // per_tile, 0, i % per_tile))],
        out_specs=pl.BlockSpec(memory_space=pl.ANY),
        scratch_shapes=[pltpu.VMEM((2, DISPATCH_SLOTS, d), F32),
                        pltpu.VMEM((SEG_ALIGN << (SEG_PIECE_BITS - 1), d), F32),
                        pltpu.SemaphoreType.DMA((2,))],
    )
    return pl.pallas_call(
        _dispatch_kernel,
        grid_spec=grid_spec,
        out_shape=jax.ShapeDtypeStruct((n_rows, d), F32),
        compiler_params=_cparams(("arbitrary",)),
        name="moe_dispatch",
    )(meta["off"], meta["n_groups"], meta["group_row"], meta["gap_start"], meta["gap_len"], meta["tail"], u2, rt)


def _experts_kernel(te_s, na_s, x_ref, wg_ref, wu_ref, wd_ref, y_ref, wg_bf, wu_bf, wd_bf):
    i = pl.program_id(0)
    active = i < na_s[0]

    @pl.when(active & ((i == 0) | (te_s[i] != te_s[jnp.maximum(i - 1, 0)])))
    def _():
        wg_bf[...] = wg_ref[0].astype(BF16)
        wu_bf[...] = wu_ref[0].astype(BF16)
        wd_bf[...] = wd_ref[0].astype(BF16)

    @pl.when(active)
    def _():
        x = x_ref[...].astype(BF16)
        gate = jnp.dot(x, wg_bf[...], preferred_element_type=F32)
        up = jnp.dot(x, wu_bf[...], preferred_element_type=F32)
        hidden = (gate * _sigmoid(gate) * up).astype(BF16)
        y_ref[...] = jnp.dot(hidden, wd_bf[...], preferred_element_type=F32)

    @pl.when(jnp.logical_not(active))
    def _():
        y_ref[...] = jnp.zeros_like(y_ref)


def _experts_call(meta, xs, wg, wu, wd):
    n_rows, d = xs.shape
    tm = TM_MOE
    _, _, de = wg.shape
    tile = lambda i, te, na: jnp.minimum(i, na[0] - 1)
    grid_spec = pltpu.PrefetchScalarGridSpec(
        num_scalar_prefetch=2,
        grid=(n_rows // tm,),
        in_specs=[pl.BlockSpec((tm, d), lambda i, te, na: (tile(i, te, na), 0)),
                  pl.BlockSpec((1, d, de), lambda i, te, na: (te[tile(i, te, na)], 0, 0)),
                  pl.BlockSpec((1, d, de), lambda i, te, na: (te[tile(i, te, na)], 0, 0)),
                  pl.BlockSpec((1, de, d), lambda i, te, na: (te[tile(i, te, na)], 0, 0))],
        out_specs=pl.BlockSpec((tm, d), lambda i, te, na: (i, 0)),
        scratch_shapes=[pltpu.VMEM((d, de), BF16), pltpu.VMEM((d, de), BF16), pltpu.VMEM((de, d), BF16)],
    )
    return pl.pallas_call(
        _experts_kernel,
        grid_spec=grid_spec,
        out_shape=jax.ShapeDtypeStruct((n_rows, d), F32),
        compiler_params=_cparams(("arbitrary",)),
        name="experts",
    )(meta["tile_expert"], meta["n_active"], xs, wg, wu, wd)


def _combine_kernel(off_s, ng_s, gd_s, rtc_ref, x1_ref, mod_ref, lng_ref, lnb_ref, ys_ref, o_ref, slab, sem):
    blk = pl.program_id(0)
    tb = x1_ref.shape[0]
    n_slots = slab.shape[1]
    cur = blk % 2

    def copies(b, slot, go):
        def group(g, carry):
            src = pl.multiple_of(gd_s[b * DISPATCH_GROUPS + g], SEG_ALIGN)
            go(pltpu.make_async_copy(ys_ref.at[pl.ds(src, SEG_ALIGN)],
                                     slab.at[slot, pl.ds(pl.multiple_of(g * SEG_ALIGN, SEG_ALIGN), SEG_ALIGN)],
                                     sem.at[slot]))
            return carry
        lax.fori_loop(0, ng_s[b], group, 0)

    @pl.when(blk == 0)
    def _():
        slab[...] = jnp.zeros_like(slab)
        copies(blk, cur, lambda cp: cp.start())

    @pl.when(blk + 1 < pl.num_programs(0))
    def _():
        copies(blk + 1, 1 - cur, lambda cp: cp.start())

    rtc = rtc_ref[...]
    e1, e2, g1, g2 = rtc[:, 0:1], rtc[:, 1:2], rtc[:, 2:3], rtc[:, 3:4]
    lane_e = lax.broadcasted_iota(jnp.int32, (1, LANES), 1).astype(F32)
    hit1 = e1 == lane_e
    hit2 = e2 == lane_e
    onehot = jnp.where(hit1 | hit2, 1.0, 0.0).astype(BF16)
    r = lax.broadcasted_iota(jnp.int32, (tb, tb), 0)
    c = lax.broadcasted_iota(jnp.int32, (tb, tb), 1)
    lower = jnp.where(c < r, 1.0, 0.0).astype(BF16)
    rank = _slot_positions(onehot, lower, transposed=True)
    lane1 = lax.broadcasted_iota(jnp.int32, (1, LANES), 1)
    offv = jnp.zeros((1, LANES), F32)
    for e in range(N_EXPERTS):
        offv = jnp.where(lane1 == e, off_s[blk * N_EXPERTS + e].astype(F32), offv)
    pos = offv + rank
    pos1 = jnp.sum(jnp.where(hit1, pos, 0.0), axis=-1, keepdims=True)
    pos2 = jnp.sum(jnp.where(hit2, pos, 0.0), axis=-1, keepdims=True)
    slot = lax.broadcasted_iota(jnp.int32, (1, n_slots), 1).astype(F32)
    sel1 = jnp.where(slot == pos1, 1.0, 0.0).astype(BF16)
    sel2 = jnp.where(slot == pos2, 1.0, 0.0).astype(BF16)

    copies(blk, cur, lambda cp: cp.wait())
    ys = slab[cur].astype(BF16)
    y = (g1 * jnp.dot(sel1, ys, preferred_element_type=F32) + g2 * jnp.dot(sel2, ys, preferred_element_type=F32))
    mod = mod_ref[0]
    o_ref[...] = _layer_norm(ALPHA * x1_ref[...] + (1.0 + mod[5:6, :]) * y, lng_ref[...], lnb_ref[...])


def _combine_call(meta, ys, rtc, x1, mod_l, lng, lnb, seq):
    n, d = x1.shape
    tb = DISPATCH_BLOCK
    bps = seq // tb
    grid_spec = pltpu.PrefetchScalarGridSpec(
        num_scalar_prefetch=3,
        grid=(n // tb,),
        in_specs=[pl.BlockSpec((tb, LANES), lambda i, *_: (i, 0)),
                  pl.BlockSpec((tb, d), lambda i, *_: (i, 0)),
                  pl.BlockSpec((1, 6, d), lambda i, *_: (i // bps, 0, 0)),
                  pl.BlockSpec((1, d), lambda i, *_: (0, 0)),
                  pl.BlockSpec((1, d), lambda i, *_: (0, 0)),
                  pl.BlockSpec(memory_space=pl.ANY)],
        out_specs=pl.BlockSpec((tb, d), lambda i, *_: (i, 0)),
        scratch_shapes=[pltpu.VMEM((2, DISPATCH_SLOTS, d), F32), pltpu.SemaphoreType.DMA((2,))],
    )
    return pl.pallas_call(
        _combine_kernel,
        grid_spec=grid_spec,
        out_shape=jax.ShapeDtypeStruct((n, d), F32),
        compiler_params=_cparams(("arbitrary",)),
        name="moe_combine_ln2",
    )(meta["off"], meta["n_groups"], meta["group_row"], rtc, x1, mod_l, lng, lnb, ys)


def _sorted_rows_bound(n_tokens):
    n_blocks = n_tokens // DISPATCH_BLOCK
    worst = 2 * n_tokens + n_blocks * N_EXPERTS * (SEG_ALIGN - 1) + N_EXPERTS * (TM_MOE - 1)
    return -(-worst // TM_MOE) * TM_MOE


def _route_meta(cnt, n_rows):
    pc = (cnt + SEG_ALIGN - 1) // SEG_ALIGN * SEG_ALIGN
    off = jnp.cumsum(pc, axis=1) - pc
    tot = pc.sum(axis=0)
    reg = (tot + TM_MOE - 1) // TM_MOE * TM_MOE
    reg_end = jnp.cumsum(reg)
    ebase = reg_end - reg
    gbase = ebase[None, :] + jnp.cumsum(pc, axis=0) - pc
    tiles = jnp.arange(n_rows // TM_MOE, dtype=jnp.int32)
    tile_expert = jnp.minimum(jnp.sum(tiles[:, None] >= (reg_end // TM_MOE)[None, :], axis=1), N_EXPERTS - 1)
    g0 = jnp.arange(DISPATCH_GROUPS, dtype=jnp.int32) * SEG_ALIGN
    seg_of = jnp.minimum(jnp.sum(g0[None, :, None] >= (off + pc)[:, None, :], axis=-1), N_EXPERTS - 1)
    group_row = jnp.take_along_axis(gbase - off, seg_of, axis=1) + g0[None, :]
    i32 = lambda a: a.astype(jnp.int32).reshape(-1)
    used = reg_end[-1:]
    tail_chunk = SEG_ALIGN << (SEG_PIECE_BITS - 1)
    return {"off": i32(off), "n_groups": i32(pc.sum(axis=1) // SEG_ALIGN), "group_row": i32(group_row),
            "gap_start": i32(ebase + tot),
            "gap_len": i32(reg - tot), "tail": i32(jnp.concatenate([used, (n_rows - used) // tail_chunk])),
            "tile_expert": i32(tile_expert), "n_active": i32(used // TM_MOE)}


def _rearranged_w_in(w_in_l):
    d = w_in_l.shape[0]
    zeros = lambda w: jnp.zeros((d, w), w_in_l.dtype)
    o_gate, o_cb, o_dq, o_dk, o_dv, o_g, o_ga = 1024, 1032, 1800, 2056, 2312, 2568, 3592
    v_cols = []
    for h in range(HEADS):
        v_cols += [w_in_l[:, o_dv + h * HEAD_DIM:o_dv + (h + 1) * HEAD_DIM], zeros(LANES - HEAD_DIM)]
    small = [w_in_l[:, o_gate:o_gate + 2 * HEADS], w_in_l[:, o_ga:o_ga + GLA_RANK],
             zeros(ZS - 2 * HEADS - GLA_RANK)]
    cols = [w_in_l[:, 0:ZM], w_in_l[:, o_cb:o_cb + ZC], w_in_l[:, o_dq:o_dq + ZQ], w_in_l[:, o_dk:o_dk + ZK]]
    cols += v_cols + [w_in_l[:, o_g:o_g + ZG]] + small
    return jnp.concatenate(cols, axis=1).astype(BF16)


def _rope_tables(positions):
    inv_freq = ROPE_THETA ** (-jnp.arange(0, ROPE_DIM, 2, dtype=F32) / ROPE_DIM)
    ang = positions.astype(F32).reshape(-1, 1) * inv_freq
    cos, sin = jnp.cos(ang), jnp.sin(ang)
    d = np.arange(ZQ) % DIFF_DIM
    half = ROPE_DIM // 2
    idx = d % half
    cosf = jnp.where(d < ROPE_DIM, cos[:, idx], 1.0)
    sinf = jnp.where(d < half, -sin[:, idx], jnp.where(d < ROPE_DIM, sin[:, idx], 0.0))
    return cosf, sinf


def _pad_lanes(v, width=LANES):
    v = v.reshape(1, -1)
    return jnp.pad(v, ((0, 0), (0, width - v.shape[1])))


def kernel(x, c, positions, w_ada, b_ada, w_in, w_out, mlstm_b_i, mlstm_b_f, mlstm_norm_g, conv_w, conv_b, diff_lambda_q1, diff_lambda_k1, diff_lambda_q2, diff_lambda_k2, diff_norm_g, gla_w_a2, gla_b_a, gla_norm_g, ln1_g, ln1_b, ln2_g, ln2_b, router_w, router_bias, exp_w_gate, exp_w_up, exp_w_down):
    bsz, seq, d = x.shape
    depth = w_in.shape[0]
    n = bsz * seq
    x2 = x.reshape(n, d)
    mod = _mod_call(c, w_ada, b_ada)
    cosf, sinf = _rope_tables(positions)
    rw_t = router_w.T.astype(BF16)
    rb = router_bias.reshape(N_EXPERTS, 1).astype(F32)
    n_sorted = _sorted_rows_bound(n)
    for l in range(depth):
        lam_init = 0.8 - 0.6 * math.exp(-0.3 * l)
        w_l = _rearranged_w_in(w_in[l])
        zm, zc, zq, zk, zv, zg, zs = _inproj_call(x2, mod[l], w_l, cosf, sinf, seq)
        bif = _pad_lanes(jnp.concatenate([mlstm_b_i[l], mlstm_b_f[l]]))
        ya = _mlstm_call(zm, zs, bif, mlstm_norm_g[l].reshape(1, -1), bsz, seq)
        lamp = jnp.concatenate([_pad_lanes(diff_lambda_q1[l]), _pad_lanes(diff_lambda_k1[l]),
                                _pad_lanes(diff_lambda_q2[l]), _pad_lanes(diff_lambda_k2[l])], axis=0)
        yc = _attn_call(zq, zk, zv, lamp, diff_norm_g[l].reshape(1, -1), lam_init, bsz, seq)
        wa = jnp.pad(gla_w_a2[l], ((2 * HEADS, ZS - 2 * HEADS - GLA_RANK), (0, 0))).astype(BF16)
        yd = _gla_call(zg, zs, wa, gla_b_a[l].reshape(1, -1), gla_norm_g[l].reshape(1, -1), bsz, seq)
        x1, u2, rt, rtc, cnt = _outproj_call(ya, zc, yc, yd, x2, mod[l], w_out[l].astype(BF16), conv_w[l],
                                             conv_b[l].reshape(1, -1), ln1_g[l].reshape(1, -1),
                                             ln1_b[l].reshape(1, -1), rw_t, rb, seq)
        per_tile = TM_PROJ // DISPATCH_BLOCK
        cnt_blk = cnt[:, :, :per_tile].transpose(0, 2, 1).reshape(-1, N_EXPERTS).astype(jnp.int32)
        meta = _route_meta(cnt_blk, n_sorted)
        xs = _dispatch_call(meta, u2, rt, n_sorted)
        ys = _experts_call(meta, xs, exp_w_gate[l], exp_w_up[l], exp_w_down[l])
        x2 = _combine_call(meta, ys, rtc, x1, mod[l], ln2_g[l].reshape(1, -1), ln2_b[l].reshape(1, -1), seq)
    return x2.reshape(bsz, seq, d)
```

```python
import functools
import math

import numpy as np
import jax
import jax.numpy as jnp
from jax import lax
from jax.experimental import pallas as pl
from jax.experimental.pallas import tpu as pltpu

F32 = jnp.float32
BF16 = jnp.bfloat16

D_MODEL = 1024
DEPTH = 4
HEADS = 4
GW = 256
HEAD_DIM = 64
DIFF_DIM = 32
ROPE_DIM = 8
ROPE_THETA = 500000.0
CONV_WIDTH = 3
GLA_RANK = 16
GLA_TAU = 16.0
CHUNK = 64
GLA_CHUNK = 64
N_EXPERTS = 16
N_GROUPS = 4
GROUP_SIZE = 4
D_EXPERT = 512
ALPHA = (2 * DEPTH) ** 0.25
EPS = 1e-6
LANES = 128
SUBLANES = 8
NEG_INF = float("-inf")
LOG2_E = math.log2(math.e)

ZM, ZC, ZQ, ZK, ZV, ZG, ZS = 1024, 768, 256, 256, 512, 1024, 128
Z_OFFS = np.cumsum([0, ZM, ZC, ZQ, ZK, ZV, ZG, ZS])
Z_TOTAL = int(Z_OFFS[-1])

TM_PROJ = 512
TB_MLSTM = 256
TB_GLA = 512
CUM_SPAN = 128
TQ_ATTN = 512
TM_MOE = 512
DISPATCH_BLOCK = 256
SEG_ALIGN = 8
SEG_PIECE_BITS = 6
DISPATCH_SLOTS = 2 * DISPATCH_BLOCK + N_EXPERTS * SEG_ALIGN
assert SEG_ALIGN << (SEG_PIECE_BITS - 1) >= DISPATCH_BLOCK and SEG_ALIGN << SEG_PIECE_BITS >= TM_MOE
VMEM_LIMIT = 56 * 1024 * 1024


def _cparams(sem):
    return pltpu.CompilerParams(dimension_semantics=sem, vmem_limit_bytes=VMEM_LIMIT)


def _log_sigmoid(x):
    return jnp.minimum(x, 0.0) - jnp.log(1.0 + jnp.exp(-jnp.abs(x)))


def _sigmoid(x):
    return 1.0 / (1.0 + jnp.exp(-x))


def _layer_norm(v, g, b):
    mu = jnp.mean(v, axis=-1, keepdims=True)
    d = v - mu
    var = jnp.mean(d * d, axis=-1, keepdims=True)
    return d * lax.rsqrt(var + EPS) * g + b


def _mod_kernel(c_ref, w_ref, b_ref, o_ref):
    c = c_ref[...]
    c_act = (c * _sigmoid(c)).astype(BF16)
    o_ref[0, 0] = jnp.dot(c_act, w_ref[0].astype(BF16), preferred_element_type=F32) + b_ref[0, 0]


def _mod_call(c, w_ada, b_ada):
    depth, d, _ = w_ada.shape
    bsz = c.shape[0]
    b3 = b_ada.reshape(depth, 6, 1, d)
    out = pl.pallas_call(
        _mod_kernel,
        grid=(depth, 6),
        in_specs=[pl.BlockSpec((bsz, d), lambda l, j: (0, 0)),
                  pl.BlockSpec((1, d, d), lambda l, j: (l, 0, j)),
                  pl.BlockSpec((1, 1, 1, d), lambda l, j: (l, j, 0, 0))],
        out_specs=pl.BlockSpec((1, 1, bsz, d), lambda l, j: (l, j, 0, 0)),
        out_shape=jax.ShapeDtypeStruct((depth, 6, bsz, d), F32),
        compiler_params=_cparams(("arbitrary", "arbitrary")),
        name="adaln_mod",
    )(c, w_ada, b3)
    return out.transpose(0, 2, 1, 3)


def _inproj_kernel(x_ref, mod_ref, w_ref, cos_ref, sin_ref, zm_ref, zc_ref, zq_ref, zk_ref, zvt_ref, zg_ref,
                   zs_ref):
    mod = mod_ref[0]
    u = (x_ref[...] * (1.0 + mod[1:2, :]) + mod[0:1, :]).astype(BF16)

    def proj(idx):
        return jnp.dot(u, w_ref[:, int(Z_OFFS[idx]):int(Z_OFFS[idx + 1])], preferred_element_type=F32)

    zm_ref[...] = proj(0).astype(BF16)
    zc_ref[...] = proj(1).astype(BF16)

    lane = lax.broadcasted_iota(jnp.int32, (1, ZQ), 1)
    first_half = (lane % ROPE_DIM) < (ROPE_DIM // 2)
    cosf = cos_ref[...]
    sinf = sin_ref[...]

    def rope(r):
        swapped = jnp.where(first_half, pltpu.roll(r, ZQ - ROPE_DIM // 2, 1), pltpu.roll(r, ROPE_DIM // 2, 1))
        return r * cosf + swapped * sinf

    zq_ref[...] = (rope(proj(2)) * (LOG2_E * DIFF_DIM ** -0.5)).astype(BF16)
    zk_ref[...] = rope(proj(3)).astype(BF16)
    lane_v = lax.broadcasted_iota(jnp.int32, (1, ZV), 1)
    ones_col = jnp.where((lane_v % LANES) == HEAD_DIM, 1.0, 0.0)
    zvt_ref[...] = (proj(4) + ones_col).T.astype(BF16)
    zg_ref[...] = proj(5).astype(BF16)
    zs_ref[...] = proj(6)


def _inproj_call(x2, mod_l, w_l, cosf, sinf, seq):
    n, d = x2.shape
    tm = TM_PROJ
    tps = seq // tm
    row = lambda w: pl.BlockSpec((tm, w), lambda i: (i, 0))
    outs = [(ZM, BF16), (ZC, BF16), (ZQ, BF16), (ZK, BF16), (ZV, BF16), (ZG, BF16), (ZS, F32)]
    return pl.pallas_call(
        _inproj_kernel,
        grid=(n // tm,),
        in_specs=[row(d),
                  pl.BlockSpec((1, 6, d), lambda i: (i // tps, 0, 0)),
                  pl.BlockSpec((d, Z_TOTAL), lambda i: (0, 0)),
                  row(ZQ), row(ZQ)],
        out_specs=[pl.BlockSpec((ZV, tm), lambda i: (0, i)) if k == 4 else row(w) for k, (w, _) in enumerate(outs)],
        out_shape=[jax.ShapeDtypeStruct((ZV, n) if k == 4 else (n, w), dt) for k, (w, dt) in enumerate(outs)],
        compiler_params=_cparams(("parallel",)),
        name="in_proj",
    )(x2, mod_l, w_l, cosf, sinf)


NT_DIMS = (((1,), (1,)), ((), ()))
TN_DIMS = (((0,), (0,)), ((), ()))


def _head_of(idx):
    return idx // HEAD_DIM


def _cumsum_matrix(span, chunk):
    r = lax.broadcasted_iota(jnp.int32, (span, span), 0)
    c = lax.broadcasted_iota(jnp.int32, (span, span), 1)
    return jnp.where(((r // chunk) == (c // chunk)) & (c <= r), 1.0, 0.0).astype(BF16)


def _bf16_terms(x, n_terms):
    terms, rest = [], x
    for _ in range(n_terms):
        t = rest.astype(BF16)
        terms.append(t)
        rest = rest - t.astype(F32)
    return terms


def _exact_matmul(lhs_f32, rhs01, n_terms=3):
    return functools.reduce(lambda a, b: a + b, [jnp.dot(t, rhs01, preferred_element_type=F32)
                                                 for t in reversed(_bf16_terms(lhs_f32, n_terms))])


def _chunk_cumsum(x, chunk):
    cm = _cumsum_matrix(CUM_SPAN, chunk)
    outs = []
    for i in range(x.shape[0] // CUM_SPAN):
        terms = _bf16_terms(x[i * CUM_SPAN:(i + 1) * CUM_SPAN], 3)
        outs.append(functools.reduce(lambda a, b: a + b,
                                     [jnp.dot(cm, t, preferred_element_type=F32) for t in reversed(terms)]))
    return jnp.concatenate(outs, axis=0)


def _rows_broadcast(x, chunk, row):
    n = x.shape[0] // chunk
    return jnp.concatenate([jnp.broadcast_to(x[c * chunk + row:c * chunk + row + 1, :], (chunk, x.shape[1]))
                            for c in range(n)], axis=0)


def _head_masks(chunk):
    lane_head = _head_of(lax.broadcasted_iota(jnp.int32, (1, GW), 1))
    row_head = lax.broadcasted_iota(jnp.int32, (HEADS * chunk, 1), 0) // chunk
    stack_keep = row_head == lane_head
    rr = lax.broadcasted_iota(jnp.int32, (HEADS * chunk, chunk), 0) % chunk
    cc = lax.broadcasted_iota(jnp.int32, (HEADS * chunk, chunk), 1)
    block_diag = (_head_of(lax.broadcasted_iota(jnp.int32, (GW, GW), 0))
                  == _head_of(lax.broadcasted_iota(jnp.int32, (GW, GW), 1)))
    return stack_keep, cc <= rr, block_diag


def _unstack_heads(x4, keep, chunk):
    x4 = jnp.where(keep, x4, 0.0)
    return functools.reduce(lambda a, b: a + b, [x4[h * chunk:(h + 1) * chunk] for h in range(HEADS)])


def _head_rms_norm(x, block_diag):
    seg = jnp.where(block_diag, 1.0, 0.0).astype(BF16)
    return x * lax.rsqrt(_exact_matmul(x * x, seg, 2) * (1.0 / HEAD_DIM) + EPS)


def _mlstm_kernel(zm_ref, zs_ref, bif_ref, g_ref, o_ref, ct_scr, nm_scr, ni_scr, up_scr):
    @pl.when(pl.program_id(1) == 0)
    def _():
        ct_scr[...] = jnp.zeros_like(ct_scr)
        nm_scr[...] = jnp.zeros_like(nm_scr)

    tb = zm_ref.shape[0]
    L = CHUNK
    nc = tb // L
    lane = lax.broadcasted_iota(jnp.int32, (1, LANES), 1)
    graw = zs_ref[...] + bif_ref[...]
    gp = jnp.where(lane < HEADS, graw, jnp.where(lane < 2 * HEADS, _log_sigmoid(graw), 0.0))
    cum = _chunk_cumsum(gp, L)

    def expand(x, first_lane):
        sel = jnp.where((lax.broadcasted_iota(jnp.int32, (LANES, GW), 0) - first_lane)
                        == _head_of(lax.broadcasted_iota(jnp.int32, (LANES, GW), 1)), 1.0, 0.0).astype(BF16)
        return _exact_matmul(x, sel)

    bx = expand(cum, HEADS)
    lix = expand(gp, 0)
    bex = _rows_broadcast(bx, L, L - 1)
    dkx = bex - bx + lix
    dkmax = [jnp.max(dkx[c * L:(c + 1) * L], axis=0, keepdims=True) for c in range(nc)]
    kw = (zm_ref[:, GW:2 * GW].astype(F32)
          * jnp.exp(dkx - jnp.concatenate([jnp.broadcast_to(d, (L, GW)) for d in dkmax], axis=0))).astype(BF16)
    rowv_t = (cum - pltpu.roll(gp, HEADS, 1)).T
    stack_keep, causal4, block_diag = _head_masks(L)
    stack_keep2 = jnp.concatenate([stack_keep, stack_keep], axis=1)
    ones_blk = jnp.ones((L, GW), BF16)

    dmaxx = []
    for c in range(nc):
        rs = slice(c * L, (c + 1) * L)
        qc = zm_ref[rs, 0:GW] * (HEAD_DIM ** -0.5)
        kc = zm_ref[rs, GW:2 * GW]
        vc = zm_ref[rs, 2 * GW:3 * GW]
        q4 = jnp.where(stack_keep, jnp.concatenate([qc] * HEADS, axis=0), jnp.zeros((), BF16))
        s4 = lax.dot_general(q4, kc, NT_DIMS, preferred_element_type=F32)
        bcol4 = jnp.concatenate([cum[rs, HEADS + h:HEADS + h + 1] for h in range(HEADS)], axis=0)
        brow4 = jnp.concatenate([jnp.broadcast_to(rowv_t[HEADS + h:HEADS + h + 1, rs], (L, L))
                                 for h in range(HEADS)], axis=0)
        d4 = jnp.where(causal4, bcol4 - brow4, NEG_INF)
        dmax4 = jnp.max(d4, axis=-1, keepdims=True)
        p4 = (s4 * jnp.exp(d4 - dmax4)).astype(BF16)
        i4 = jnp.dot(p4, jnp.concatenate([vc, ones_blk], axis=1), preferred_element_type=F32)
        ni_scr[rs, :] = _unstack_heads(i4, stack_keep2, L)
        dmaxx.append(_unstack_heads(jnp.broadcast_to(dmax4, (HEADS * L, GW)), stack_keep, L))
        up = lax.dot_general(vc, kw[rs], TN_DIMS, preferred_element_type=F32)
        up_scr[c] = jnp.where(block_diag, up, 0.0)

    ct = ct_scr[...]
    n_row = nm_scr[0:1, :]
    m_prev = nm_scr[1:2, :]
    hv_rows = []
    for c in range(nc):
        rs = slice(c * L, (c + 1) * L)
        qc = zm_ref[rs, 0:GW] * (HEAD_DIM ** -0.5)
        nmat = jnp.where(block_diag, jnp.broadcast_to(n_row, (GW, GW)), 0.0).astype(BF16)
        num_inter = lax.dot_general(qc, ct.astype(BF16), NT_DIMS, preferred_element_type=F32)
        den_inter = lax.dot_general(qc, nmat, NT_DIMS, preferred_element_type=F32)
        inter = bx[rs] + m_prev
        m_row = jnp.maximum(inter, dmaxx[c])
        e1 = jnp.exp(dmaxx[c] - m_row)
        e2 = jnp.exp(inter - m_row)
        num = e1 * ni_scr[rs, 0:GW] + e2 * num_inter
        den = e1 * ni_scr[rs, GW:2 * GW] + e2 * den_inter
        hv_rows.append(num / jnp.maximum(jnp.abs(den), jnp.exp(-m_row)))
        b_end = bex[c * L:c * L + 1, :]
        m_new = jnp.maximum(b_end + m_prev, dkmax[c])
        decay = jnp.exp(b_end + m_prev - m_new)
        f2 = jnp.exp(dkmax[c] - m_new)
        ct = ct * decay + up_scr[c] * f2
        n_row = n_row * decay + f2 * jnp.sum(kw[rs].astype(F32), axis=0, keepdims=True)
        m_prev = m_new
    ct_scr[...] = ct
    nm_scr[0:1, :] = n_row
    nm_scr[1:2, :] = m_prev

    hn = _head_rms_norm(jnp.concatenate(hv_rows, axis=0), block_diag)
    o_ref[...] = (hn * g_ref[...] * _sigmoid(zm_ref[:, 3 * GW:4 * GW].astype(F32))).astype(BF16)


def _mlstm_call(zm, zs, bif, gain, bsz, seq):
    tb = TB_MLSTM
    tps = seq // tb
    return pl.pallas_call(
        _mlstm_kernel,
        grid=(bsz, tps),
        in_specs=[pl.BlockSpec((tb, ZM), lambda b, t: (b * tps + t, 0)),
                  pl.BlockSpec((tb, ZS), lambda b, t: (b * tps + t, 0)),
                  pl.BlockSpec((1, LANES), lambda b, t: (0, 0)),
                  pl.BlockSpec((1, GW), lambda b, t: (0, 0))],
        out_specs=pl.BlockSpec((tb, GW), lambda b, t: (b * tps + t, 0)),
        out_shape=jax.ShapeDtypeStruct((bsz * seq, GW), BF16),
        scratch_shapes=[pltpu.VMEM((GW, GW), F32),
                        pltpu.VMEM((8, GW), F32),
                        pltpu.VMEM((tb, 2 * GW), F32),
                        pltpu.VMEM((tb // CHUNK, GW, GW), F32)],
        compiler_params=_cparams(("parallel", "arbitrary")),
        name="mlstm",
    )(zm, zs, bif, jnp.tile(gain, (1, HEADS)))


def _gla_kernel(zg_ref, zs_ref, wa_ref, ba_ref, g_ref, o_ref, st_scr, oi_scr, up_scr):
    @pl.when(pl.program_id(1) == 0)
    def _():
        st_scr[...] = jnp.zeros_like(st_scr)

    tb = zg_ref.shape[0]
    lg = GLA_CHUNK
    nc = tb // lg
    la = _log_sigmoid(jnp.dot(zs_ref[...].astype(BF16), wa_ref[...], preferred_element_type=F32)
                      + ba_ref[...]) * (1.0 / GLA_TAU)
    bc = _chunk_cumsum(la, lg)
    be = _rows_broadcast(bc, lg, lg - 1)
    bm = _rows_broadcast(bc, lg, lg // 2)
    q = zg_ref[:, 0:GW].astype(F32) * (HEAD_DIM ** -0.5)
    k = zg_ref[:, GW:2 * GW].astype(F32)
    qm = (q * jnp.exp(bc - bm)).astype(BF16)
    km = (k * jnp.exp(bm - bc)).astype(BF16)
    qt = (q * jnp.exp(bc)).astype(BF16)
    ks = (k * jnp.exp(be - bc)).astype(BF16)
    dec = jnp.exp(be)
    stack_keep, causal4, block_diag = _head_masks(lg)

    for c in range(nc):
        rs = slice(c * lg, (c + 1) * lg)
        vc = zg_ref[rs, 2 * GW:3 * GW]
        q4 = jnp.where(stack_keep, jnp.concatenate([qm[rs]] * HEADS, axis=0), jnp.zeros((), BF16))
        att4 = lax.dot_general(q4, km[rs], NT_DIMS, preferred_element_type=F32)
        att4 = jnp.where(causal4, att4, 0.0).astype(BF16)
        oi_scr[rs, :] = _unstack_heads(jnp.dot(att4, vc, preferred_element_type=F32), stack_keep, lg)
        up = lax.dot_general(vc, ks[rs], TN_DIMS, preferred_element_type=F32)
        up_scr[c] = jnp.where(block_diag, up, 0.0)

    st = st_scr[...]
    inter = []
    for c in range(nc):
        rs = slice(c * lg, (c + 1) * lg)
        inter.append(lax.dot_general(qt[rs], st.astype(BF16), NT_DIMS, preferred_element_type=F32))
        st = st * dec[c * lg:c * lg + 1, :] + up_scr[c]
    st_scr[...] = st

    on = _head_rms_norm(oi_scr[...] + jnp.concatenate(inter, axis=0), block_diag)
    r = zg_ref[:, 3 * GW:4 * GW].astype(F32)
    o_ref[...] = (on * g_ref[...] * (r * _sigmoid(r))).astype(BF16)


def _gla_call(zg, zs, wa, ba, gain, bsz, seq):
    tb = TB_GLA
    tps = seq // tb
    return pl.pallas_call(
        _gla_kernel,
        grid=(bsz, tps),
        in_specs=[pl.BlockSpec((tb, ZG), lambda b, t: (b * tps + t, 0)),
                  pl.BlockSpec((tb, ZS), lambda b, t: (b * tps + t, 0)),
                  pl.BlockSpec((ZS, GW), lambda b, t: (0, 0)),
                  pl.BlockSpec((1, GW), lambda b, t: (0, 0)),
                  pl.BlockSpec((1, GW), lambda b, t: (0, 0))],
        out_specs=pl.BlockSpec((tb, GW), lambda b, t: (b * tps + t, 0)),
        out_shape=jax.ShapeDtypeStruct((bsz * seq, GW), BF16),
        scratch_shapes=[pltpu.VMEM((GW, GW), F32),
                        pltpu.VMEM((tb, GW), F32),
                        pltpu.VMEM((tb // GLA_CHUNK, GW, GW), F32)],
        compiler_params=_cparams(("parallel", "arbitrary")),
        name="gla",
    )(zg, zs, wa, ba, jnp.tile(gain, (1, HEADS)))


def _attn_kernel(lam_init, q_ref, k_ref, vt_ref, lamp_ref, g_ref, o_ref, qm_scr, m_scr, acc_scr, st_scr):
    tq = q_ref.shape[0]
    qi = pl.program_id(1)
    n_hc = 2 * HEADS
    lane = lax.broadcasted_iota(jnp.int32, (1, LANES), 1)
    for hc in range(n_hc):
        h, c = divmod(hc, 2)
        pair = h // 2
        lo = (h % 2) * HEAD_DIM + c * DIFF_DIM
        qp = q_ref[:, pair * LANES:(pair + 1) * LANES]
        qm_scr[hc] = jnp.where((lane >= lo) & (lane < lo + DIFF_DIM), qp, jnp.zeros_like(qp))
    m_scr[...] = jnp.full(m_scr.shape, NEG_INF, F32)
    acc_scr[...] = jnp.zeros_like(acc_scr)

    def sublane_allmax(x):
        for shift in (4, 2, 1):
            x = jnp.maximum(x, pltpu.roll(x, shift, 0))
        return x

    def kv_block(j, keep_t):
        key0 = pl.multiple_of(j * tq, tq)
        for hc in range(n_hc):
            pair = (hc // 2) // 2
            kb = k_ref[pl.ds(key0, tq), pair * LANES:(pair + 1) * LANES]
            st_scr[hc] = lax.dot_general(kb, qm_scr[hc], NT_DIMS, preferred_element_type=F32)
        for hc in range(n_hc):
            h = hc // 2
            vt = vt_ref[h * LANES:(h + 1) * LANES, pl.ds(key0, tq)]
            st = st_scr[hc]
            if keep_t is not None:
                st = jnp.where(keep_t, st, NEG_INF)
            st3 = st.reshape(tq // SUBLANES, SUBLANES, tq)
            m_old = m_scr[hc]
            m_new = jnp.maximum(m_old, sublane_allmax(jnp.max(st3, axis=0)))
            alpha = jnp.exp2(m_old - m_new)
            p = jnp.exp2(st3 - m_new[None]).reshape(tq, tq).astype(BF16)
            acc = acc_scr[hc].reshape(LANES // SUBLANES, SUBLANES, tq) * alpha[None]
            acc_scr[hc] = acc.reshape(LANES, tq) + jnp.dot(vt, p, preferred_element_type=F32)
            m_scr[hc] = m_new

    def full_block(j, carry):
        kv_block(j, None)
        return carry

    lax.fori_loop(0, qi, full_block, 0)
    key_idx = lax.broadcasted_iota(jnp.int32, (tq, tq), 0)
    query_idx = lax.broadcasted_iota(jnp.int32, (tq, tq), 1)
    kv_block(qi, key_idx <= query_idx)

    lamp = lamp_ref[...]
    lam = (jnp.exp(jnp.sum(lamp[0:1, :] * lamp[1:2, :], axis=-1, keepdims=True))
           - jnp.exp(jnp.sum(lamp[2:3, :] * lamp[3:4, :], axis=-1, keepdims=True)) + lam_init)
    gain = g_ref[...] * (1.0 - lam_init)
    outs = []
    for h in range(HEADS):
        a1 = acc_scr[2 * h]
        a2 = acc_scr[2 * h + 1]
        o = a1[:HEAD_DIM] / a1[HEAD_DIM:HEAD_DIM + 1] - lam * (a2[:HEAD_DIM] / a2[HEAD_DIM:HEAD_DIM + 1])
        on = o * lax.rsqrt(jnp.mean(o * o, axis=0, keepdims=True) + EPS)
        outs.append(on.T * gain)
    o_ref[...] = jnp.concatenate(outs, axis=1).astype(BF16)


def _attn_call(zq, zk, zvt, lamp, gain, lam_init, bsz, seq):
    tq = TQ_ATTN
    nq = seq // tq
    return pl.pallas_call(
        functools.partial(_attn_kernel, lam_init),
        grid=(bsz, nq),
        in_specs=[pl.BlockSpec((tq, ZQ), lambda b, i: (b * nq + i, 0)),
                  pl.BlockSpec((seq, ZK), lambda b, i: (b, 0)),
                  pl.BlockSpec((ZV, seq), lambda b, i: (0, b)),
                  pl.BlockSpec((4, LANES), lambda b, i: (0, 0)),
                  pl.BlockSpec((1, HEAD_DIM), lambda b, i: (0, 0))],
        out_specs=pl.BlockSpec((tq, GW), lambda b, i: (b * nq + i, 0)),
        out_shape=jax.ShapeDtypeStruct((bsz * seq, GW), BF16),
        scratch_shapes=[pltpu.VMEM((2 * HEADS, tq, LANES), BF16),
                        pltpu.VMEM((2 * HEADS, SUBLANES, tq), F32),
                        pltpu.VMEM((2 * HEADS, LANES, tq), F32),
                        pltpu.VMEM((2 * HEADS, tq, tq), F32)],
        compiler_params=_cparams(("parallel", "arbitrary")),
        name="diff_attn",
    )(zq, zk, zvt, lamp, gain)


def _outproj_kernel(tiles_per_seq, ya_ref, zc_ref, halo_ref, yc_ref, yd_ref, x_ref, mod_ref, wo_ref, cw_ref,
                    cb_ref, lng_ref, lnb_ref, rw_ref, rb_ref, x1_ref, u2_ref, rt_ref, rtc_ref, cnt_ref):
    tm = x_ref.shape[0]
    mod = mod_ref[0]
    zc = zc_ref[...].astype(F32)
    halo = halo_ref[...].astype(F32)
    first = (pl.program_id(0) % tiles_per_seq) == 0
    s_cur = zc[:, GW:2 * GW] * zc[:, 2 * GW:3 * GW]
    s_halo = jnp.where(first, 0.0, halo[:, GW:2 * GW] * halo[:, 2 * GW:3 * GW])
    s_ext = jnp.concatenate([s_halo, s_cur], axis=0)
    hw = halo.shape[0]
    cw = cw_ref[...]
    conv = (cw[0:1, :] * s_ext[hw - 2:hw - 2 + tm, :] + cw[1:2, :] * s_ext[hw - 1:hw - 1 + tm, :]
            + cw[2:3, :] * s_cur + cb_ref[...])
    yb = (zc[:, 0:GW] * conv).astype(BF16)

    y = (jnp.dot(ya_ref[...], wo_ref[0:GW, :], preferred_element_type=F32)
         + jnp.dot(yb, wo_ref[GW:2 * GW, :], preferred_element_type=F32)
         + jnp.dot(yc_ref[...], wo_ref[2 * GW:3 * GW, :], preferred_element_type=F32)
         + jnp.dot(yd_ref[...], wo_ref[3 * GW:4 * GW, :], preferred_element_type=F32))
    x1 = _layer_norm(ALPHA * x_ref[...] + (1.0 + mod[2:3, :]) * y, lng_ref[...], lnb_ref[...])
    x1_ref[...] = x1
    u2 = (x1 * (1.0 + mod[4:5, :]) + mod[3:4, :]).astype(BF16)
    u2_ref[...] = u2

    logits = lax.dot_general(rw_ref[...], u2, (((1,), (1,)), ((), ())), preferred_element_type=F32)
    scores = _sigmoid(logits)
    biased = scores + rb_ref[...]
    rows = [biased[e:e + 1, :] for e in range(N_EXPERTS)]
    srow = [scores[e:e + 1, :] for e in range(N_EXPERTS)]
    gscore = []
    for g in range(N_GROUPS):
        a = rows[g * GROUP_SIZE:(g + 1) * GROUP_SIZE]
        best = None
        for i in range(GROUP_SIZE):
            for j in range(i + 1, GROUP_SIZE):
                pair_sum = a[i] + a[j]
                best = pair_sum if best is None else jnp.maximum(best, pair_sum)
        gscore.append(best)
    gmax = functools.reduce(jnp.maximum, gscore)
    gsel = jnp.full(gmax.shape, N_GROUPS, jnp.int32)
    for g in reversed(range(N_GROUPS)):
        gsel = jnp.where(gscore[g] == gmax, g, gsel)
    masked = [jnp.where(gsel == (e // GROUP_SIZE), rows[e], NEG_INF) for e in range(N_EXPERTS)]
    m1 = functools.reduce(jnp.maximum, masked)
    i1 = jnp.full(m1.shape, N_EXPERTS, jnp.int32)
    for e in reversed(range(N_EXPERTS)):
        i1 = jnp.where(masked[e] == m1, e, i1)
    masked2 = [jnp.where(i1 == e, NEG_INF, masked[e]) for e in range(N_EXPERTS)]
    m2 = functools.reduce(jnp.maximum, masked2)
    i2 = jnp.full(m2.shape, N_EXPERTS, jnp.int32)
    for e in reversed(range(N_EXPERTS)):
        i2 = jnp.where((masked2[e] == m2) & (i1 != e), e, i2)
    sel1 = functools.reduce(lambda acc, e: acc + jnp.where(i1 == e, srow[e], 0.0), range(N_EXPERTS), 0.0)
    sel2 = functools.reduce(lambda acc, e: acc + jnp.where(i2 == e, srow[e], 0.0), range(N_EXPERTS), 0.0)
    tot = sel1 + sel2
    g1 = sel1 / tot
    g2 = sel2 / tot
    route = jnp.concatenate([i1.astype(F32), i2.astype(F32), g1, g2, jnp.zeros((4, tm), F32)], axis=0)
    rt_ref[0] = route
    rtc_ref[...] = jnp.concatenate([route, jnp.zeros((LANES - 8, tm), F32)], axis=0).T
    sub = lax.broadcasted_iota(jnp.int32, (N_EXPERTS, tm), 0)
    onehot = jnp.where((i1 == sub) | (i2 == sub), 1.0, 0.0)
    lane = lax.broadcasted_iota(jnp.int32, (1, LANES), 1)
    cnt = jnp.zeros((N_EXPERTS, LANES), F32)
    for j in range(tm // DISPATCH_BLOCK):
        cj = jnp.sum(onehot[:, j * DISPATCH_BLOCK:(j + 1) * DISPATCH_BLOCK], axis=-1, keepdims=True)
        cnt = jnp.where(lane == j, cj, cnt)
    cnt_ref[0] = cnt


def _outproj_call(ya, zc, yc, yd, x2, mod_l, wo, cw, cb, lng, lnb, rw, rb, seq):
    n, d = x2.shape
    tm = TM_PROJ
    tps = seq // tm
    hw = 16
    row = lambda w: pl.BlockSpec((tm, w), lambda i: (i, 0))
    full = lambda a: pl.BlockSpec(a.shape, lambda i: (0,) * a.ndim)
    return pl.pallas_call(
        functools.partial(_outproj_kernel, tps),
        grid=(n // tm,),
        in_specs=[row(GW), row(ZC),
                  pl.BlockSpec((hw, ZC), lambda i: (jnp.maximum(i * (tm // hw) - 1, 0), 0)),
                  row(GW), row(GW), row(d),
                  pl.BlockSpec((1, 6, d), lambda i: (i // tps, 0, 0)),
                  full(wo), full(cw), full(cb), full(lng), full(lnb), full(rw), full(rb)],
        out_specs=[row(d), row(d), pl.BlockSpec((1, 8, tm), lambda i: (i, 0, 0)), row(LANES),
                   pl.BlockSpec((1, N_EXPERTS, LANES), lambda i: (i, 0, 0))],
        out_shape=[jax.ShapeDtypeStruct((n, d), F32), jax.ShapeDtypeStruct((n, d), BF16),
                   jax.ShapeDtypeStruct((n // tm, 8, tm), F32), jax.ShapeDtypeStruct((n, LANES), F32),
                   jax.ShapeDtypeStruct((n // tm, N_EXPERTS, LANES), F32)],
        compiler_params=_cparams(("parallel",)),
        name="out_proj_router",
    )(ya, zc, zc, yc, yd, x2, mod_l, wo, cw, cb, lng, lnb, rw, rb)


def _segment_pieces(count, fn):
    for b in reversed(range(SEG_PIECE_BITS)):
        size = SEG_ALIGN << b
        shift = SEG_ALIGN.bit_length() - 1 + b
        hi = lax.shift_left(lax.shift_right_logical(count, shift + 1), shift + 1)

        @pl.when((lax.shift_right_logical(count, shift) & 1) == 1)
        def _():
            fn(hi, size)


def _slot_positions(onehot_bf16, strict_lower_bf16, transposed):
    if transposed:
        return jnp.dot(strict_lower_bf16, onehot_bf16, preferred_element_type=F32)
    return lax.dot_general(onehot_bf16, strict_lower_bf16, (((1,), (1,)), ((), ())), preferred_element_type=F32)


def _dispatch_kernel(off_s, pc_s, gb_s, gs_s, gl_s, tail_s, u_ref, rt_ref, xs_ref, slab, zbuf, sem):
    blk = pl.program_id(0)
    tb = u_ref.shape[0]
    n_slots = slab.shape[1]
    e1 = rt_ref[0, 0:1, :]
    e2 = rt_ref[0, 1:2, :]
    sub_e = lax.broadcasted_iota(jnp.int32, (N_EXPERTS, tb), 0).astype(F32)
    hit1 = e1 == sub_e
    hit2 = e2 == sub_e
    onehot = jnp.where(hit1 | hit2, 1.0, 0.0).astype(BF16)
    r = lax.broadcasted_iota(jnp.int32, (tb, tb), 0)
    c = lax.broadcasted_iota(jnp.int32, (tb, tb), 1)
    lower = jnp.where(c < r, 1.0, 0.0).astype(BF16)
    rank = _slot_positions(onehot, lower, transposed=False)
    sub1 = lax.broadcasted_iota(jnp.int32, (N_EXPERTS, 1), 0)
    offv = jnp.zeros((N_EXPERTS, 1), F32)
    for e in range(N_EXPERTS):
        offv = jnp.where(sub1 == e, off_s[blk * N_EXPERTS + e].astype(F32), offv)
    pos = offv + rank
    pos1 = jnp.sum(jnp.where(hit1, pos, 0.0), axis=0, keepdims=True)
    pos2 = jnp.sum(jnp.where(hit2, pos, 0.0), axis=0, keepdims=True)
    slot = lax.broadcasted_iota(jnp.int32, (n_slots, tb), 0).astype(F32)
    perm = jnp.where((slot == pos1) | (slot == pos2), 1.0, 0.0).astype(BF16)
    cur = blk % 2
    slab[cur] = jnp.dot(perm, u_ref[...], preferred_element_type=F32)

    @pl.when(blk == 0)
    def _():
        zbuf[...] = jnp.zeros_like(zbuf)

    def copies(b, slot, go):
        for e in range(N_EXPERTS):
            idx = b * N_EXPERTS + e
            src0, dst0 = off_s[idx], gb_s[idx]

            def piece(hi, size, src0=src0, dst0=dst0):
                go(pltpu.make_async_copy(slab.at[slot, pl.ds(pl.multiple_of(src0 + hi, SEG_ALIGN), size)],
                                         xs_ref.at[pl.ds(pl.multiple_of(dst0 + hi, SEG_ALIGN), size)],
                                         sem.at[slot]))
            _segment_pieces(pc_s[idx], piece)

        @pl.when(b == 0)
        def _():
            for e in range(N_EXPERTS):
                dst0 = gs_s[e]

                def piece(hi, size, dst0=dst0):
                    go(pltpu.make_async_copy(zbuf.at[pl.ds(0, size)],
                                             xs_ref.at[pl.ds(pl.multiple_of(dst0 + hi, SEG_ALIGN), size)],
                                             sem.at[slot]))
                _segment_pieces(gl_s[e], piece)

            zrows = zbuf.shape[0]

            def tail_chunk(i, carry):
                go(pltpu.make_async_copy(zbuf, xs_ref.at[pl.ds(pl.multiple_of(tail_s[0] + i * zrows, zrows), zrows)],
                                         sem.at[slot]))
                return carry
            lax.fori_loop(0, tail_s[1], tail_chunk, 0)

    copies(blk, cur, lambda cp: cp.start())

    @pl.when(blk > 0)
    def _():
        copies(blk - 1, 1 - cur, lambda cp: cp.wait())

    @pl.when(blk == pl.num_programs(0) - 1)
    def _():
        copies(blk, cur, lambda cp: cp.wait())


def _dispatch_call(meta, u2, rt, n_rows):
    n, d = u2.shape
    tb = DISPATCH_BLOCK
    per_tile = rt.shape[2] // tb
    grid_spec = pltpu.PrefetchScalarGridSpec(
        num_scalar_prefetch=6,
        grid=(n // tb,),
        in_specs=[pl.BlockSpec((tb, d), lambda i, *_: (i, 0)),
                  pl.BlockSpec((1, 8, tb), lambda i, *_: (i // per_tile, 0, i % per_tile))],
        out_specs=pl.BlockSpec(memory_space=pl.ANY),
        scratch_shapes=[pltpu.VMEM((2, DISPATCH_SLOTS, d), F32),
                        pltpu.VMEM((SEG_ALIGN << (SEG_PIECE_BITS - 1), d), F32),
                        pltpu.SemaphoreType.DMA((2,))],
    )
    return pl.pallas_call(
        _dispatch_kernel,
        grid_spec=grid_spec,
        out_shape=jax.ShapeDtypeStruct((n_rows, d), F32),
        compiler_params=_cparams(("arbitrary",)),
        name="moe_dispatch",
    )(meta["off"], meta["pc"], meta["gbase"], meta["gap_start"], meta["gap_len"], meta["tail"], u2, rt)


def _experts_kernel(te_s, na_s, x_ref, wg_ref, wu_ref, wd_ref, y_ref):
    active = pl.program_id(0) < na_s[0]

    @pl.when(active)
    def _():
        x = x_ref[...].astype(BF16)
        gate = jnp.dot(x, wg_ref[0], preferred_element_type=F32)
        up = jnp.dot(x, wu_ref[0], preferred_element_type=F32)
        hidden = (gate * _sigmoid(gate) * up).astype(BF16)
        y_ref[...] = jnp.dot(hidden, wd_ref[0], preferred_element_type=F32)

    @pl.when(jnp.logical_not(active))
    def _():
        y_ref[...] = jnp.zeros_like(y_ref)


def _experts_call(meta, xs, wg, wu, wd):
    n_rows, d = xs.shape
    tm = TM_MOE
    _, _, de = wg.shape
    tile = lambda i, te, na: jnp.minimum(i, na[0] - 1)
    grid_spec = pltpu.PrefetchScalarGridSpec(
        num_scalar_prefetch=2,
        grid=(n_rows // tm,),
        in_specs=[pl.BlockSpec((tm, d), lambda i, te, na: (tile(i, te, na), 0)),
                  pl.BlockSpec((1, d, de), lambda i, te, na: (te[tile(i, te, na)], 0, 0)),
                  pl.BlockSpec((1, d, de), lambda i, te, na: (te[tile(i, te, na)], 0, 0)),
                  pl.BlockSpec((1, de, d), lambda i, te, na: (te[tile(i, te, na)], 0, 0))],
        out_specs=pl.BlockSpec((tm, d), lambda i, te, na: (i, 0)),
    )
    return pl.pallas_call(
        _experts_kernel,
        grid_spec=grid_spec,
        out_shape=jax.ShapeDtypeStruct((n_rows, d), F32),
        compiler_params=_cparams(("arbitrary",)),
        name="experts",
    )(meta["tile_expert"], meta["n_active"], xs, wg, wu, wd)


def _combine_kernel(off_s, pc_s, gb_s, rtc_ref, x1_ref, mod_ref, lng_ref, lnb_ref, ys_ref, o_ref, slab, sem):
    blk = pl.program_id(0)
    tb = x1_ref.shape[0]
    n_slots = slab.shape[1]
    cur = blk % 2

    def copies(b, slot, go):
        for e in range(N_EXPERTS):
            idx = b * N_EXPERTS + e
            dst0, src0 = off_s[idx], gb_s[idx]

            def piece(hi, size, src0=src0, dst0=dst0):
                go(pltpu.make_async_copy(ys_ref.at[pl.ds(pl.multiple_of(src0 + hi, SEG_ALIGN), size)],
                                         slab.at[slot, pl.ds(pl.multiple_of(dst0 + hi, SEG_ALIGN), size)],
                                         sem.at[slot]))
            _segment_pieces(pc_s[idx], piece)

    @pl.when(blk == 0)
    def _():
        slab[...] = jnp.zeros_like(slab)
        copies(blk, cur, lambda cp: cp.start())

    @pl.when(blk + 1 < pl.num_programs(0))
    def _():
        copies(blk + 1, 1 - cur, lambda cp: cp.start())

    rtc = rtc_ref[...]
    e1, e2, g1, g2 = rtc[:, 0:1], rtc[:, 1:2], rtc[:, 2:3], rtc[:, 3:4]
    lane_e = lax.broadcasted_iota(jnp.int32, (1, LANES), 1).astype(F32)
    hit1 = e1 == lane_e
    hit2 = e2 == lane_e
    onehot = jnp.where(hit1 | hit2, 1.0, 0.0).astype(BF16)
    r = lax.broadcasted_iota(jnp.int32, (tb, tb), 0)
    c = lax.broadcasted_iota(jnp.int32, (tb, tb), 1)
    lower = jnp.where(c < r, 1.0, 0.0).astype(BF16)
    rank = _slot_positions(onehot, lower, transposed=True)
    lane1 = lax.broadcasted_iota(jnp.int32, (1, LANES), 1)
    offv = jnp.zeros((1, LANES), F32)
    for e in range(N_EXPERTS):
        offv = jnp.where(lane1 == e, off_s[blk * N_EXPERTS + e].astype(F32), offv)
    pos = offv + rank
    pos1 = jnp.sum(jnp.where(hit1, pos, 0.0), axis=-1, keepdims=True)
    pos2 = jnp.sum(jnp.where(hit2, pos, 0.0), axis=-1, keepdims=True)
    slot = lax.broadcasted_iota(jnp.int32, (1, n_slots), 1).astype(F32)
    sel1 = jnp.where(slot == pos1, 1.0, 0.0).astype(BF16)
    sel2 = jnp.where(slot == pos2, 1.0, 0.0).astype(BF16)

    copies(blk, cur, lambda cp: cp.wait())
    ys = slab[cur].astype(BF16)
    y = (g1 * jnp.dot(sel1, ys, preferred_element_type=F32) + g2 * jnp.dot(sel2, ys, preferred_element_type=F32))
    mod = mod_ref[0]
    o_ref[...] = _layer_norm(ALPHA * x1_ref[...] + (1.0 + mod[5:6, :]) * y, lng_ref[...], lnb_ref[...])


def _combine_call(meta, ys, rtc, x1, mod_l, lng, lnb, seq):
    n, d = x1.shape
    tb = DISPATCH_BLOCK
    bps = seq // tb
    grid_spec = pltpu.PrefetchScalarGridSpec(
        num_scalar_prefetch=3,
        grid=(n // tb,),
        in_specs=[pl.BlockSpec((tb, LANES), lambda i, *_: (i, 0)),
                  pl.BlockSpec((tb, d), lambda i, *_: (i, 0)),
                  pl.BlockSpec((1, 6, d), lambda i, *_: (i // bps, 0, 0)),
                  pl.BlockSpec((1, d), lambda i, *_: (0, 0)),
                  pl.BlockSpec((1, d), lambda i, *_: (0, 0)),
                  pl.BlockSpec(memory_space=pl.ANY)],
        out_specs=pl.BlockSpec((tb, d), lambda i, *_: (i, 0)),
        scratch_shapes=[pltpu.VMEM((2, DISPATCH_SLOTS, d), F32), pltpu.SemaphoreType.DMA((2,))],
    )
    return pl.pallas_call(
        _combine_kernel,
        grid_spec=grid_spec,
        out_shape=jax.ShapeDtypeStruct((n, d), F32),
        compiler_params=_cparams(("arbitrary",)),
        name="moe_combine_ln2",
    )(meta["off"], meta["pc"], meta["gbase"], rtc, x1, mod_l, lng, lnb, ys)


def _sorted_rows_bound(n_tokens):
    n_blocks = n_tokens // DISPATCH_BLOCK
    worst = 2 * n_tokens + n_blocks * N_EXPERTS * (SEG_ALIGN - 1) + N_EXPERTS * (TM_MOE - 1)
    return -(-worst // TM_MOE) * TM_MOE


def _route_meta(cnt, n_rows):
    pc = (cnt + SEG_ALIGN - 1) // SEG_ALIGN * SEG_ALIGN
    off = jnp.cumsum(pc, axis=1) - pc
    tot = pc.sum(axis=0)
    reg = (tot + TM_MOE - 1) // TM_MOE * TM_MOE
    reg_end = jnp.cumsum(reg)
    ebase = reg_end - reg
    gbase = ebase[None, :] + jnp.cumsum(pc, axis=0) - pc
    tiles = jnp.arange(n_rows // TM_MOE, dtype=jnp.int32)
    tile_expert = jnp.minimum(jnp.sum(tiles[:, None] >= (reg_end // TM_MOE)[None, :], axis=1), N_EXPERTS - 1)
    i32 = lambda a: a.astype(jnp.int32).reshape(-1)
    used = reg_end[-1:]
    tail_chunk = SEG_ALIGN << (SEG_PIECE_BITS - 1)
    return {"off": i32(off), "pc": i32(pc), "gbase": i32(gbase), "gap_start": i32(ebase + tot),
            "gap_len": i32(reg - tot), "tail": i32(jnp.concatenate([used, (n_rows - used) // tail_chunk])),
            "tile_expert": i32(tile_expert), "n_active": i32(used // TM_MOE)}


def _rearranged_w_in(w_in_l):
    d = w_in_l.shape[0]
    zeros = lambda w: jnp.zeros((d, w), w_in_l.dtype)
    o_gate, o_cb, o_dq, o_dk, o_dv, o_g, o_ga = 1024, 1032, 1800, 2056, 2312, 2568, 3592
    v_cols = []
    for h in range(HEADS):
        v_cols += [w_in_l[:, o_dv + h * HEAD_DIM:o_dv + (h + 1) * HEAD_DIM], zeros(LANES - HEAD_DIM)]
    small = [w_in_l[:, o_gate:o_gate + 2 * HEADS], w_in_l[:, o_ga:o_ga + GLA_RANK],
             zeros(ZS - 2 * HEADS - GLA_RANK)]
    cols = [w_in_l[:, 0:ZM], w_in_l[:, o_cb:o_cb + ZC], w_in_l[:, o_dq:o_dq + ZQ], w_in_l[:, o_dk:o_dk + ZK]]
    cols += v_cols + [w_in_l[:, o_g:o_g + ZG]] + small
    return jnp.concatenate(cols, axis=1).astype(BF16)


def _rope_tables(positions):
    inv_freq = ROPE_THETA ** (-jnp.arange(0, ROPE_DIM, 2, dtype=F32) / ROPE_DIM)
    ang = positions.astype(F32).reshape(-1, 1) * inv_freq
    cos, sin = jnp.cos(ang), jnp.sin(ang)
    d = np.arange(ZQ) % DIFF_DIM
    half = ROPE_DIM // 2
    idx = d % half
    cosf = jnp.where(d < ROPE_DIM, cos[:, idx], 1.0)
    sinf = jnp.where(d < half, -sin[:, idx], jnp.where(d < ROPE_DIM, sin[:, idx], 0.0))
    return cosf, sinf


def _pad_lanes(v, width=LANES):
    v = v.reshape(1, -1)
    return jnp.pad(v, ((0, 0), (0, width - v.shape[1])))


def kernel(x, c, positions, w_ada, b_ada, w_in, w_out, mlstm_b_i, mlstm_b_f, mlstm_norm_g, conv_w, conv_b, diff_lambda_q1, diff_lambda_k1, diff_lambda_q2, diff_lambda_k2, diff_norm_g, gla_w_a2, gla_b_a, gla_norm_g, ln1_g, ln1_b, ln2_g, ln2_b, router_w, router_bias, exp_w_gate, exp_w_up, exp_w_down):
    bsz, seq, d = x.shape
    depth = w_in.shape[0]
    n = bsz * seq
    x2 = x.reshape(n, d)
    mod = _mod_call(c, w_ada, b_ada)
    cosf, sinf = _rope_tables(positions)
    rw_t = router_w.T.astype(BF16)
    rb = router_bias.reshape(N_EXPERTS, 1).astype(F32)
    n_sorted = _sorted_rows_bound(n)
    for l in range(depth):
        lam_init = 0.8 - 0.6 * math.exp(-0.3 * l)
        w_l = _rearranged_w_in(w_in[l])
        zm, zc, zq, zk, zvt, zg, zs = _inproj_call(x2, mod[l], w_l, cosf, sinf, seq)
        bif = _pad_lanes(jnp.concatenate([mlstm_b_i[l], mlstm_b_f[l]]))
        ya = _mlstm_call(zm, zs, bif, mlstm_norm_g[l].reshape(1, -1), bsz, seq)
        lamp = jnp.concatenate([_pad_lanes(diff_lambda_q1[l]), _pad_lanes(diff_lambda_k1[l]),
                                _pad_lanes(diff_lambda_q2[l]), _pad_lanes(diff_lambda_k2[l])], axis=0)
        yc = _attn_call(zq, zk, zvt, lamp, diff_norm_g[l].reshape(1, -1), lam_init, bsz, seq)
        wa = jnp.pad(gla_w_a2[l], ((2 * HEADS, ZS - 2 * HEADS - GLA_RANK), (0, 0))).astype(BF16)
        yd = _gla_call(zg, zs, wa, gla_b_a[l].reshape(1, -1), gla_norm_g[l].reshape(1, -1), bsz, seq)
        x1, u2, rt, rtc, cnt = _outproj_call(ya, zc, yc, yd, x2, mod[l], w_out[l].astype(BF16), conv_w[l],
                                             conv_b[l].reshape(1, -1), ln1_g[l].reshape(1, -1),
                                             ln1_b[l].reshape(1, -1), rw_t, rb, seq)
        per_tile = TM_PROJ // DISPATCH_BLOCK
        cnt_blk = cnt[:, :, :per_tile].transpose(0, 2, 1).reshape(-1, N_EXPERTS).astype(jnp.int32)
        meta = _route_meta(cnt_blk, n_sorted)
        xs = _dispatch_call(meta, u2, rt, n_sorted)
        ys = _experts_call(meta, xs, exp_w_gate[l].astype(BF16), exp_w_up[l].astype(BF16),
                           exp_w_down[l].astype(BF16))
        x2 = _combine_call(meta, ys, rtc, x1, mod[l], ln2_g[l].reshape(1, -1), ln2_b[l].reshape(1, -1), seq)
    return x2.reshape(bsz, seq, d)
```

```python
import functools
import math

import numpy as np
import jax
import jax.numpy as jnp
from jax import lax
from jax.experimental import pallas as pl
from jax.experimental.pallas import tpu as pltpu

F32 = jnp.float32
BF16 = jnp.bfloat16

D_MODEL = 1024
DEPTH = 4
HEADS = 4
GW = 256
HEAD_DIM = 64
DIFF_DIM = 32
ROPE_DIM = 8
ROPE_THETA = 500000.0
CONV_WIDTH = 3
GLA_RANK = 16
GLA_TAU = 16.0
CHUNK = 64
GLA_CHUNK = 64
N_EXPERTS = 16
N_GROUPS = 4
GROUP_SIZE = 4
D_EXPERT = 512
ALPHA = (2 * DEPTH) ** 0.25
EPS = 1e-6
LANES = 128
SUBLANES = 8
NEG_INF = float("-inf")
LOG2_E = math.log2(math.e)

ZM, ZC, ZQ, ZK, ZVP, ZG, ZS = 1024, 768, 256, 256, 256, 1024, 128
ZV = HEADS * LANES
Z_OFFS = np.cumsum([0, ZM, ZC, ZQ, ZK, ZVP, ZG, ZS])
Z_TOTAL = int(Z_OFFS[-1])

TM_PROJ = 512
TB_MLSTM = 256
TB_GLA = 512
CUM_SPAN = 128
TQ_ATTN = 512
TM_MOE = 512
DISPATCH_BLOCK = 256
SEG_ALIGN = 8
SEG_PIECE_BITS = 6
DISPATCH_SLOTS = 2 * DISPATCH_BLOCK + N_EXPERTS * SEG_ALIGN
assert SEG_ALIGN << (SEG_PIECE_BITS - 1) >= DISPATCH_BLOCK and SEG_ALIGN << SEG_PIECE_BITS >= TM_MOE
VMEM_LIMIT = 56 * 1024 * 1024


def _cparams(sem):
    return pltpu.CompilerParams(dimension_semantics=sem, vmem_limit_bytes=VMEM_LIMIT)


def _log_sigmoid(x):
    return jnp.minimum(x, 0.0) - jnp.log(1.0 + jnp.exp(-jnp.abs(x)))


def _sigmoid(x):
    return 1.0 / (1.0 + jnp.exp(-x))


def _layer_norm(v, g, b):
    mu = jnp.mean(v, axis=-1, keepdims=True)
    d = v - mu
    var = jnp.mean(d * d, axis=-1, keepdims=True)
    return d * lax.rsqrt(var + EPS) * g + b


def _mod_kernel(c_ref, w_ref, b_ref, o_ref):
    c = c_ref[...]
    c_act = (c * _sigmoid(c)).astype(BF16)
    o_ref[0, 0] = jnp.dot(c_act, w_ref[0].astype(BF16), preferred_element_type=F32) + b_ref[0, 0]


def _mod_call(c, w_ada, b_ada):
    depth, d, _ = w_ada.shape
    bsz = c.shape[0]
    b3 = b_ada.reshape(depth, 6, 1, d)
    out = pl.pallas_call(
        _mod_kernel,
        grid=(depth, 6),
        in_specs=[pl.BlockSpec((bsz, d), lambda l, j: (0, 0)),
                  pl.BlockSpec((1, d, d), lambda l, j: (l, 0, j)),
                  pl.BlockSpec((1, 1, 1, d), lambda l, j: (l, j, 0, 0))],
        out_specs=pl.BlockSpec((1, 1, bsz, d), lambda l, j: (l, j, 0, 0)),
        out_shape=jax.ShapeDtypeStruct((depth, 6, bsz, d), F32),
        compiler_params=_cparams(("arbitrary", "arbitrary")),
        name="adaln_mod",
    )(c, w_ada, b3)
    return out.transpose(0, 2, 1, 3)


def _inproj_kernel(x_ref, mod_ref, w_ref, cos_ref, sin_ref, zm_ref, zc_ref, zq_ref, zk_ref, zvt_ref, zg_ref,
                   zs_ref):
    mod = mod_ref[0]
    u = (x_ref[...] * (1.0 + mod[1:2, :]) + mod[0:1, :]).astype(BF16)

    def proj(idx):
        return jnp.dot(u, w_ref[:, int(Z_OFFS[idx]):int(Z_OFFS[idx + 1])], preferred_element_type=F32)

    zm_ref[...] = proj(0).astype(BF16)
    zc_ref[...] = proj(1).astype(BF16)

    lane = lax.broadcasted_iota(jnp.int32, (1, ZQ), 1)
    first_half = (lane % ROPE_DIM) < (ROPE_DIM // 2)
    cosf = cos_ref[...]
    sinf = sin_ref[...]

    def rope(r):
        swapped = jnp.where(first_half, pltpu.roll(r, ZQ - ROPE_DIM // 2, 1), pltpu.roll(r, ROPE_DIM // 2, 1))
        return r * cosf + swapped * sinf

    zq_ref[...] = (rope(proj(2)) * (LOG2_E * DIFF_DIM ** -0.5)).astype(BF16)
    zk_ref[...] = rope(proj(3)).astype(BF16)
    vt = proj(4).T
    tm = vt.shape[1]
    ones_rows = jnp.where(lax.broadcasted_iota(jnp.int32, (LANES - HEAD_DIM, tm), 0) == 0, 1.0, 0.0)
    zvt_ref[...] = jnp.concatenate([blk for h in range(HEADS)
                                    for blk in (vt[h * HEAD_DIM:(h + 1) * HEAD_DIM], ones_rows)], axis=0).astype(BF16)
    zg_ref[...] = proj(5).astype(BF16)
    zs_ref[...] = proj(6)


def _inproj_call(x2, mod_l, w_l, cosf, sinf, seq):
    n, d = x2.shape
    tm = TM_PROJ
    tps = seq // tm
    row = lambda w: pl.BlockSpec((tm, w), lambda i: (i, 0))
    outs = [(ZM, BF16), (ZC, BF16), (ZQ, BF16), (ZK, BF16), (ZV, BF16), (ZG, BF16), (ZS, F32)]
    return pl.pallas_call(
        _inproj_kernel,
        grid=(n // tm,),
        in_specs=[row(d),
                  pl.BlockSpec((1, 6, d), lambda i: (i // tps, 0, 0)),
                  pl.BlockSpec((d, Z_TOTAL), lambda i: (0, 0)),
                  row(ZQ), row(ZQ)],
        out_specs=[pl.BlockSpec((ZV, tm), lambda i: (0, i)) if k == 4 else row(w) for k, (w, _) in enumerate(outs)],
        out_shape=[jax.ShapeDtypeStruct((ZV, n) if k == 4 else (n, w), dt) for k, (w, dt) in enumerate(outs)],
        compiler_params=_cparams(("parallel",)),
        name="in_proj",
    )(x2, mod_l, w_l, cosf, sinf)


NT_DIMS = (((1,), (1,)), ((), ()))
TN_DIMS = (((0,), (0,)), ((), ()))


def _head_of(idx):
    return idx // HEAD_DIM


def _cumsum_matrix(span, chunk):
    r = lax.broadcasted_iota(jnp.int32, (span, span), 0)
    c = lax.broadcasted_iota(jnp.int32, (span, span), 1)
    return jnp.where(((r // chunk) == (c // chunk)) & (c <= r), 1.0, 0.0).astype(BF16)


def _bf16_terms(x, n_terms):
    terms, rest = [], x
    for _ in range(n_terms):
        t = rest.astype(BF16)
        terms.append(t)
        rest = rest - t.astype(F32)
    return terms


def _exact_matmul(lhs_f32, rhs01, n_terms=3):
    return functools.reduce(lambda a, b: a + b, [jnp.dot(t, rhs01, preferred_element_type=F32)
                                                 for t in reversed(_bf16_terms(lhs_f32, n_terms))])


def _chunk_cumsum(x, chunk):
    cm = _cumsum_matrix(CUM_SPAN, chunk)
    outs = []
    for i in range(x.shape[0] // CUM_SPAN):
        terms = _bf16_terms(x[i * CUM_SPAN:(i + 1) * CUM_SPAN], 3)
        outs.append(functools.reduce(lambda a, b: a + b,
                                     [jnp.dot(cm, t, preferred_element_type=F32) for t in reversed(terms)]))
    return jnp.concatenate(outs, axis=0)


def _rows_broadcast(x, chunk, row):
    n = x.shape[0] // chunk
    return jnp.concatenate([jnp.broadcast_to(x[c * chunk + row:c * chunk + row + 1, :], (chunk, x.shape[1]))
                            for c in range(n)], axis=0)


def _head_masks(chunk):
    lane_head = _head_of(lax.broadcasted_iota(jnp.int32, (1, GW), 1))
    row_head = lax.broadcasted_iota(jnp.int32, (HEADS * chunk, 1), 0) // chunk
    stack_keep = row_head == lane_head
    rr = lax.broadcasted_iota(jnp.int32, (HEADS * chunk, chunk), 0) % chunk
    cc = lax.broadcasted_iota(jnp.int32, (HEADS * chunk, chunk), 1)
    block_diag = (_head_of(lax.broadcasted_iota(jnp.int32, (GW, GW), 0))
                  == _head_of(lax.broadcasted_iota(jnp.int32, (GW, GW), 1)))
    return stack_keep, cc <= rr, block_diag


def _unstack_heads(x4, keep, chunk):
    x4 = jnp.where(keep, x4, 0.0)
    return functools.reduce(lambda a, b: a + b, [x4[h * chunk:(h + 1) * chunk] for h in range(HEADS)])


def _head_rms_norm(x, block_diag):
    seg = jnp.where(block_diag, 1.0, 0.0).astype(BF16)
    return x * lax.rsqrt(_exact_matmul(x * x, seg, 2) * (1.0 / HEAD_DIM) + EPS)


def _mlstm_kernel(zm_ref, zs_ref, bif_ref, g_ref, o_ref, ct_scr, nm_scr, ni_scr, up_scr):
    @pl.when(pl.program_id(1) == 0)
    def _():
        ct_scr[...] = jnp.zeros_like(ct_scr)
        nm_scr[...] = jnp.zeros_like(nm_scr)

    tb = zm_ref.shape[0]
    L = CHUNK
    nc = tb // L
    lane = lax.broadcasted_iota(jnp.int32, (1, LANES), 1)
    graw = zs_ref[...] + bif_ref[...]
    gp = jnp.where(lane < HEADS, graw, jnp.where(lane < 2 * HEADS, _log_sigmoid(graw), 0.0))
    cum = _chunk_cumsum(gp, L)

    def expand(x, first_lane):
        sel = jnp.where((lax.broadcasted_iota(jnp.int32, (LANES, GW), 0) - first_lane)
                        == _head_of(lax.broadcasted_iota(jnp.int32, (LANES, GW), 1)), 1.0, 0.0).astype(BF16)
        return _exact_matmul(x, sel)

    bx = expand(cum, HEADS)
    lix = expand(gp, 0)
    bex = _rows_broadcast(bx, L, L - 1)
    dkx = bex - bx + lix
    dkmax = [jnp.max(dkx[c * L:(c + 1) * L], axis=0, keepdims=True) for c in range(nc)]
    kw = (zm_ref[:, GW:2 * GW].astype(F32)
          * jnp.exp(dkx - jnp.concatenate([jnp.broadcast_to(d, (L, GW)) for d in dkmax], axis=0))).astype(BF16)
    rowv_t = (cum - pltpu.roll(gp, HEADS, 1)).T
    stack_keep, causal4, block_diag = _head_masks(L)
    stack_keep2 = jnp.concatenate([stack_keep, stack_keep], axis=1)
    ones_blk = jnp.ones((L, GW), BF16)

    dmaxx = []
    for c in range(nc):
        rs = slice(c * L, (c + 1) * L)
        qc = zm_ref[rs, 0:GW] * (HEAD_DIM ** -0.5)
        kc = zm_ref[rs, GW:2 * GW]
        vc = zm_ref[rs, 2 * GW:3 * GW]
        q4 = jnp.where(stack_keep, jnp.concatenate([qc] * HEADS, axis=0), jnp.zeros((), BF16))
        s4 = lax.dot_general(q4, kc, NT_DIMS, preferred_element_type=F32)
        bcol4 = jnp.concatenate([cum[rs, HEADS + h:HEADS + h + 1] for h in range(HEADS)], axis=0)
        brow4 = jnp.concatenate([jnp.broadcast_to(rowv_t[HEADS + h:HEADS + h + 1, rs], (L, L))
                                 for h in range(HEADS)], axis=0)
        d4 = jnp.where(causal4, bcol4 - brow4, NEG_INF)
        dmax4 = jnp.max(d4, axis=-1, keepdims=True)
        p4 = (s4 * jnp.exp(d4 - dmax4)).astype(BF16)
        i4 = jnp.dot(p4, jnp.concatenate([vc, ones_blk], axis=1), preferred_element_type=F32)
        ni_scr[rs, :] = _unstack_heads(i4, stack_keep2, L)
        dmaxx.append(_unstack_heads(jnp.broadcast_to(dmax4, (HEADS * L, GW)), stack_keep, L))
        up = lax.dot_general(vc, kw[rs], TN_DIMS, preferred_element_type=F32)
        up_scr[c] = jnp.where(block_diag, up, 0.0)

    ct = ct_scr[...]
    n_row = nm_scr[0:1, :]
    m_prev = nm_scr[1:2, :]
    hv_rows = []
    for c in range(nc):
        rs = slice(c * L, (c + 1) * L)
        qc = zm_ref[rs, 0:GW] * (HEAD_DIM ** -0.5)
        nmat = jnp.where(block_diag, jnp.broadcast_to(n_row, (GW, GW)), 0.0).astype(BF16)
        num_inter = lax.dot_general(qc, ct.astype(BF16), NT_DIMS, preferred_element_type=F32)
        den_inter = lax.dot_general(qc, nmat, NT_DIMS, preferred_element_type=F32)
        inter = bx[rs] + m_prev
        m_row = jnp.maximum(inter, dmaxx[c])
        e1 = jnp.exp(dmaxx[c] - m_row)
        e2 = jnp.exp(inter - m_row)
        num = e1 * ni_scr[rs, 0:GW] + e2 * num_inter
        den = e1 * ni_scr[rs, GW:2 * GW] + e2 * den_inter
        hv_rows.append(num / jnp.maximum(jnp.abs(den), jnp.exp(-m_row)))
        b_end = bex[c * L:c * L + 1, :]
        m_new = jnp.maximum(b_end + m_prev, dkmax[c])
        decay = jnp.exp(b_end + m_prev - m_new)
        f2 = jnp.exp(dkmax[c] - m_new)
        ct = ct * decay + up_scr[c] * f2
        n_row = n_row * decay + f2 * jnp.sum(kw[rs].astype(F32), axis=0, keepdims=True)
        m_prev = m_new
    ct_scr[...] = ct
    nm_scr[0:1, :] = n_row
    nm_scr[1:2, :] = m_prev

    hn = _head_rms_norm(jnp.concatenate(hv_rows, axis=0), block_diag)
    o_ref[...] = (hn * g_ref[...] * _sigmoid(zm_ref[:, 3 * GW:4 * GW].astype(F32))).astype(BF16)


def _mlstm_call(zm, zs, bif, gain, bsz, seq):
    tb = TB_MLSTM
    tps = seq // tb
    return pl.pallas_call(
        _mlstm_kernel,
        grid=(bsz, tps),
        in_specs=[pl.BlockSpec((tb, ZM), lambda b, t: (b * tps + t, 0)),
                  pl.BlockSpec((tb, ZS), lambda b, t: (b * tps + t, 0)),
                  pl.BlockSpec((1, LANES), lambda b, t: (0, 0)),
                  pl.BlockSpec((1, GW), lambda b, t: (0, 0))],
        out_specs=pl.BlockSpec((tb, GW), lambda b, t: (b * tps + t, 0)),
        out_shape=jax.ShapeDtypeStruct((bsz * seq, GW), BF16),
        scratch_shapes=[pltpu.VMEM((GW, GW), F32),
                        pltpu.VMEM((8, GW), F32),
                        pltpu.VMEM((tb, 2 * GW), F32),
                        pltpu.VMEM((tb // CHUNK, GW, GW), F32)],
        compiler_params=_cparams(("parallel", "arbitrary")),
        name="mlstm",
    )(zm, zs, bif, jnp.tile(gain, (1, HEADS)))


def _gla_kernel(zg_ref, zs_ref, wa_ref, ba_ref, g_ref, o_ref, st_scr, oi_scr, up_scr):
    @pl.when(pl.program_id(1) == 0)
    def _():
        st_scr[...] = jnp.zeros_like(st_scr)

    tb = zg_ref.shape[0]
    lg = GLA_CHUNK
    nc = tb // lg
    la = _log_sigmoid(jnp.dot(zs_ref[...].astype(BF16), wa_ref[...], preferred_element_type=F32)
                      + ba_ref[...]) * (1.0 / GLA_TAU)
    bc = _chunk_cumsum(la, lg)
    be = _rows_broadcast(bc, lg, lg - 1)
    bm = _rows_broadcast(bc, lg, lg // 2)
    q = zg_ref[:, 0:GW].astype(F32) * (HEAD_DIM ** -0.5)
    k = zg_ref[:, GW:2 * GW].astype(F32)
    qm = (q * jnp.exp(bc - bm)).astype(BF16)
    km = (k * jnp.exp(bm - bc)).astype(BF16)
    qt = (q * jnp.exp(bc)).astype(BF16)
    ks = (k * jnp.exp(be - bc)).astype(BF16)
    dec = jnp.exp(be)
    stack_keep, causal4, block_diag = _head_masks(lg)

    for c in range(nc):
        rs = slice(c * lg, (c + 1) * lg)
        vc = zg_ref[rs, 2 * GW:3 * GW]
        q4 = jnp.where(stack_keep, jnp.concatenate([qm[rs]] * HEADS, axis=0), jnp.zeros((), BF16))
        att4 = lax.dot_general(q4, km[rs], NT_DIMS, preferred_element_type=F32)
        att4 = jnp.where(causal4, att4, 0.0).astype(BF16)
        oi_scr[rs, :] = _unstack_heads(jnp.dot(att4, vc, preferred_element_type=F32), stack_keep, lg)
        up = lax.dot_general(vc, ks[rs], TN_DIMS, preferred_element_type=F32)
        up_scr[c] = jnp.where(block_diag, up, 0.0)

    st = st_scr[...]
    inter = []
    for c in range(nc):
        rs = slice(c * lg, (c + 1) * lg)
        inter.append(lax.dot_general(qt[rs], st.astype(BF16), NT_DIMS, preferred_element_type=F32))
        st = st * dec[c * lg:c * lg + 1, :] + up_scr[c]
    st_scr[...] = st

    on = _head_rms_norm(oi_scr[...] + jnp.concatenate(inter, axis=0), block_diag)
    r = zg_ref[:, 3 * GW:4 * GW].astype(F32)
    o_ref[...] = (on * g_ref[...] * (r * _sigmoid(r))).astype(BF16)


def _gla_call(zg, zs, wa, ba, gain, bsz, seq):
    tb = TB_GLA
    tps = seq // tb
    return pl.pallas_call(
        _gla_kernel,
        grid=(bsz, tps),
        in_specs=[pl.BlockSpec((tb, ZG), lambda b, t: (b * tps + t, 0)),
                  pl.BlockSpec((tb, ZS), lambda b, t: (b * tps + t, 0)),
                  pl.BlockSpec((ZS, GW), lambda b, t: (0, 0)),
                  pl.BlockSpec((1, GW), lambda b, t: (0, 0)),
                  pl.BlockSpec((1, GW), lambda b, t: (0, 0))],
        out_specs=pl.BlockSpec((tb, GW), lambda b, t: (b * tps + t, 0)),
        out_shape=jax.ShapeDtypeStruct((bsz * seq, GW), BF16),
        scratch_shapes=[pltpu.VMEM((GW, GW), F32),
                        pltpu.VMEM((tb, GW), F32),
                        pltpu.VMEM((tb // GLA_CHUNK, GW, GW), F32)],
        compiler_params=_cparams(("parallel", "arbitrary")),
        name="gla",
    )(zg, zs, wa, ba, jnp.tile(gain, (1, HEADS)))


def _attn_kernel(lam_init, q_ref, k_ref, vt_ref, lamp_ref, g_ref, o_ref, qm_scr, m_scr, acc_scr, st_scr):
    tq = q_ref.shape[0]
    qi = pl.program_id(1)
    n_hc = 2 * HEADS
    lane = lax.broadcasted_iota(jnp.int32, (1, LANES), 1)
    for hc in range(n_hc):
        h, c = divmod(hc, 2)
        pair = h // 2
        lo = (h % 2) * HEAD_DIM + c * DIFF_DIM
        qp = q_ref[:, pair * LANES:(pair + 1) * LANES]
        qm_scr[hc] = jnp.where((lane >= lo) & (lane < lo + DIFF_DIM), qp, jnp.zeros_like(qp))
    m_scr[...] = jnp.full(m_scr.shape, NEG_INF, F32)
    acc_scr[...] = jnp.zeros_like(acc_scr)

    def sublane_allmax(x):
        for shift in (4, 2, 1):
            x = jnp.maximum(x, pltpu.roll(x, shift, 0))
        return x

    def kv_block(j, keep_t):
        key0 = pl.multiple_of(j * tq, tq)
        for hc in range(n_hc):
            pair = (hc // 2) // 2
            kb = k_ref[pl.ds(key0, tq), pair * LANES:(pair + 1) * LANES]
            st_scr[hc] = lax.dot_general(kb, qm_scr[hc], NT_DIMS, preferred_element_type=F32)
        for hc in range(n_hc):
            h = hc // 2
            vt = vt_ref[h * LANES:(h + 1) * LANES, pl.ds(key0, tq)]
            st = st_scr[hc]
            if keep_t is not None:
                st = jnp.where(keep_t, st, NEG_INF)
            st3 = st.reshape(tq // SUBLANES, SUBLANES, tq)
            m_old = m_scr[hc]
            m_new = jnp.maximum(m_old, sublane_allmax(jnp.max(st3, axis=0)))
            alpha = jnp.exp2(m_old - m_new)
            p = jnp.exp2(st3 - m_new[None]).reshape(tq, tq).astype(BF16)
            acc = acc_scr[hc].reshape(LANES // SUBLANES, SUBLANES, tq) * alpha[None]
            acc_scr[hc] = acc.reshape(LANES, tq) + jnp.dot(vt, p, preferred_element_type=F32)
            m_scr[hc] = m_new

    def full_block(j, carry):
        kv_block(j, None)
        return carry

    lax.fori_loop(0, qi, full_block, 0)
    key_idx = lax.broadcasted_iota(jnp.int32, (tq, tq), 0)
    query_idx = lax.broadcasted_iota(jnp.int32, (tq, tq), 1)
    kv_block(qi, key_idx <= query_idx)

    lamp = lamp_ref[...]
    lam = (jnp.exp(jnp.sum(lamp[0:1, :] * lamp[1:2, :], axis=-1, keepdims=True))
           - jnp.exp(jnp.sum(lamp[2:3, :] * lamp[3:4, :], axis=-1, keepdims=True)) + lam_init)
    gain = g_ref[...] * (1.0 - lam_init)
    outs = []
    for h in range(HEADS):
        a1 = acc_scr[2 * h]
        a2 = acc_scr[2 * h + 1]
        o = a1[:HEAD_DIM] / a1[HEAD_DIM:HEAD_DIM + 1] - lam * (a2[:HEAD_DIM] / a2[HEAD_DIM:HEAD_DIM + 1])
        on = o * lax.rsqrt(jnp.mean(o * o, axis=0, keepdims=True) + EPS)
        outs.append(on.T * gain)
    o_ref[...] = jnp.concatenate(outs, axis=1).astype(BF16)


def _attn_call(zq, zk, zvt, lamp, gain, lam_init, bsz, seq):
    tq = TQ_ATTN
    nq = seq // tq
    return pl.pallas_call(
        functools.partial(_attn_kernel, lam_init),
        grid=(bsz, nq),
        in_specs=[pl.BlockSpec((tq, ZQ), lambda b, i: (b * nq + i, 0)),
                  pl.BlockSpec((seq, ZK), lambda b, i: (b, 0)),
                  pl.BlockSpec((ZV, seq), lambda b, i: (0, b)),
                  pl.BlockSpec((4, LANES), lambda b, i: (0, 0)),
                  pl.BlockSpec((1, HEAD_DIM), lambda b, i: (0, 0))],
        out_specs=pl.BlockSpec((tq, GW), lambda b, i: (b * nq + i, 0)),
        out_shape=jax.ShapeDtypeStruct((bsz * seq, GW), BF16),
        scratch_shapes=[pltpu.VMEM((2 * HEADS, tq, LANES), BF16),
                        pltpu.VMEM((2 * HEADS, SUBLANES, tq), F32),
                        pltpu.VMEM((2 * HEADS, LANES, tq), F32),
                        pltpu.VMEM((2 * HEADS, tq, tq), F32)],
        compiler_params=_cparams(("parallel", "arbitrary")),
        name="diff_attn",
    )(zq, zk, zvt, lamp, gain)


def _outproj_kernel(tiles_per_seq, ya_ref, zc_ref, halo_ref, yc_ref, yd_ref, x_ref, mod_ref, wo_ref, cw_ref,
                    cb_ref, lng_ref, lnb_ref, rw_ref, rb_ref, x1_ref, u2_ref, rt_ref, rtc_ref, cnt_ref):
    tm = x_ref.shape[0]
    mod = mod_ref[0]
    zc = zc_ref[...].astype(F32)
    halo = halo_ref[...].astype(F32)
    first = (pl.program_id(0) % tiles_per_seq) == 0
    s_cur = zc[:, GW:2 * GW] * zc[:, 2 * GW:3 * GW]
    s_halo = jnp.where(first, 0.0, halo[:, GW:2 * GW] * halo[:, 2 * GW:3 * GW])
    s_ext = jnp.concatenate([s_halo, s_cur], axis=0)
    hw = halo.shape[0]
    cw = cw_ref[...]
    conv = (cw[0:1, :] * s_ext[hw - 2:hw - 2 + tm, :] + cw[1:2, :] * s_ext[hw - 1:hw - 1 + tm, :]
            + cw[2:3, :] * s_cur + cb_ref[...])
    yb = (zc[:, 0:GW] * conv).astype(BF16)

    y = (jnp.dot(ya_ref[...], wo_ref[0:GW, :], preferred_element_type=F32)
         + jnp.dot(yb, wo_ref[GW:2 * GW, :], preferred_element_type=F32)
         + jnp.dot(yc_ref[...], wo_ref[2 * GW:3 * GW, :], preferred_element_type=F32)
         + jnp.dot(yd_ref[...], wo_ref[3 * GW:4 * GW, :], preferred_element_type=F32))
    x1 = _layer_norm(ALPHA * x_ref[...] + (1.0 + mod[2:3, :]) * y, lng_ref[...], lnb_ref[...])
    x1_ref[...] = x1
    u2 = (x1 * (1.0 + mod[4:5, :]) + mod[3:4, :]).astype(BF16)
    u2_ref[...] = u2

    logits = lax.dot_general(rw_ref[...], u2, (((1,), (1,)), ((), ())), preferred_element_type=F32)
    scores = _sigmoid(logits)
    biased = scores + rb_ref[...]
    rows = [biased[e:e + 1, :] for e in range(N_EXPERTS)]
    srow = [scores[e:e + 1, :] for e in range(N_EXPERTS)]
    gscore = []
    for g in range(N_GROUPS):
        a = rows[g * GROUP_SIZE:(g + 1) * GROUP_SIZE]
        best = None
        for i in range(GROUP_SIZE):
            for j in range(i + 1, GROUP_SIZE):
                pair_sum = a[i] + a[j]
                best = pair_sum if best is None else jnp.maximum(best, pair_sum)
        gscore.append(best)
    gmax = functools.reduce(jnp.maximum, gscore)
    gsel = jnp.full(gmax.shape, N_GROUPS, jnp.int32)
    for g in reversed(range(N_GROUPS)):
        gsel = jnp.where(gscore[g] == gmax, g, gsel)
    masked = [jnp.where(gsel == (e // GROUP_SIZE), rows[e], NEG_INF) for e in range(N_EXPERTS)]
    m1 = functools.reduce(jnp.maximum, masked)
    i1 = jnp.full(m1.shape, N_EXPERTS, jnp.int32)
    for e in reversed(range(N_EXPERTS)):
        i1 = jnp.where(masked[e] == m1, e, i1)
    masked2 = [jnp.where(i1 == e, NEG_INF, masked[e]) for e in range(N_EXPERTS)]
    m2 = functools.reduce(jnp.maximum, masked2)
    i2 = jnp.full(m2.shape, N_EXPERTS, jnp.int32)
    for e in reversed(range(N_EXPERTS)):
        i2 = jnp.where((masked2[e] == m2) & (i1 != e), e, i2)
    sel1 = functools.reduce(lambda acc, e: acc + jnp.where(i1 == e, srow[e], 0.0), range(N_EXPERTS), 0.0)
    sel2 = functools.reduce(lambda acc, e: acc + jnp.where(i2 == e, srow[e], 0.0), range(N_EXPERTS), 0.0)
    tot = sel1 + sel2
    g1 = sel1 / tot
    g2 = sel2 / tot
    route = jnp.concatenate([i1.astype(F32), i2.astype(F32), g1, g2, jnp.zeros((4, tm), F32)], axis=0)
    rt_ref[0] = route
    rtc_ref[...] = jnp.concatenate([route, jnp.zeros((LANES - 8, tm), F32)], axis=0).T
    sub = lax.broadcasted_iota(jnp.int32, (N_EXPERTS, tm), 0)
    onehot = jnp.where((i1 == sub) | (i2 == sub), 1.0, 0.0)
    lane = lax.broadcasted_iota(jnp.int32, (1, LANES), 1)
    cnt = jnp.zeros((N_EXPERTS, LANES), F32)
    for j in range(tm // DISPATCH_BLOCK):
        cj = jnp.sum(onehot[:, j * DISPATCH_BLOCK:(j + 1) * DISPATCH_BLOCK], axis=-1, keepdims=True)
        cnt = jnp.where(lane == j, cj, cnt)
    cnt_ref[0] = cnt


def _outproj_call(ya, zc, yc, yd, x2, mod_l, wo, cw, cb, lng, lnb, rw, rb, seq):
    n, d = x2.shape
    tm = TM_PROJ
    tps = seq // tm
    hw = 16
    row = lambda w: pl.BlockSpec((tm, w), lambda i: (i, 0))
    full = lambda a: pl.BlockSpec(a.shape, lambda i: (0,) * a.ndim)
    return pl.pallas_call(
        functools.partial(_outproj_kernel, tps),
        grid=(n // tm,),
        in_specs=[row(GW), row(ZC),
                  pl.BlockSpec((hw, ZC), lambda i: (jnp.maximum(i * (tm // hw) - 1, 0), 0)),
                  row(GW), row(GW), row(d),
                  pl.BlockSpec((1, 6, d), lambda i: (i // tps, 0, 0)),
                  full(wo), full(cw), full(cb), full(lng), full(lnb), full(rw), full(rb)],
        out_specs=[row(d), row(d), pl.BlockSpec((1, 8, tm), lambda i: (i, 0, 0)), row(LANES),
                   pl.BlockSpec((1, N_EXPERTS, LANES), lambda i: (i, 0, 0))],
        out_shape=[jax.ShapeDtypeStruct((n, d), F32), jax.ShapeDtypeStruct((n, d), BF16),
                   jax.ShapeDtypeStruct((n // tm, 8, tm), F32), jax.ShapeDtypeStruct((n, LANES), F32),
                   jax.ShapeDtypeStruct((n // tm, N_EXPERTS, LANES), F32)],
        compiler_params=_cparams(("parallel",)),
        name="out_proj_router",
    )(ya, zc, zc, yc, yd, x2, mod_l, wo, cw, cb, lng, lnb, rw, rb)


def _pack_bf16_pairs(x):
    half = x.shape[1] // 2
    xb = x.astype(BF16).astype(F32)
    hi = lax.bitcast_convert_type(xb[:, :half], jnp.uint32)
    lo = lax.bitcast_convert_type(xb[:, half:], jnp.uint32)
    return (hi & jnp.uint32(0xFFFF0000)) | (lo >> 16)


def _unpack_bf16_pairs(w):
    hi = lax.bitcast_convert_type(w & jnp.uint32(0xFFFF0000), F32).astype(BF16)
    lo = lax.bitcast_convert_type(w << 16, F32).astype(BF16)
    return jnp.concatenate([hi, lo], axis=1)


def _segment_pieces(count, fn):
    for b in reversed(range(SEG_PIECE_BITS)):
        size = SEG_ALIGN << b
        shift = SEG_ALIGN.bit_length() - 1 + b
        hi = lax.shift_left(lax.shift_right_logical(count, shift + 1), shift + 1)

        @pl.when((lax.shift_right_logical(count, shift) & 1) == 1)
        def _():
            fn(hi, size)


def _slot_positions(onehot_bf16, strict_lower_bf16, transposed):
    if transposed:
        return jnp.dot(strict_lower_bf16, onehot_bf16, preferred_element_type=F32)
    return lax.dot_general(onehot_bf16, strict_lower_bf16, (((1,), (1,)), ((), ())), preferred_element_type=F32)


def _dispatch_kernel(off_s, pc_s, gb_s, gs_s, gl_s, tail_s, u_ref, rt_ref, xs_ref, slab, zbuf, sem):
    blk = pl.program_id(0)
    tb = u_ref.shape[0]
    n_slots = slab.shape[1]
    e1 = rt_ref[0, 0:1, :]
    e2 = rt_ref[0, 1:2, :]
    sub_e = lax.broadcasted_iota(jnp.int32, (N_EXPERTS, tb), 0).astype(F32)
    hit1 = e1 == sub_e
    hit2 = e2 == sub_e
    onehot = jnp.where(hit1 | hit2, 1.0, 0.0).astype(BF16)
    r = lax.broadcasted_iota(jnp.int32, (tb, tb), 0)
    c = lax.broadcasted_iota(jnp.int32, (tb, tb), 1)
    lower = jnp.where(c < r, 1.0, 0.0).astype(BF16)
    rank = _slot_positions(onehot, lower, transposed=False)
    sub1 = lax.broadcasted_iota(jnp.int32, (N_EXPERTS, 1), 0)
    offv = jnp.zeros((N_EXPERTS, 1), F32)
    for e in range(N_EXPERTS):
        offv = jnp.where(sub1 == e, off_s[blk * N_EXPERTS + e].astype(F32), offv)
    pos = offv + rank
    pos1 = jnp.sum(jnp.where(hit1, pos, 0.0), axis=0, keepdims=True)
    pos2 = jnp.sum(jnp.where(hit2, pos, 0.0), axis=0, keepdims=True)
    slot = lax.broadcasted_iota(jnp.int32, (n_slots, tb), 0).astype(F32)
    perm = jnp.where((slot == pos1) | (slot == pos2), 1.0, 0.0).astype(BF16)
    cur = blk % 2
    slab[cur] = _pack_bf16_pairs(jnp.dot(perm, u_ref[...], preferred_element_type=F32))

    @pl.when(blk == 0)
    def _():
        zbuf[...] = jnp.zeros_like(zbuf)

    def copies(b, slot, go):
        for e in range(N_EXPERTS):
            idx = b * N_EXPERTS + e
            src0, dst0 = off_s[idx], gb_s[idx]

            def piece(hi, size, src0=src0, dst0=dst0):
                go(pltpu.make_async_copy(slab.at[slot, pl.ds(pl.multiple_of(src0 + hi, SEG_ALIGN), size)],
                                         xs_ref.at[pl.ds(pl.multiple_of(dst0 + hi, SEG_ALIGN), size)],
                                         sem.at[slot]))
            _segment_pieces(pc_s[idx], piece)

        @pl.when(b == 0)
        def _():
            for e in range(N_EXPERTS):
                dst0 = gs_s[e]

                def piece(hi, size, dst0=dst0):
                    go(pltpu.make_async_copy(zbuf.at[pl.ds(0, size)],
                                             xs_ref.at[pl.ds(pl.multiple_of(dst0 + hi, SEG_ALIGN), size)],
                                             sem.at[slot]))
                _segment_pieces(gl_s[e], piece)

            zrows = zbuf.shape[0]

            def tail_chunk(i, carry):
                go(pltpu.make_async_copy(zbuf, xs_ref.at[pl.ds(pl.multiple_of(tail_s[0] + i * zrows, zrows), zrows)],
                                         sem.at[slot]))
                return carry
            lax.fori_loop(0, tail_s[1], tail_chunk, 0)

    copies(blk, cur, lambda cp: cp.start())

    @pl.when(blk > 0)
    def _():
        copies(blk - 1, 1 - cur, lambda cp: cp.wait())

    @pl.when(blk == pl.num_programs(0) - 1)
    def _():
        copies(blk, cur, lambda cp: cp.wait())


def _dispatch_call(meta, u2, rt, n_rows):
    n, d = u2.shape
    tb = DISPATCH_BLOCK
    per_tile = rt.shape[2] // tb
    grid_spec = pltpu.PrefetchScalarGridSpec(
        num_scalar_prefetch=6,
        grid=(n // tb,),
        in_specs=[pl.BlockSpec((tb, d), lambda i, *_: (i, 0)),
                  pl.BlockSpec((1, 8, tb), lambda i, *_: (i // per_tile, 0, i % per_tile))],
        out_specs=pl.BlockSpec(memory_space=pl.ANY),
        scratch_shapes=[pltpu.VMEM((2, DISPATCH_SLOTS, d // 2), jnp.uint32),
                        pltpu.VMEM((SEG_ALIGN << (SEG_PIECE_BITS - 1), d // 2), jnp.uint32),
                        pltpu.SemaphoreType.DMA((2,))],
    )
    return pl.pallas_call(
        _dispatch_kernel,
        grid_spec=grid_spec,
        out_shape=jax.ShapeDtypeStruct((n_rows, d // 2), jnp.uint32),
        compiler_params=_cparams(("arbitrary",)),
        name="moe_dispatch",
    )(meta["off"], meta["pc"], meta["gbase"], meta["gap_start"], meta["gap_len"], meta["tail"], u2, rt)


def _experts_kernel(te_s, na_s, x_ref, wg_ref, wu_ref, wd_ref, y_ref):
    active = pl.program_id(0) < na_s[0]

    @pl.when(active)
    def _():
        x = _unpack_bf16_pairs(x_ref[...])
        gate = jnp.dot(x, wg_ref[0], preferred_element_type=F32)
        up = jnp.dot(x, wu_ref[0], preferred_element_type=F32)
        hidden = (gate * _sigmoid(gate) * up).astype(BF16)
        y_ref[...] = _pack_bf16_pairs(jnp.dot(hidden, wd_ref[0], preferred_element_type=F32))

    @pl.when(jnp.logical_not(active))
    def _():
        y_ref[...] = jnp.zeros_like(y_ref)


def _experts_call(meta, xs, wg, wu, wd):
    n_rows = xs.shape[0]
    tm = TM_MOE
    _, d, de = wg.shape
    tile = lambda i, te, na: jnp.minimum(i, na[0] - 1)
    grid_spec = pltpu.PrefetchScalarGridSpec(
        num_scalar_prefetch=2,
        grid=(n_rows // tm,),
        in_specs=[pl.BlockSpec((tm, d // 2), lambda i, te, na: (tile(i, te, na), 0)),
                  pl.BlockSpec((1, d, de), lambda i, te, na: (te[tile(i, te, na)], 0, 0)),
                  pl.BlockSpec((1, d, de), lambda i, te, na: (te[tile(i, te, na)], 0, 0)),
                  pl.BlockSpec((1, de, d), lambda i, te, na: (te[tile(i, te, na)], 0, 0))],
        out_specs=pl.BlockSpec((tm, d // 2), lambda i, te, na: (i, 0)),
    )
    return pl.pallas_call(
        _experts_kernel,
        grid_spec=grid_spec,
        out_shape=jax.ShapeDtypeStruct((n_rows, d // 2), jnp.uint32),
        compiler_params=_cparams(("arbitrary",)),
        name="experts",
    )(meta["tile_expert"], meta["n_active"], xs, wg, wu, wd)


def _combine_kernel(off_s, pc_s, gb_s, rtc_ref, x1_ref, mod_ref, lng_ref, lnb_ref, ys_ref, o_ref, slab, sem):
    blk = pl.program_id(0)
    tb = x1_ref.shape[0]
    n_slots = slab.shape[1]
    cur = blk % 2

    def copies(b, slot, go):
        for e in range(N_EXPERTS):
            idx = b * N_EXPERTS + e
            dst0, src0 = off_s[idx], gb_s[idx]

            def piece(hi, size, src0=src0, dst0=dst0):
                go(pltpu.make_async_copy(ys_ref.at[pl.ds(pl.multiple_of(src0 + hi, SEG_ALIGN), size)],
                                         slab.at[slot, pl.ds(pl.multiple_of(dst0 + hi, SEG_ALIGN), size)],
                                         sem.at[slot]))
            _segment_pieces(pc_s[idx], piece)

    @pl.when(blk == 0)
    def _():
        slab[...] = jnp.zeros_like(slab)
        copies(blk, cur, lambda cp: cp.start())

    @pl.when(blk + 1 < pl.num_programs(0))
    def _():
        copies(blk + 1, 1 - cur, lambda cp: cp.start())

    rtc = rtc_ref[...]
    e1, e2, g1, g2 = rtc[:, 0:1], rtc[:, 1:2], rtc[:, 2:3], rtc[:, 3:4]
    lane_e = lax.broadcasted_iota(jnp.int32, (1, LANES), 1).astype(F32)
    hit1 = e1 == lane_e
    hit2 = e2 == lane_e
    onehot = jnp.where(hit1 | hit2, 1.0, 0.0).astype(BF16)
    r = lax.broadcasted_iota(jnp.int32, (tb, tb), 0)
    c = lax.broadcasted_iota(jnp.int32, (tb, tb), 1)
    lower = jnp.where(c < r, 1.0, 0.0).astype(BF16)
    rank = _slot_positions(onehot, lower, transposed=True)
    lane1 = lax.broadcasted_iota(jnp.int32, (1, LANES), 1)
    offv = jnp.zeros((1, LANES), F32)
    for e in range(N_EXPERTS):
        offv = jnp.where(lane1 == e, off_s[blk * N_EXPERTS + e].astype(F32), offv)
    pos = offv + rank
    pos1 = jnp.sum(jnp.where(hit1, pos, 0.0), axis=-1, keepdims=True)
    pos2 = jnp.sum(jnp.where(hit2, pos, 0.0), axis=-1, keepdims=True)
    slot = lax.broadcasted_iota(jnp.int32, (1, n_slots), 1).astype(F32)
    sel1 = jnp.where(slot == pos1, 1.0, 0.0).astype(BF16)
    sel2 = jnp.where(slot == pos2, 1.0, 0.0).astype(BF16)

    copies(blk, cur, lambda cp: cp.wait())
    ys = _unpack_bf16_pairs(slab[cur])
    y = (g1 * jnp.dot(sel1, ys, preferred_element_type=F32) + g2 * jnp.dot(sel2, ys, preferred_element_type=F32))
    mod = mod_ref[0]
    o_ref[...] = _layer_norm(ALPHA * x1_ref[...] + (1.0 + mod[5:6, :]) * y, lng_ref[...], lnb_ref[...])


def _combine_call(meta, ys, rtc, x1, mod_l, lng, lnb, seq):
    n, d = x1.shape
    tb = DISPATCH_BLOCK
    bps = seq // tb
    grid_spec = pltpu.PrefetchScalarGridSpec(
        num_scalar_prefetch=3,
        grid=(n // tb,),
        in_specs=[pl.BlockSpec((tb, LANES), lambda i, *_: (i, 0)),
                  pl.BlockSpec((tb, d), lambda i, *_: (i, 0)),
                  pl.BlockSpec((1, 6, d), lambda i, *_: (i // bps, 0, 0)),
                  pl.BlockSpec((1, d), lambda i, *_: (0, 0)),
                  pl.BlockSpec((1, d), lambda i, *_: (0, 0)),
                  pl.BlockSpec(memory_space=pl.ANY)],
        out_specs=pl.BlockSpec((tb, d), lambda i, *_: (i, 0)),
        scratch_shapes=[pltpu.VMEM((2, DISPATCH_SLOTS, d // 2), jnp.uint32), pltpu.SemaphoreType.DMA((2,))],
    )
    return pl.pallas_call(
        _combine_kernel,
        grid_spec=grid_spec,
        out_shape=jax.ShapeDtypeStruct((n, d), F32),
        compiler_params=_cparams(("arbitrary",)),
        name="moe_combine_ln2",
    )(meta["off"], meta["pc"], meta["gbase"], rtc, x1, mod_l, lng, lnb, ys)


def _sorted_rows_bound(n_tokens):
    n_blocks = n_tokens // DISPATCH_BLOCK
    worst = 2 * n_tokens + n_blocks * N_EXPERTS * (SEG_ALIGN - 1) + N_EXPERTS * (TM_MOE - 1)
    return -(-worst // TM_MOE) * TM_MOE


def _route_meta(cnt, n_rows):
    pc = (cnt + SEG_ALIGN - 1) // SEG_ALIGN * SEG_ALIGN
    off = jnp.cumsum(pc, axis=1) - pc
    tot = pc.sum(axis=0)
    reg = (tot + TM_MOE - 1) // TM_MOE * TM_MOE
    reg_end = jnp.cumsum(reg)
    ebase = reg_end - reg
    gbase = ebase[None, :] + jnp.cumsum(pc, axis=0) - pc
    tiles = jnp.arange(n_rows // TM_MOE, dtype=jnp.int32)
    tile_expert = jnp.minimum(jnp.sum(tiles[:, None] >= (reg_end // TM_MOE)[None, :], axis=1), N_EXPERTS - 1)
    i32 = lambda a: a.astype(jnp.int32).reshape(-1)
    used = reg_end[-1:]
    tail_chunk = SEG_ALIGN << (SEG_PIECE_BITS - 1)
    return {"off": i32(off), "pc": i32(pc), "gbase": i32(gbase), "gap_start": i32(ebase + tot),
            "gap_len": i32(reg - tot), "tail": i32(jnp.concatenate([used, (n_rows - used) // tail_chunk])),
            "tile_expert": i32(tile_expert), "n_active": i32(used // TM_MOE)}


def _rearranged_w_in(w_in_l):
    d = w_in_l.shape[0]
    o_gate, o_cb, o_ga = ZM, ZM + 2 * HEADS, ZM + 2 * HEADS + ZC + ZQ + ZK + ZVP + ZG
    cols = [w_in_l[:, 0:ZM], w_in_l[:, o_cb:o_ga],
            w_in_l[:, o_gate:o_gate + 2 * HEADS], w_in_l[:, o_ga:o_ga + GLA_RANK],
            jnp.zeros((d, ZS - 2 * HEADS - GLA_RANK), w_in_l.dtype)]
    return jnp.concatenate(cols, axis=1).astype(BF16)


def _rope_tables(positions):
    inv_freq = ROPE_THETA ** (-jnp.arange(0, ROPE_DIM, 2, dtype=F32) / ROPE_DIM)
    ang = positions.astype(F32).reshape(-1, 1) * inv_freq
    cos, sin = jnp.cos(ang), jnp.sin(ang)
    d = np.arange(ZQ) % DIFF_DIM
    half = ROPE_DIM // 2
    idx = d % half
    cosf = jnp.where(d < ROPE_DIM, cos[:, idx], 1.0)
    sinf = jnp.where(d < half, -sin[:, idx], jnp.where(d < ROPE_DIM, sin[:, idx], 0.0))
    return cosf, sinf


def _pad_lanes(v, width=LANES):
    v = v.reshape(1, -1)
    return jnp.pad(v, ((0, 0), (0, width - v.shape[1])))


def kernel(x, c, positions, w_ada, b_ada, w_in, w_out, mlstm_b_i, mlstm_b_f, mlstm_norm_g, conv_w, conv_b, diff_lambda_q1, diff_lambda_k1, diff_lambda_q2, diff_lambda_k2, diff_norm_g, gla_w_a2, gla_b_a, gla_norm_g, ln1_g, ln1_b, ln2_g, ln2_b, router_w, router_bias, exp_w_gate, exp_w_up, exp_w_down):
    bsz, seq, d = x.shape
    depth = w_in.shape[0]
    n = bsz * seq
    x2 = x.reshape(n, d)
    mod = _mod_call(c, w_ada, b_ada)
    cosf, sinf = _rope_tables(positions)
    rw_t = router_w.T.astype(BF16)
    rb = router_bias.reshape(N_EXPERTS, 1).astype(F32)
    n_sorted = _sorted_rows_bound(n)
    for l in range(depth):
        lam_init = 0.8 - 0.6 * math.exp(-0.3 * l)
        w_l = _rearranged_w_in(w_in[l])
        zm, zc, zq, zk, zvt, zg, zs = _inproj_call(x2, mod[l], w_l, cosf, sinf, seq)
        bif = _pad_lanes(jnp.concatenate([mlstm_b_i[l], mlstm_b_f[l]]))
        ya = _mlstm_call(zm, zs, bif, mlstm_norm_g[l].reshape(1, -1), bsz, seq)
        lamp = jnp.concatenate([_pad_lanes(diff_lambda_q1[l]), _pad_lanes(diff_lambda_k1[l]),
                                _pad_lanes(diff_lambda_q2[l]), _pad_lanes(diff_lambda_k2[l])], axis=0)
        yc = _attn_call(zq, zk, zvt, lamp, diff_norm_g[l].reshape(1, -1), lam_init, bsz, seq)
        wa = jnp.pad(gla_w_a2[l], ((2 * HEADS, ZS - 2 * HEADS - GLA_RANK), (0, 0))).astype(BF16)
        yd = _gla_call(zg, zs, wa, gla_b_a[l].reshape(1, -1), gla_norm_g[l].reshape(1, -1), bsz, seq)
        x1, u2, rt, rtc, cnt = _outproj_call(ya, zc, yc, yd, x2, mod[l], w_out[l].astype(BF16), conv_w[l],
                                             conv_b[l].reshape(1, -1), ln1_g[l].reshape(1, -1),
                                             ln1_b[l].reshape(1, -1), rw_t, rb, seq)
        per_tile = TM_PROJ // DISPATCH_BLOCK
        cnt_blk = cnt[:, :, :per_tile].transpose(0, 2, 1).reshape(-1, N_EXPERTS).astype(jnp.int32)
        meta = _route_meta(cnt_blk, n_sorted)
        xs = _dispatch_call(meta, u2, rt, n_sorted)
        ys = _experts_call(meta, xs, exp_w_gate[l].astype(BF16), exp_w_up[l].astype(BF16),
                           exp_w_down[l].astype(BF16))
        x2 = _combine_call(meta, ys, rtc, x1, mod[l], ln2_g[l].reshape(1, -1), ln2_b[l].reshape(1, -1), seq)
    return x2.reshape(bsz, seq, d)
```

```python
import functools
import math

import numpy as np
import jax
import jax.numpy as jnp
from jax import lax
from jax.experimental import pallas as pl
from jax.experimental.pallas import tpu as pltpu

F32 = jnp.float32
BF16 = jnp.bfloat16

D_MODEL = 1024
DEPTH = 4
HEADS = 4
GW = 256
HEAD_DIM = 64
DIFF_DIM = 32
ROPE_DIM = 8
ROPE_THETA = 500000.0
CONV_WIDTH = 3
GLA_RANK = 16
GLA_TAU = 16.0
CHUNK = 64
GLA_CHUNK = 64
N_EXPERTS = 16
N_GROUPS = 4
GROUP_SIZE = 4
D_EXPERT = 512
ALPHA = (2 * DEPTH) ** 0.25
EPS = 1e-6
LANES = 128
SUBLANES = 8
NEG_INF = float("-inf")
LOG2_E = math.log2(math.e)

ZM, ZC, ZQ, ZK, ZVP, ZG, ZS = 1024, 768, 256, 256, 256, 1024, 128
ZV = HEADS * LANES
Z_OFFS = np.cumsum([0, ZM, ZC, ZQ, ZK, ZVP, ZG, ZS])
Z_TOTAL = int(Z_OFFS[-1])

TM_PROJ = 512
TB_MLSTM = 256
TB_GLA = 512
CUM_SPAN = 128
TQ_ATTN = 512
TM_MOE = 512
DISPATCH_BLOCK = 256
SEG_ALIGN = 8
SEG_PIECE_BITS = 6
DISPATCH_SLOTS = 2 * DISPATCH_BLOCK + N_EXPERTS * SEG_ALIGN
assert SEG_ALIGN << (SEG_PIECE_BITS - 1) >= DISPATCH_BLOCK and SEG_ALIGN << SEG_PIECE_BITS >= TM_MOE
VMEM_LIMIT = 56 * 1024 * 1024


def _cparams(sem):
    return pltpu.CompilerParams(dimension_semantics=sem, vmem_limit_bytes=VMEM_LIMIT)


def _log_sigmoid(x):
    return jnp.minimum(x, 0.0) - jnp.log(1.0 + jnp.exp(-jnp.abs(x)))


def _sigmoid(x):
    return 1.0 / (1.0 + jnp.exp(-x))


def _layer_norm(v, g, b):
    mu = jnp.mean(v, axis=-1, keepdims=True)
    d = v - mu
    var = jnp.mean(d * d, axis=-1, keepdims=True)
    return d * lax.rsqrt(var + EPS) * g + b


def _mod_kernel(c_ref, w_ref, b_ref, o_ref):
    c = c_ref[...]
    c_act = (c * _sigmoid(c)).astype(BF16)
    o_ref[0, 0] = jnp.dot(c_act, w_ref[0].astype(BF16), preferred_element_type=F32) + b_ref[0, 0]


def _mod_call(c, w_ada, b_ada):
    depth, d, _ = w_ada.shape
    bsz = c.shape[0]
    b3 = b_ada.reshape(depth, 6, 1, d)
    out = pl.pallas_call(
        _mod_kernel,
        grid=(depth, 6),
        in_specs=[pl.BlockSpec((bsz, d), lambda l, j: (0, 0)),
                  pl.BlockSpec((1, d, d), lambda l, j: (l, 0, j)),
                  pl.BlockSpec((1, 1, 1, d), lambda l, j: (l, j, 0, 0))],
        out_specs=pl.BlockSpec((1, 1, bsz, d), lambda l, j: (l, j, 0, 0)),
        out_shape=jax.ShapeDtypeStruct((depth, 6, bsz, d), F32),
        compiler_params=_cparams(("arbitrary", "arbitrary")),
        name="adaln_mod",
    )(c, w_ada, b3)
    return out.transpose(0, 2, 1, 3)


def _inproj_kernel(x_ref, mod_ref, w_ref, cos_ref, sin_ref, zm_ref, zc_ref, zq_ref, zk_ref, zvt_ref, zg_ref,
                   zs_ref):
    mod = mod_ref[0]
    u = (x_ref[...] * (1.0 + mod[1:2, :]) + mod[0:1, :]).astype(BF16)

    def proj(idx):
        return jnp.dot(u, w_ref[:, int(Z_OFFS[idx]):int(Z_OFFS[idx + 1])], preferred_element_type=F32)

    zm_ref[...] = proj(0).astype(BF16)
    zc_ref[...] = proj(1).astype(BF16)

    lane = lax.broadcasted_iota(jnp.int32, (1, ZQ), 1)
    first_half = (lane % ROPE_DIM) < (ROPE_DIM // 2)
    cosf = cos_ref[...]
    sinf = sin_ref[...]

    def rope(r):
        swapped = jnp.where(first_half, pltpu.roll(r, ZQ - ROPE_DIM // 2, 1), pltpu.roll(r, ROPE_DIM // 2, 1))
        return r * cosf + swapped * sinf

    zq_ref[...] = (rope(proj(2)) * (LOG2_E * DIFF_DIM ** -0.5)).astype(BF16)
    zk_ref[...] = rope(proj(3)).astype(BF16)
    vt = proj(4).T
    tm = vt.shape[1]
    ones_rows = jnp.where(lax.broadcasted_iota(jnp.int32, (LANES - HEAD_DIM, tm), 0) == 0, 1.0, 0.0)
    zvt_ref[...] = jnp.concatenate([blk for h in range(HEADS)
                                    for blk in (vt[h * HEAD_DIM:(h + 1) * HEAD_DIM], ones_rows)], axis=0).astype(BF16)
    zg_ref[...] = proj(5).astype(BF16)
    zs_ref[...] = proj(6)


def _inproj_call(x2, mod_l, w_l, cosf, sinf, seq):
    n, d = x2.shape
    tm = TM_PROJ
    tps = seq // tm
    row = lambda w: pl.BlockSpec((tm, w), lambda i: (i, 0))
    outs = [(ZM, BF16), (ZC, BF16), (ZQ, BF16), (ZK, BF16), (ZV, BF16), (ZG, BF16), (ZS, F32)]
    return pl.pallas_call(
        _inproj_kernel,
        grid=(n // tm,),
        in_specs=[row(d),
                  pl.BlockSpec((1, 6, d), lambda i: (i // tps, 0, 0)),
                  pl.BlockSpec((d, Z_TOTAL), lambda i: (0, 0)),
                  row(ZQ), row(ZQ)],
        out_specs=[pl.BlockSpec((ZV, tm), lambda i: (0, i)) if k == 4 else row(w) for k, (w, _) in enumerate(outs)],
        out_shape=[jax.ShapeDtypeStruct((ZV, n) if k == 4 else (n, w), dt) for k, (w, dt) in enumerate(outs)],
        compiler_params=_cparams(("parallel",)),
        name="in_proj",
    )(x2, mod_l, w_l, cosf, sinf)


NT_DIMS = (((1,), (1,)), ((), ()))
TN_DIMS = (((0,), (0,)), ((), ()))


def _head_of(idx):
    return idx // HEAD_DIM


def _cumsum_matrix(span, chunk):
    r = lax.broadcasted_iota(jnp.int32, (span, span), 0)
    c = lax.broadcasted_iota(jnp.int32, (span, span), 1)
    return jnp.where(((r // chunk) == (c // chunk)) & (c <= r), 1.0, 0.0).astype(BF16)


def _bf16_terms(x, n_terms):
    terms, rest = [], x
    for _ in range(n_terms):
        t = rest.astype(BF16)
        terms.append(t)
        rest = rest - t.astype(F32)
    return terms


def _exact_matmul(lhs_f32, rhs01, n_terms=3):
    return functools.reduce(lambda a, b: a + b, [jnp.dot(t, rhs01, preferred_element_type=F32)
                                                 for t in reversed(_bf16_terms(lhs_f32, n_terms))])


def _chunk_cumsum(x, chunk):
    cm = _cumsum_matrix(CUM_SPAN, chunk)
    outs = []
    for i in range(x.shape[0] // CUM_SPAN):
        terms = _bf16_terms(x[i * CUM_SPAN:(i + 1) * CUM_SPAN], 3)
        outs.append(functools.reduce(lambda a, b: a + b,
                                     [jnp.dot(cm, t, preferred_element_type=F32) for t in reversed(terms)]))
    return jnp.concatenate(outs, axis=0)


def _rows_broadcast(x, chunk, row):
    n = x.shape[0] // chunk
    return jnp.concatenate([jnp.broadcast_to(x[c * chunk + row:c * chunk + row + 1, :], (chunk, x.shape[1]))
                            for c in range(n)], axis=0)


def _head_masks(chunk):
    lane_head = _head_of(lax.broadcasted_iota(jnp.int32, (1, GW), 1))
    row_head = lax.broadcasted_iota(jnp.int32, (HEADS * chunk, 1), 0) // chunk
    stack_keep = row_head == lane_head
    rr = lax.broadcasted_iota(jnp.int32, (HEADS * chunk, chunk), 0) % chunk
    cc = lax.broadcasted_iota(jnp.int32, (HEADS * chunk, chunk), 1)
    block_diag = (_head_of(lax.broadcasted_iota(jnp.int32, (GW, GW), 0))
                  == _head_of(lax.broadcasted_iota(jnp.int32, (GW, GW), 1)))
    return stack_keep, cc <= rr, block_diag


def _unstack_heads(x4, keep, chunk):
    x4 = jnp.where(keep, x4, 0.0)
    return functools.reduce(lambda a, b: a + b, [x4[h * chunk:(h + 1) * chunk] for h in range(HEADS)])


def _head_rms_norm(x, block_diag):
    seg = jnp.where(block_diag, 1.0, 0.0).astype(BF16)
    return x * lax.rsqrt(_exact_matmul(x * x, seg, 2) * (1.0 / HEAD_DIM) + EPS)


def _mlstm_kernel(zm_ref, zs_ref, bif_ref, g_ref, o_ref, ct_scr, nm_scr, ni_scr, up_scr):
    @pl.when(pl.program_id(1) == 0)
    def _():
        ct_scr[...] = jnp.zeros_like(ct_scr)
        nm_scr[...] = jnp.zeros_like(nm_scr)

    tb = zm_ref.shape[0]
    L = CHUNK
    nc = tb // L
    lane = lax.broadcasted_iota(jnp.int32, (1, LANES), 1)
    graw = zs_ref[...] + bif_ref[...]
    gp = jnp.where(lane < HEADS, graw, jnp.where(lane < 2 * HEADS, _log_sigmoid(graw), 0.0))
    cum = _chunk_cumsum(gp, L)

    def expand(x, first_lane):
        sel = jnp.where((lax.broadcasted_iota(jnp.int32, (LANES, GW), 0) - first_lane)
                        == _head_of(lax.broadcasted_iota(jnp.int32, (LANES, GW), 1)), 1.0, 0.0).astype(BF16)
        return _exact_matmul(x, sel)

    bx = expand(cum, HEADS)
    lix = expand(gp, 0)
    bex = _rows_broadcast(bx, L, L - 1)
    dkx = bex - bx + lix
    dkmax = [jnp.max(dkx[c * L:(c + 1) * L], axis=0, keepdims=True) for c in range(nc)]
    kw = (zm_ref[:, GW:2 * GW].astype(F32)
          * jnp.exp(dkx - jnp.concatenate([jnp.broadcast_to(d, (L, GW)) for d in dkmax], axis=0))).astype(BF16)
    rowv_t = (cum - pltpu.roll(gp, HEADS, 1)).T
    stack_keep, causal4, block_diag = _head_masks(L)
    stack_keep2 = jnp.concatenate([stack_keep, stack_keep], axis=1)
    ones_blk = jnp.ones((L, GW), BF16)

    dmaxx = []
    for c in range(nc):
        rs = slice(c * L, (c + 1) * L)
        qc = zm_ref[rs, 0:GW] * (HEAD_DIM ** -0.5)
        kc = zm_ref[rs, GW:2 * GW]
        vc = zm_ref[rs, 2 * GW:3 * GW]
        q4 = jnp.where(stack_keep, jnp.concatenate([qc] * HEADS, axis=0), jnp.zeros((), BF16))
        s4 = lax.dot_general(q4, kc, NT_DIMS, preferred_element_type=F32)
        bcol4 = jnp.concatenate([cum[rs, HEADS + h:HEADS + h + 1] for h in range(HEADS)], axis=0)
        brow4 = jnp.concatenate([jnp.broadcast_to(rowv_t[HEADS + h:HEADS + h + 1, rs], (L, L))
                                 for h in range(HEADS)], axis=0)
        d4 = jnp.where(causal4, bcol4 - brow4, NEG_INF)
        dmax4 = jnp.max(d4, axis=-1, keepdims=True)
        p4 = (s4 * jnp.exp(d4 - dmax4)).astype(BF16)
        i4 = jnp.dot(p4, jnp.concatenate([vc, ones_blk], axis=1), preferred_element_type=F32)
        ni_scr[rs, :] = _unstack_heads(i4, stack_keep2, L)
        dmaxx.append(_unstack_heads(jnp.broadcast_to(dmax4, (HEADS * L, GW)), stack_keep, L))
        up = lax.dot_general(vc, kw[rs], TN_DIMS, preferred_element_type=F32)
        up_scr[c] = jnp.where(block_diag, up, 0.0)

    ct = ct_scr[...]
    n_row = nm_scr[0:1, :]
    m_prev = nm_scr[1:2, :]
    hv_rows = []
    for c in range(nc):
        rs = slice(c * L, (c + 1) * L)
        qc = zm_ref[rs, 0:GW] * (HEAD_DIM ** -0.5)
        nmat = jnp.where(block_diag, jnp.broadcast_to(n_row, (GW, GW)), 0.0).astype(BF16)
        num_inter = lax.dot_general(qc, ct.astype(BF16), NT_DIMS, preferred_element_type=F32)
        den_inter = lax.dot_general(qc, nmat, NT_DIMS, preferred_element_type=F32)
        inter = bx[rs] + m_prev
        m_row = jnp.maximum(inter, dmaxx[c])
        e1 = jnp.exp(dmaxx[c] - m_row)
        e2 = jnp.exp(inter - m_row)
        num = e1 * ni_scr[rs, 0:GW] + e2 * num_inter
        den = e1 * ni_scr[rs, GW:2 * GW] + e2 * den_inter
        hv_rows.append(num / jnp.maximum(jnp.abs(den), jnp.exp(-m_row)))
        b_end = bex[c * L:c * L + 1, :]
        m_new = jnp.maximum(b_end + m_prev, dkmax[c])
        decay = jnp.exp(b_end + m_prev - m_new)
        f2 = jnp.exp(dkmax[c] - m_new)
        ct = ct * decay + up_scr[c] * f2
        n_row = n_row * decay + f2 * jnp.sum(kw[rs].astype(F32), axis=0, keepdims=True)
        m_prev = m_new
    ct_scr[...] = ct
    nm_scr[0:1, :] = n_row
    nm_scr[1:2, :] = m_prev

    hn = _head_rms_norm(jnp.concatenate(hv_rows, axis=0), block_diag)
    o_ref[...] = (hn * g_ref[...] * _sigmoid(zm_ref[:, 3 * GW:4 * GW].astype(F32))).astype(BF16)


def _mlstm_call(zm, zs, bif, gain, bsz, seq):
    tb = TB_MLSTM
    tps = seq // tb
    return pl.pallas_call(
        _mlstm_kernel,
        grid=(bsz, tps),
        in_specs=[pl.BlockSpec((tb, ZM), lambda b, t: (b * tps + t, 0)),
                  pl.BlockSpec((tb, ZS), lambda b, t: (b * tps + t, 0)),
                  pl.BlockSpec((1, LANES), lambda b, t: (0, 0)),
                  pl.BlockSpec((1, GW), lambda b, t: (0, 0))],
        out_specs=pl.BlockSpec((tb, GW), lambda b, t: (b * tps + t, 0)),
        out_shape=jax.ShapeDtypeStruct((bsz * seq, GW), BF16),
        scratch_shapes=[pltpu.VMEM((GW, GW), F32),
                        pltpu.VMEM((8, GW), F32),
                        pltpu.VMEM((tb, 2 * GW), F32),
                        pltpu.VMEM((tb // CHUNK, GW, GW), F32)],
        compiler_params=_cparams(("parallel", "arbitrary")),
        name="mlstm",
    )(zm, zs, bif, jnp.tile(gain, (1, HEADS)))


def _gla_kernel(zg_ref, zs_ref, wa_ref, ba_ref, g_ref, o_ref, st_scr, oi_scr, up_scr):
    @pl.when(pl.program_id(1) == 0)
    def _():
        st_scr[...] = jnp.zeros_like(st_scr)

    tb = zg_ref.shape[0]
    lg = GLA_CHUNK
    nc = tb // lg
    la = _log_sigmoid(jnp.dot(zs_ref[...].astype(BF16), wa_ref[...], preferred_element_type=F32)
                      + ba_ref[...]) * (1.0 / GLA_TAU)
    bc = _chunk_cumsum(la, lg)
    be = _rows_broadcast(bc, lg, lg - 1)
    bm = _rows_broadcast(bc, lg, lg // 2)
    q = zg_ref[:, 0:GW].astype(F32) * (HEAD_DIM ** -0.5)
    k = zg_ref[:, GW:2 * GW].astype(F32)
    qm = (q * jnp.exp(bc - bm)).astype(BF16)
    km = (k * jnp.exp(bm - bc)).astype(BF16)
    qt = (q * jnp.exp(bc)).astype(BF16)
    ks = (k * jnp.exp(be - bc)).astype(BF16)
    dec = jnp.exp(be)
    stack_keep, causal4, block_diag = _head_masks(lg)

    for c in range(nc):
        rs = slice(c * lg, (c + 1) * lg)
        vc = zg_ref[rs, 2 * GW:3 * GW]
        q4 = jnp.where(stack_keep, jnp.concatenate([qm[rs]] * HEADS, axis=0), jnp.zeros((), BF16))
        att4 = lax.dot_general(q4, km[rs], NT_DIMS, preferred_element_type=F32)
        att4 = jnp.where(causal4, att4, 0.0).astype(BF16)
        oi_scr[rs, :] = _unstack_heads(jnp.dot(att4, vc, preferred_element_type=F32), stack_keep, lg)
        up = lax.dot_general(vc, ks[rs], TN_DIMS, preferred_element_type=F32)
        up_scr[c] = jnp.where(block_diag, up, 0.0)

    st = st_scr[...]
    inter = []
    for c in range(nc):
        rs = slice(c * lg, (c + 1) * lg)
        inter.append(lax.dot_general(qt[rs], st.astype(BF16), NT_DIMS, preferred_element_type=F32))
        st = st * dec[c * lg:c * lg + 1, :] + up_scr[c]
    st_scr[...] = st

    on = _head_rms_norm(oi_scr[...] + jnp.concatenate(inter, axis=0), block_diag)
    r = zg_ref[:, 3 * GW:4 * GW].astype(F32)
    o_ref[...] = (on * g_ref[...] * (r * _sigmoid(r))).astype(BF16)


def _gla_call(zg, zs, wa, ba, gain, bsz, seq):
    tb = TB_GLA
    tps = seq // tb
    return pl.pallas_call(
        _gla_kernel,
        grid=(bsz, tps),
        in_specs=[pl.BlockSpec((tb, ZG), lambda b, t: (b * tps + t, 0)),
                  pl.BlockSpec((tb, ZS), lambda b, t: (b * tps + t, 0)),
                  pl.BlockSpec((ZS, GW), lambda b, t: (0, 0)),
                  pl.BlockSpec((1, GW), lambda b, t: (0, 0)),
                  pl.BlockSpec((1, GW), lambda b, t: (0, 0))],
        out_specs=pl.BlockSpec((tb, GW), lambda b, t: (b * tps + t, 0)),
        out_shape=jax.ShapeDtypeStruct((bsz * seq, GW), BF16),
        scratch_shapes=[pltpu.VMEM((GW, GW), F32),
                        pltpu.VMEM((tb, GW), F32),
                        pltpu.VMEM((tb // GLA_CHUNK, GW, GW), F32)],
        compiler_params=_cparams(("parallel", "arbitrary")),
        name="gla",
    )(zg, zs, wa, ba, jnp.tile(gain, (1, HEADS)))


def _attn_kernel(lam_init, q_ref, k_ref, vt_ref, lamp_ref, g_ref, o_ref, qm_scr, m_scr, acc_scr, st_scr):
    tq = q_ref.shape[0]
    qi = pl.program_id(1)
    n_hc = 2 * HEADS
    lane = lax.broadcasted_iota(jnp.int32, (1, LANES), 1)
    for hc in range(n_hc):
        h, c = divmod(hc, 2)
        pair = h // 2
        lo = (h % 2) * HEAD_DIM + c * DIFF_DIM
        qp = q_ref[:, pair * LANES:(pair + 1) * LANES]
        qm_scr[hc] = jnp.where((lane >= lo) & (lane < lo + DIFF_DIM), qp, jnp.zeros_like(qp))
    m_scr[...] = jnp.full(m_scr.shape, NEG_INF, F32)
    acc_scr[...] = jnp.zeros_like(acc_scr)

    def sublane_allmax(x):
        for shift in (4, 2, 1):
            x = jnp.maximum(x, pltpu.roll(x, shift, 0))
        return x

    def kv_block(j, keep_t):
        key0 = pl.multiple_of(j * tq, tq)
        for hc in range(n_hc):
            pair = (hc // 2) // 2
            kb = k_ref[pl.ds(key0, tq), pair * LANES:(pair + 1) * LANES]
            st_scr[hc] = lax.dot_general(kb, qm_scr[hc], NT_DIMS, preferred_element_type=F32)
        for hc in range(n_hc):
            h = hc // 2
            vt = vt_ref[h * LANES:(h + 1) * LANES, pl.ds(key0, tq)]
            st = st_scr[hc]
            if keep_t is not None:
                st = jnp.where(keep_t, st, NEG_INF)
            st3 = st.reshape(tq // SUBLANES, SUBLANES, tq)
            m_old = m_scr[hc]
            m_new = jnp.maximum(m_old, sublane_allmax(jnp.max(st3, axis=0)))
            alpha = jnp.exp2(m_old - m_new)
            p = jnp.exp2(st3 - m_new[None]).reshape(tq, tq).astype(BF16)
            acc = acc_scr[hc].reshape(LANES // SUBLANES, SUBLANES, tq) * alpha[None]
            acc_scr[hc] = acc.reshape(LANES, tq) + jnp.dot(vt, p, preferred_element_type=F32)
            m_scr[hc] = m_new

    def full_block(j, carry):
        kv_block(j, None)
        return carry

    lax.fori_loop(0, qi, full_block, 0)
    key_idx = lax.broadcasted_iota(jnp.int32, (tq, tq), 0)
    query_idx = lax.broadcasted_iota(jnp.int32, (tq, tq), 1)
    kv_block(qi, key_idx <= query_idx)

    lamp = lamp_ref[...]
    lam = (jnp.exp(jnp.sum(lamp[0:1, :] * lamp[1:2, :], axis=-1, keepdims=True))
           - jnp.exp(jnp.sum(lamp[2:3, :] * lamp[3:4, :], axis=-1, keepdims=True)) + lam_init)
    gain = g_ref[...] * (1.0 - lam_init)
    outs = []
    for h in range(HEADS):
        a1 = acc_scr[2 * h]
        a2 = acc_scr[2 * h + 1]
        o = a1[:HEAD_DIM] / a1[HEAD_DIM:HEAD_DIM + 1] - lam * (a2[:HEAD_DIM] / a2[HEAD_DIM:HEAD_DIM + 1])
        on = o * lax.rsqrt(jnp.mean(o * o, axis=0, keepdims=True) + EPS)
        outs.append(on.T * gain)
    o_ref[...] = jnp.concatenate(outs, axis=1).astype(BF16)


def _attn_call(zq, zk, zvt, lamp, gain, lam_init, bsz, seq):
    tq = TQ_ATTN
    nq = seq // tq
    return pl.pallas_call(
        functools.partial(_attn_kernel, lam_init),
        grid=(bsz, nq),
        in_specs=[pl.BlockSpec((tq, ZQ), lambda b, i: (b * nq + i, 0)),
                  pl.BlockSpec((seq, ZK), lambda b, i: (b, 0)),
                  pl.BlockSpec((ZV, seq), lambda b, i: (0, b)),
                  pl.BlockSpec((4, LANES), lambda b, i: (0, 0)),
                  pl.BlockSpec((1, HEAD_DIM), lambda b, i: (0, 0))],
        out_specs=pl.BlockSpec((tq, GW), lambda b, i: (b * nq + i, 0)),
        out_shape=jax.ShapeDtypeStruct((bsz * seq, GW), BF16),
        scratch_shapes=[pltpu.VMEM((2 * HEADS, tq, LANES), BF16),
                        pltpu.VMEM((2 * HEADS, SUBLANES, tq), F32),
                        pltpu.VMEM((2 * HEADS, LANES, tq), F32),
                        pltpu.VMEM((2 * HEADS, tq, tq), F32)],
        compiler_params=_cparams(("parallel", "arbitrary")),
        name="diff_attn",
    )(zq, zk, zvt, lamp, gain)


def _outproj_kernel(tiles_per_seq, ya_ref, zc_ref, halo_ref, yc_ref, yd_ref, x_ref, mod_ref, wo_ref, cw_ref,
                    cb_ref, lng_ref, lnb_ref, rw_ref, rb_ref, x1_ref, u2_ref, rt_ref, rtc_ref, cnt_ref):
    tm = x_ref.shape[0]
    mod = mod_ref[0]
    zc = zc_ref[...].astype(F32)
    halo = halo_ref[...].astype(F32)
    first = (pl.program_id(0) % tiles_per_seq) == 0
    s_cur = zc[:, GW:2 * GW] * zc[:, 2 * GW:3 * GW]
    s_halo = jnp.where(first, 0.0, halo[:, GW:2 * GW] * halo[:, 2 * GW:3 * GW])
    s_ext = jnp.concatenate([s_halo, s_cur], axis=0)
    hw = halo.shape[0]
    cw = cw_ref[...]
    conv = (cw[0:1, :] * s_ext[hw - 2:hw - 2 + tm, :] + cw[1:2, :] * s_ext[hw - 1:hw - 1 + tm, :]
            + cw[2:3, :] * s_cur + cb_ref[...])
    yb = (zc[:, 0:GW] * conv).astype(BF16)

    y = (jnp.dot(ya_ref[...], wo_ref[0:GW, :], preferred_element_type=F32)
         + jnp.dot(yb, wo_ref[GW:2 * GW, :], preferred_element_type=F32)
         + jnp.dot(yc_ref[...], wo_ref[2 * GW:3 * GW, :], preferred_element_type=F32)
         + jnp.dot(yd_ref[...], wo_ref[3 * GW:4 * GW, :], preferred_element_type=F32))
    x1 = _layer_norm(ALPHA * x_ref[...] + (1.0 + mod[2:3, :]) * y, lng_ref[...], lnb_ref[...])
    x1_ref[...] = x1
    u2 = (x1 * (1.0 + mod[4:5, :]) + mod[3:4, :]).astype(BF16)
    u2_ref[...] = u2

    logits = lax.dot_general(rw_ref[...], u2, (((1,), (1,)), ((), ())), preferred_element_type=F32)
    scores = _sigmoid(logits)
    biased = scores + rb_ref[...]
    rows = [biased[e:e + 1, :] for e in range(N_EXPERTS)]
    srow = [scores[e:e + 1, :] for e in range(N_EXPERTS)]
    gscore = []
    for g in range(N_GROUPS):
        a = rows[g * GROUP_SIZE:(g + 1) * GROUP_SIZE]
        best = None
        for i in range(GROUP_SIZE):
            for j in range(i + 1, GROUP_SIZE):
                pair_sum = a[i] + a[j]
                best = pair_sum if best is None else jnp.maximum(best, pair_sum)
        gscore.append(best)
    gmax = functools.reduce(jnp.maximum, gscore)
    gsel = jnp.full(gmax.shape, N_GROUPS, jnp.int32)
    for g in reversed(range(N_GROUPS)):
        gsel = jnp.where(gscore[g] == gmax, g, gsel)
    masked = [jnp.where(gsel == (e // GROUP_SIZE), rows[e], NEG_INF) for e in range(N_EXPERTS)]
    m1 = functools.reduce(jnp.maximum, masked)
    i1 = jnp.full(m1.shape, N_EXPERTS, jnp.int32)
    for e in reversed(range(N_EXPERTS)):
        i1 = jnp.where(masked[e] == m1, e, i1)
    masked2 = [jnp.where(i1 == e, NEG_INF, masked[e]) for e in range(N_EXPERTS)]
    m2 = functools.reduce(jnp.maximum, masked2)
    i2 = jnp.full(m2.shape, N_EXPERTS, jnp.int32)
    for e in reversed(range(N_EXPERTS)):
        i2 = jnp.where((masked2[e] == m2) & (i1 != e), e, i2)
    sel1 = functools.reduce(lambda acc, e: acc + jnp.where(i1 == e, srow[e], 0.0), range(N_EXPERTS), 0.0)
    sel2 = functools.reduce(lambda acc, e: acc + jnp.where(i2 == e, srow[e], 0.0), range(N_EXPERTS), 0.0)
    tot = sel1 + sel2
    g1 = sel1 / tot
    g2 = sel2 / tot
    route = jnp.concatenate([i1.astype(F32), i2.astype(F32), g1, g2, jnp.zeros((4, tm), F32)], axis=0)
    rt_ref[0] = route
    rtc_ref[...] = jnp.concatenate([route, jnp.zeros((LANES - 8, tm), F32)], axis=0).T
    sub = lax.broadcasted_iota(jnp.int32, (N_EXPERTS, tm), 0)
    onehot = jnp.where((i1 == sub) | (i2 == sub), 1.0, 0.0)
    lane = lax.broadcasted_iota(jnp.int32, (1, LANES), 1)
    cnt = jnp.zeros((N_EXPERTS, LANES), F32)
    for j in range(tm // DISPATCH_BLOCK):
        cj = jnp.sum(onehot[:, j * DISPATCH_BLOCK:(j + 1) * DISPATCH_BLOCK], axis=-1, keepdims=True)
        cnt = jnp.where(lane == j, cj, cnt)
    cnt_ref[0] = cnt


def _outproj_call(ya, zc, yc, yd, x2, mod_l, wo, cw, cb, lng, lnb, rw, rb, seq):
    n, d = x2.shape
    tm = TM_PROJ
    tps = seq // tm
    hw = 16
    row = lambda w: pl.BlockSpec((tm, w), lambda i: (i, 0))
    full = lambda a: pl.BlockSpec(a.shape, lambda i: (0,) * a.ndim)
    return pl.pallas_call(
        functools.partial(_outproj_kernel, tps),
        grid=(n // tm,),
        in_specs=[row(GW), row(ZC),
                  pl.BlockSpec((hw, ZC), lambda i: (jnp.maximum(i * (tm // hw) - 1, 0), 0)),
                  row(GW), row(GW), row(d),
                  pl.BlockSpec((1, 6, d), lambda i: (i // tps, 0, 0)),
                  full(wo), full(cw), full(cb), full(lng), full(lnb), full(rw), full(rb)],
        out_specs=[row(d), row(d), pl.BlockSpec((1, 8, tm), lambda i: (i, 0, 0)), row(LANES),
                   pl.BlockSpec((1, N_EXPERTS, LANES), lambda i: (i, 0, 0))],
        out_shape=[jax.ShapeDtypeStruct((n, d), F32), jax.ShapeDtypeStruct((n, d), BF16),
                   jax.ShapeDtypeStruct((n // tm, 8, tm), F32), jax.ShapeDtypeStruct((n, LANES), F32),
                   jax.ShapeDtypeStruct((n // tm, N_EXPERTS, LANES), F32)],
        compiler_params=_cparams(("parallel",)),
        name="out_proj_router",
    )(ya, zc, zc, yc, yd, x2, mod_l, wo, cw, cb, lng, lnb, rw, rb)


def _pack_bf16_pairs(x):
    half = x.shape[1] // 2
    xb = x.astype(BF16).astype(F32)
    hi = lax.bitcast_convert_type(xb[:, :half], jnp.uint32)
    lo = lax.bitcast_convert_type(xb[:, half:], jnp.uint32)
    return (hi & jnp.uint32(0xFFFF0000)) | (lo >> 16)


def _unpack_bf16_pairs(w):
    hi = lax.bitcast_convert_type(w & jnp.uint32(0xFFFF0000), F32).astype(BF16)
    lo = lax.bitcast_convert_type(w << 16, F32).astype(BF16)
    return jnp.concatenate([hi, lo], axis=1)


def _segment_pieces(count, fn):
    for b in reversed(range(SEG_PIECE_BITS)):
        size = SEG_ALIGN << b
        shift = SEG_ALIGN.bit_length() - 1 + b
        hi = lax.shift_left(lax.shift_right_logical(count, shift + 1), shift + 1)

        @pl.when((lax.shift_right_logical(count, shift) & 1) == 1)
        def _():
            fn(hi, size)


def _slot_positions(onehot_bf16, strict_lower_bf16, transposed):
    if transposed:
        return jnp.dot(strict_lower_bf16, onehot_bf16, preferred_element_type=F32)
    return lax.dot_general(onehot_bf16, strict_lower_bf16, (((1,), (1,)), ((), ())), preferred_element_type=F32)


def _dispatch_kernel(off_s, pc_s, gb_s, rows_s, gs_s, gl_s, tail_s, u_ref, rt_ref, xs_ref, slab, zbuf, sem):
    blk = pl.program_id(0)
    tb = u_ref.shape[0]
    n_slots = slab.shape[1]
    e1 = rt_ref[0, 0:1, :]
    e2 = rt_ref[0, 1:2, :]
    sub_e = lax.broadcasted_iota(jnp.int32, (N_EXPERTS, tb), 0).astype(F32)
    hit1 = e1 == sub_e
    hit2 = e2 == sub_e
    onehot = jnp.where(hit1 | hit2, 1.0, 0.0).astype(BF16)
    r = lax.broadcasted_iota(jnp.int32, (tb, tb), 0)
    c = lax.broadcasted_iota(jnp.int32, (tb, tb), 1)
    lower = jnp.where(c < r, 1.0, 0.0).astype(BF16)
    rank = _slot_positions(onehot, lower, transposed=False)
    sub1 = lax.broadcasted_iota(jnp.int32, (N_EXPERTS, 1), 0)
    offv = jnp.zeros((N_EXPERTS, 1), F32)
    for e in range(N_EXPERTS):
        offv = jnp.where(sub1 == e, off_s[blk * N_EXPERTS + e].astype(F32), offv)
    pos = offv + rank
    pos1 = jnp.sum(jnp.where(hit1, pos, 0.0), axis=0, keepdims=True)
    pos2 = jnp.sum(jnp.where(hit2, pos, 0.0), axis=0, keepdims=True)
    slot = lax.broadcasted_iota(jnp.int32, (n_slots, tb), 0).astype(F32)
    perm = jnp.where((slot == pos1) | (slot == pos2), 1.0, 0.0).astype(BF16)
    cur = blk % 2
    slab[cur] = _pack_bf16_pairs(jnp.dot(perm, u_ref[...], preferred_element_type=F32))

    @pl.when(blk == 0)
    def _():
        zbuf[...] = jnp.zeros_like(zbuf)

    def start_segments(b, slot):
        for e in range(N_EXPERTS):
            idx = b * N_EXPERTS + e
            src0, dst0 = off_s[idx], gb_s[idx]

            def piece(hi, size, src0=src0, dst0=dst0):
                pltpu.make_async_copy(slab.at[slot, pl.ds(pl.multiple_of(src0 + hi, SEG_ALIGN), size)],
                                      xs_ref.at[pl.ds(pl.multiple_of(dst0 + hi, SEG_ALIGN), size)],
                                      sem.at[slot]).start()
            _segment_pieces(pc_s[idx], piece)

    def wait_segments(b, slot):
        def wait_rows(n_static):
            pltpu.make_async_copy(slab.at[slot, pl.ds(0, n_static)], xs_ref.at[pl.ds(0, n_static)],
                                  sem.at[slot]).wait()
        wait_rows(2 * DISPATCH_BLOCK)
        _segment_pieces(rows_s[b] - 2 * DISPATCH_BLOCK, lambda hi, size: wait_rows(size))

    def fill_copies(slot, go):
        for e in range(N_EXPERTS):
            dst0 = gs_s[e]

            def piece(hi, size, dst0=dst0):
                go(pltpu.make_async_copy(zbuf.at[pl.ds(0, size)],
                                         xs_ref.at[pl.ds(pl.multiple_of(dst0 + hi, SEG_ALIGN), size)], sem.at[slot]))
            _segment_pieces(gl_s[e], piece)

        zrows = zbuf.shape[0]

        def tail_chunk(i, carry):
            go(pltpu.make_async_copy(zbuf, xs_ref.at[pl.ds(pl.multiple_of(tail_s[0] + i * zrows, zrows), zrows)],
                                     sem.at[slot]))
            return carry
        lax.fori_loop(0, tail_s[1], tail_chunk, 0)

    last = pl.num_programs(0) - 1
    start_segments(blk, cur)

    @pl.when(blk == 0)
    def _():
        fill_copies(cur, lambda cp: cp.start())

    @pl.when(blk > 0)
    def _():
        wait_segments(blk - 1, 1 - cur)

    @pl.when(blk == jnp.minimum(1, last))
    def _():
        fill_copies(0, lambda cp: cp.wait())

    @pl.when(blk == last)
    def _():
        wait_segments(blk, cur)


def _dispatch_call(meta, u2, rt, n_rows):
    n, d = u2.shape
    tb = DISPATCH_BLOCK
    per_tile = rt.shape[2] // tb
    grid_spec = pltpu.PrefetchScalarGridSpec(
        num_scalar_prefetch=7,
        grid=(n // tb,),
        in_specs=[pl.BlockSpec((tb, d), lambda i, *_: (i, 0)),
                  pl.BlockSpec((1, 8, tb), lambda i, *_: (i // per_tile, 0, i % per_tile))],
        out_specs=pl.BlockSpec(memory_space=pl.ANY),
        scratch_shapes=[pltpu.VMEM((2, DISPATCH_SLOTS, d // 2), jnp.uint32),
                        pltpu.VMEM((SEG_ALIGN << (SEG_PIECE_BITS - 1), d // 2), jnp.uint32),
                        pltpu.SemaphoreType.DMA((2,))],
    )
    return pl.pallas_call(
        _dispatch_kernel,
        grid_spec=grid_spec,
        out_shape=jax.ShapeDtypeStruct((n_rows, d // 2), jnp.uint32),
        compiler_params=_cparams(("arbitrary",)),
        name="moe_dispatch",
    )(meta["off"], meta["pc"], meta["gbase"], meta["rows"], meta["gap_start"], meta["gap_len"], meta["tail"], u2, rt)


def _experts_kernel(te_s, na_s, x_ref, wg_ref, wu_ref, wd_ref, y_ref):
    active = pl.program_id(0) < na_s[0]

    @pl.when(active)
    def _():
        x = _unpack_bf16_pairs(x_ref[...])
        gate = jnp.dot(x, wg_ref[0], preferred_element_type=F32)
        up = jnp.dot(x, wu_ref[0], preferred_element_type=F32)
        hidden = (gate * _sigmoid(gate) * up).astype(BF16)
        y_ref[...] = _pack_bf16_pairs(jnp.dot(hidden, wd_ref[0], preferred_element_type=F32))

    @pl.when(jnp.logical_not(active))
    def _():
        y_ref[...] = jnp.zeros_like(y_ref)


def _experts_call(meta, xs, wg, wu, wd):
    n_rows = xs.shape[0]
    tm = TM_MOE
    _, d, de = wg.shape
    tile = lambda i, te, na: jnp.minimum(i, na[0] - 1)
    grid_spec = pltpu.PrefetchScalarGridSpec(
        num_scalar_prefetch=2,
        grid=(n_rows // tm,),
        in_specs=[pl.BlockSpec((tm, d // 2), lambda i, te, na: (tile(i, te, na), 0)),
                  pl.BlockSpec((1, d, de), lambda i, te, na: (te[tile(i, te, na)], 0, 0)),
                  pl.BlockSpec((1, d, de), lambda i, te, na: (te[tile(i, te, na)], 0, 0)),
                  pl.BlockSpec((1, de, d), lambda i, te, na: (te[tile(i, te, na)], 0, 0))],
        out_specs=pl.BlockSpec((tm, d // 2), lambda i, te, na: (i, 0)),
    )
    return pl.pallas_call(
        _experts_kernel,
        grid_spec=grid_spec,
        out_shape=jax.ShapeDtypeStruct((n_rows, d // 2), jnp.uint32),
        compiler_params=_cparams(("arbitrary",)),
        name="experts",
    )(meta["tile_expert"], meta["n_active"], xs, wg, wu, wd)


def _combine_kernel(off_s, pc_s, gb_s, rows_s, rtc_ref, x1_ref, mod_ref, lng_ref, lnb_ref, ys_ref, o_ref, slab, sem):
    blk = pl.program_id(0)
    tb = x1_ref.shape[0]
    n_slots = slab.shape[1]
    cur = blk % 2

    def start_segments(b, slot):
        for e in range(N_EXPERTS):
            idx = b * N_EXPERTS + e
            dst0, src0 = off_s[idx], gb_s[idx]

            def piece(hi, size, src0=src0, dst0=dst0):
                pltpu.make_async_copy(ys_ref.at[pl.ds(pl.multiple_of(src0 + hi, SEG_ALIGN), size)],
                                      slab.at[slot, pl.ds(pl.multiple_of(dst0 + hi, SEG_ALIGN), size)],
                                      sem.at[slot]).start()
            _segment_pieces(pc_s[idx], piece)

    def wait_segments(b, slot):
        def wait_rows(n_static):
            pltpu.make_async_copy(ys_ref.at[pl.ds(0, n_static)], slab.at[slot, pl.ds(0, n_static)],
                                  sem.at[slot]).wait()
        wait_rows(2 * DISPATCH_BLOCK)
        _segment_pieces(rows_s[b] - 2 * DISPATCH_BLOCK, lambda hi, size: wait_rows(size))

    @pl.when(blk == 0)
    def _():
        slab[...] = jnp.zeros_like(slab)
        start_segments(blk, cur)

    @pl.when(blk + 1 < pl.num_programs(0))
    def _():
        start_segments(blk + 1, 1 - cur)

    rtc = rtc_ref[...]
    e1, e2, g1, g2 = rtc[:, 0:1], rtc[:, 1:2], rtc[:, 2:3], rtc[:, 3:4]
    lane_e = lax.broadcasted_iota(jnp.int32, (1, LANES), 1).astype(F32)
    hit1 = e1 == lane_e
    hit2 = e2 == lane_e
    onehot = jnp.where(hit1 | hit2, 1.0, 0.0).astype(BF16)
    r = lax.broadcasted_iota(jnp.int32, (tb, tb), 0)
    c = lax.broadcasted_iota(jnp.int32, (tb, tb), 1)
    lower = jnp.where(c < r, 1.0, 0.0).astype(BF16)
    rank = _slot_positions(onehot, lower, transposed=True)
    lane1 = lax.broadcasted_iota(jnp.int32, (1, LANES), 1)
    offv = jnp.zeros((1, LANES), F32)
    for e in range(N_EXPERTS):
        offv = jnp.where(lane1 == e, off_s[blk * N_EXPERTS + e].astype(F32), offv)
    pos = offv + rank
    pos1 = jnp.sum(jnp.where(hit1, pos, 0.0), axis=-1, keepdims=True)
    pos2 = jnp.sum(jnp.where(hit2, pos, 0.0), axis=-1, keepdims=True)
    slot = lax.broadcasted_iota(jnp.int32, (1, n_slots), 1).astype(F32)
    sel1 = jnp.where(slot == pos1, 1.0, 0.0).astype(BF16)
    sel2 = jnp.where(slot == pos2, 1.0, 0.0).astype(BF16)

    wait_segments(blk, cur)
    ys = _unpack_bf16_pairs(slab[cur])
    y = (g1 * jnp.dot(sel1, ys, preferred_element_type=F32) + g2 * jnp.dot(sel2, ys, preferred_element_type=F32))
    mod = mod_ref[0]
    o_ref[...] = _layer_norm(ALPHA * x1_ref[...] + (1.0 + mod[5:6, :]) * y, lng_ref[...], lnb_ref[...])


def _combine_call(meta, ys, rtc, x1, mod_l, lng, lnb, seq):
    n, d = x1.shape
    tb = DISPATCH_BLOCK
    bps = seq // tb
    grid_spec = pltpu.PrefetchScalarGridSpec(
        num_scalar_prefetch=4,
        grid=(n // tb,),
        in_specs=[pl.BlockSpec((tb, LANES), lambda i, *_: (i, 0)),
                  pl.BlockSpec((tb, d), lambda i, *_: (i, 0)),
                  pl.BlockSpec((1, 6, d), lambda i, *_: (i // bps, 0, 0)),
                  pl.BlockSpec((1, d), lambda i, *_: (0, 0)),
                  pl.BlockSpec((1, d), lambda i, *_: (0, 0)),
                  pl.BlockSpec(memory_space=pl.ANY)],
        out_specs=pl.BlockSpec((tb, d), lambda i, *_: (i, 0)),
        scratch_shapes=[pltpu.VMEM((2, DISPATCH_SLOTS, d // 2), jnp.uint32), pltpu.SemaphoreType.DMA((2,))],
    )
    return pl.pallas_call(
        _combine_kernel,
        grid_spec=grid_spec,
        out_shape=jax.ShapeDtypeStruct((n, d), F32),
        compiler_params=_cparams(("arbitrary",)),
        name="moe_combine_ln2",
    )(meta["off"], meta["pc"], meta["gbase"], meta["rows"], rtc, x1, mod_l, lng, lnb, ys)


def _sorted_rows_bound(n_tokens):
    n_blocks = n_tokens // DISPATCH_BLOCK
    worst = 2 * n_tokens + n_blocks * N_EXPERTS * (SEG_ALIGN - 1) + N_EXPERTS * (TM_MOE - 1)
    return -(-worst // TM_MOE) * TM_MOE


def _route_meta(cnt, n_rows):
    pc = (cnt + SEG_ALIGN - 1) // SEG_ALIGN * SEG_ALIGN
    off = jnp.cumsum(pc, axis=1) - pc
    tot = pc.sum(axis=0)
    reg = (tot + TM_MOE - 1) // TM_MOE * TM_MOE
    reg_end = jnp.cumsum(reg)
    ebase = reg_end - reg
    gbase = ebase[None, :] + jnp.cumsum(pc, axis=0) - pc
    tiles = jnp.arange(n_rows // TM_MOE, dtype=jnp.int32)
    tile_expert = jnp.minimum(jnp.sum(tiles[:, None] >= (reg_end // TM_MOE)[None, :], axis=1), N_EXPERTS - 1)
    i32 = lambda a: a.astype(jnp.int32).reshape(-1)
    used = reg_end[-1:]
    tail_chunk = SEG_ALIGN << (SEG_PIECE_BITS - 1)
    return {"off": i32(off), "pc": i32(pc), "gbase": i32(gbase), "rows": i32(pc.sum(axis=1)),
            "gap_start": i32(ebase + tot),
            "gap_len": i32(reg - tot), "tail": i32(jnp.concatenate([used, (n_rows - used) // tail_chunk])),
            "tile_expert": i32(tile_expert), "n_active": i32(used // TM_MOE)}


def _rearranged_w_in(w_in_l):
    d = w_in_l.shape[0]
    o_gate, o_cb, o_ga = ZM, ZM + 2 * HEADS, ZM + 2 * HEADS + ZC + ZQ + ZK + ZVP + ZG
    cols = [w_in_l[:, 0:ZM], w_in_l[:, o_cb:o_ga],
            w_in_l[:, o_gate:o_gate + 2 * HEADS], w_in_l[:, o_ga:o_ga + GLA_RANK],
            jnp.zeros((d, ZS - 2 * HEADS - GLA_RANK), w_in_l.dtype)]
    return jnp.concatenate(cols, axis=1).astype(BF16)


def _rope_tables(positions):
    inv_freq = ROPE_THETA ** (-jnp.arange(0, ROPE_DIM, 2, dtype=F32) / ROPE_DIM)
    ang = positions.astype(F32).reshape(-1, 1) * inv_freq
    cos, sin = jnp.cos(ang), jnp.sin(ang)
    d = np.arange(ZQ) % DIFF_DIM
    half = ROPE_DIM // 2
    idx = d % half
    cosf = jnp.where(d < ROPE_DIM, cos[:, idx], 1.0)
    sinf = jnp.where(d < half, -sin[:, idx], jnp.where(d < ROPE_DIM, sin[:, idx], 0.0))
    return cosf, sinf


def _pad_lanes(v, width=LANES):
    v = v.reshape(1, -1)
    return jnp.pad(v, ((0, 0), (0, width - v.shape[1])))


def kernel(x, c, positions, w_ada, b_ada, w_in, w_out, mlstm_b_i, mlstm_b_f, mlstm_norm_g, conv_w, conv_b, diff_lambda_q1, diff_lambda_k1, diff_lambda_q2, diff_lambda_k2, diff_norm_g, gla_w_a2, gla_b_a, gla_norm_g, ln1_g, ln1_b, ln2_g, ln2_b, router_w, router_bias, exp_w_gate, exp_w_up, exp_w_down):
    bsz, seq, d = x.shape
    depth = w_in.shape[0]
    n = bsz * seq
    x2 = x.reshape(n, d)
    mod = _mod_call(c, w_ada, b_ada)
    cosf, sinf = _rope_tables(positions)
    rw_t = router_w.T.astype(BF16)
    rb = router_bias.reshape(N_EXPERTS, 1).astype(F32)
    n_sorted = _sorted_rows_bound(n)
    for l in range(depth):
        lam_init = 0.8 - 0.6 * math.exp(-0.3 * l)
        w_l = _rearranged_w_in(w_in[l])
        zm, zc, zq, zk, zvt, zg, zs = _inproj_call(x2, mod[l], w_l, cosf, sinf, seq)
        bif = _pad_lanes(jnp.concatenate([mlstm_b_i[l], mlstm_b_f[l]]))
        ya = _mlstm_call(zm, zs, bif, mlstm_norm_g[l].reshape(1, -1), bsz, seq)
        lamp = jnp.concatenate([_pad_lanes(diff_lambda_q1[l]), _pad_lanes(diff_lambda_k1[l]),
                                _pad_lanes(diff_lambda_q2[l]), _pad_lanes(diff_lambda_k2[l])], axis=0)
        yc = _attn_call(zq, zk, zvt, lamp, diff_norm_g[l].reshape(1, -1), lam_init, bsz, seq)
        wa = jnp.pad(gla_w_a2[l], ((2 * HEADS, ZS - 2 * HEADS - GLA_RANK), (0, 0))).astype(BF16)
        yd = _gla_call(zg, zs, wa, gla_b_a[l].reshape(1, -1), gla_norm_g[l].reshape(1, -1), bsz, seq)
        x1, u2, rt, rtc, cnt = _outproj_call(ya, zc, yc, yd, x2, mod[l], w_out[l].astype(BF16), conv_w[l],
                                             conv_b[l].reshape(1, -1), ln1_g[l].reshape(1, -1),
                                             ln1_b[l].reshape(1, -1), rw_t, rb, seq)
        per_tile = TM_PROJ // DISPATCH_BLOCK
        cnt_blk = cnt[:, :, :per_tile].transpose(0, 2, 1).reshape(-1, N_EXPERTS).astype(jnp.int32)
        meta = _route_meta(cnt_blk, n_sorted)
        xs = _dispatch_call(meta, u2, rt, n_sorted)
        ys = _experts_call(meta, xs, exp_w_gate[l].astype(BF16), exp_w_up[l].astype(BF16),
                           exp_w_down[l].astype(BF16))
        x2 = _combine_call(meta, ys, rtc, x1, mod[l], ln2_g[l].reshape(1, -1), ln2_b[l].reshape(1, -1), seq)
    return x2.reshape(bsz, seq, d)
```

```python
import functools
import math

import numpy as np
import jax
import jax.numpy as jnp
from jax import lax
from jax.experimental import pallas as pl
from jax.experimental.pallas import tpu as pltpu

F32 = jnp.float32
BF16 = jnp.bfloat16

DEPTH = 4
HEADS = 4
GW = 256
HEAD_DIM = 64
DIFF_DIM = 32
ROPE_DIM = 8
ROPE_THETA = 500000.0
CONV_WIDTH = 3
GLA_RANK = 16
GLA_TAU = 16.0
CHUNK = 64
GLA_CHUNK = 64
N_EXPERTS = 16
N_GROUPS = 4
GROUP_SIZE = 4
ALPHA = (2 * DEPTH) ** 0.25
EPS = 1e-6
LANES = 128
SUBLANES = 8
NEG_INF = float("-inf")
LOG2_E = math.log2(math.e)

ZM, ZC, ZQ, ZK, ZVP, ZG, ZS = 1024, 768, 256, 256, 256, 1024, 128
ZV = HEADS * LANES
Z_OFFS = np.cumsum([0, ZM, ZC, ZQ, ZK, ZVP, ZG, ZS])
Z_TOTAL = int(Z_OFFS[-1])

TM_PROJ = 512
TB_MLSTM = 256
TB_GLA = 1024
CUM_SPAN = 128
TQ_ATTN = 512
TM_MOE = 512
DISPATCH_BLOCK = 256
SEG_ALIGN = 8
SEG_PIECE_BITS = 6
DISPATCH_SLOTS = 2 * DISPATCH_BLOCK + N_EXPERTS * SEG_ALIGN
assert SEG_ALIGN << (SEG_PIECE_BITS - 1) >= DISPATCH_BLOCK and SEG_ALIGN << SEG_PIECE_BITS >= TM_MOE
VMEM_LIMIT = 56 * 1024 * 1024


def _cparams(sem):
    return pltpu.CompilerParams(dimension_semantics=sem, vmem_limit_bytes=VMEM_LIMIT)


def _log_sigmoid(x):
    return jnp.minimum(x, 0.0) - jnp.log(1.0 + jnp.exp(-jnp.abs(x)))


def _sigmoid(x):
    return 1.0 / (1.0 + jnp.exp(-x))


def _layer_norm(v, g, b):
    mu = jnp.mean(v, axis=-1, keepdims=True)
    d = v - mu
    var = jnp.mean(d * d, axis=-1, keepdims=True)
    return d * lax.rsqrt(var + EPS) * g + b


def _mod_kernel(c_ref, w_ref, b_ref, o_ref):
    c = c_ref[...]
    c_act = (c * _sigmoid(c)).astype(BF16)
    o_ref[0, 0] = jnp.dot(c_act, w_ref[0].astype(BF16), preferred_element_type=F32) + b_ref[0, 0]


def _mod_call(c, w_ada, b_ada):
    depth, d, _ = w_ada.shape
    bsz = c.shape[0]
    b3 = b_ada.reshape(depth, 6, 1, d)
    out = pl.pallas_call(
        _mod_kernel,
        grid=(depth, 6),
        in_specs=[pl.BlockSpec((bsz, d), lambda l, j: (0, 0)),
                  pl.BlockSpec((1, d, d), lambda l, j: (l, 0, j)),
                  pl.BlockSpec((1, 1, 1, d), lambda l, j: (l, j, 0, 0))],
        out_specs=pl.BlockSpec((1, 1, bsz, d), lambda l, j: (l, j, 0, 0)),
        out_shape=jax.ShapeDtypeStruct((depth, 6, bsz, d), F32),
        compiler_params=_cparams(("arbitrary", "arbitrary")),
        name="adaln_mod",
    )(c, w_ada, b3)
    return out.transpose(0, 2, 1, 3)


def _inproj_kernel(x_ref, mod_ref, w_ref, cos_ref, sin_ref, zm_ref, zc_ref, zq_ref, zk_ref, zvt_ref, zg_ref,
                   zs_ref):
    mod = mod_ref[0]
    u = (x_ref[...] * (1.0 + mod[1:2, :]) + mod[0:1, :]).astype(BF16)

    def proj(idx):
        return jnp.dot(u, w_ref[:, int(Z_OFFS[idx]):int(Z_OFFS[idx + 1])], preferred_element_type=F32)

    zm_ref[...] = proj(0).astype(BF16)
    zc_ref[...] = proj(1).astype(BF16)

    lane = lax.broadcasted_iota(jnp.int32, (1, ZQ), 1)
    first_half = (lane % ROPE_DIM) < (ROPE_DIM // 2)
    cosf = cos_ref[...]
    sinf = sin_ref[...]

    def rope(r):
        swapped = jnp.where(first_half, pltpu.roll(r, ZQ - ROPE_DIM // 2, 1), pltpu.roll(r, ROPE_DIM // 2, 1))
        return r * cosf + swapped * sinf

    zq_ref[...] = (rope(proj(2)) * (LOG2_E * DIFF_DIM ** -0.5)).astype(BF16)
    zk_ref[...] = rope(proj(3)).astype(BF16)
    vt = proj(4).T
    tm = vt.shape[1]
    ones_rows = jnp.where(lax.broadcasted_iota(jnp.int32, (LANES - HEAD_DIM, tm), 0) == 0, 1.0, 0.0)
    zvt_ref[...] = jnp.concatenate([blk for h in range(HEADS)
                                    for blk in (vt[h * HEAD_DIM:(h + 1) * HEAD_DIM], ones_rows)], axis=0).astype(BF16)
    zg_ref[...] = proj(5).astype(BF16)
    zs_ref[...] = proj(6)


def _inproj_call(x2, mod_l, w_l, cosf, sinf, seq):
    n, d = x2.shape
    tm = TM_PROJ
    tps = seq // tm
    row = lambda w: pl.BlockSpec((tm, w), lambda i: (i, 0))
    outs = [(ZM, BF16), (ZC, BF16), (ZQ, BF16), (ZK, BF16), (ZV, BF16), (ZG, BF16), (ZS, F32)]
    return pl.pallas_call(
        _inproj_kernel,
        grid=(n // tm,),
        in_specs=[row(d),
                  pl.BlockSpec((1, 6, d), lambda i: (i // tps, 0, 0)),
                  pl.BlockSpec((d, Z_TOTAL), lambda i: (0, 0)),
                  row(ZQ), row(ZQ)],
        out_specs=[pl.BlockSpec((ZV, tm), lambda i: (0, i)) if k == 4 else row(w) for k, (w, _) in enumerate(outs)],
        out_shape=[jax.ShapeDtypeStruct((ZV, n) if k == 4 else (n, w), dt) for k, (w, dt) in enumerate(outs)],
        compiler_params=_cparams(("parallel",)),
        name="in_proj",
    )(x2, mod_l, w_l, cosf, sinf)


NT_DIMS = (((1,), (1,)), ((), ()))
TN_DIMS = (((0,), (0,)), ((), ()))


def _head_of(idx):
    return idx // HEAD_DIM


def _cumsum_matrix(span, chunk):
    r = lax.broadcasted_iota(jnp.int32, (span, span), 0)
    c = lax.broadcasted_iota(jnp.int32, (span, span), 1)
    return jnp.where(((r // chunk) == (c // chunk)) & (c <= r), 1.0, 0.0).astype(BF16)


def _bf16_terms(x, n_terms):
    terms, rest = [], x
    for _ in range(n_terms):
        t = rest.astype(BF16)
        terms.append(t)
        rest = rest - t.astype(F32)
    return terms


def _exact_matmul(lhs_f32, rhs01, n_terms=3):
    return functools.reduce(lambda a, b: a + b, [jnp.dot(t, rhs01, preferred_element_type=F32)
                                                 for t in reversed(_bf16_terms(lhs_f32, n_terms))])


def _chunk_cumsum(x, chunk):
    cm = _cumsum_matrix(CUM_SPAN, chunk)
    outs = []
    for i in range(x.shape[0] // CUM_SPAN):
        terms = _bf16_terms(x[i * CUM_SPAN:(i + 1) * CUM_SPAN], 3)
        outs.append(functools.reduce(lambda a, b: a + b,
                                     [jnp.dot(cm, t, preferred_element_type=F32) for t in reversed(terms)]))
    return jnp.concatenate(outs, axis=0)


def _rows_broadcast(x, chunk, row):
    n = x.shape[0] // chunk
    return jnp.concatenate([jnp.broadcast_to(x[c * chunk + row:c * chunk + row + 1, :], (chunk, x.shape[1]))
                            for c in range(n)], axis=0)


def _head_masks(chunk):
    lane_head = _head_of(lax.broadcasted_iota(jnp.int32, (1, GW), 1))
    row_head = lax.broadcasted_iota(jnp.int32, (HEADS * chunk, 1), 0) // chunk
    stack_keep = row_head == lane_head
    rr = lax.broadcasted_iota(jnp.int32, (HEADS * chunk, chunk), 0) % chunk
    cc = lax.broadcasted_iota(jnp.int32, (HEADS * chunk, chunk), 1)
    block_diag = (_head_of(lax.broadcasted_iota(jnp.int32, (GW, GW), 0))
                  == _head_of(lax.broadcasted_iota(jnp.int32, (GW, GW), 1)))
    return stack_keep, cc <= rr, block_diag


def _unstack_heads(x4, keep, chunk):
    x4 = jnp.where(keep, x4, 0.0)
    return functools.reduce(lambda a, b: a + b, [x4[h * chunk:(h + 1) * chunk] for h in range(HEADS)])


def _head_rms_norm(x, block_diag):
    seg = jnp.where(block_diag, 1.0, 0.0).astype(BF16)
    return x * lax.rsqrt(_exact_matmul(x * x, seg, 2) * (1.0 / HEAD_DIM) + EPS)


def _mlstm_kernel(zm_ref, zs_ref, bif_ref, g_ref, o_ref, ct_scr, nm_scr, ni_scr, up_scr):
    @pl.when(pl.program_id(1) == 0)
    def _():
        ct_scr[...] = jnp.zeros_like(ct_scr)
        nm_scr[...] = jnp.zeros_like(nm_scr)

    tb = zm_ref.shape[0]
    L = CHUNK
    nc = tb // L
    lane = lax.broadcasted_iota(jnp.int32, (1, LANES), 1)
    graw = zs_ref[...] + bif_ref[...]
    gp = jnp.where(lane < HEADS, graw, jnp.where(lane < 2 * HEADS, _log_sigmoid(graw), 0.0))
    cum = _chunk_cumsum(gp, L)

    def expand(x, first_lane):
        sel = jnp.where((lax.broadcasted_iota(jnp.int32, (LANES, GW), 0) - first_lane)
                        == _head_of(lax.broadcasted_iota(jnp.int32, (LANES, GW), 1)), 1.0, 0.0).astype(BF16)
        return _exact_matmul(x, sel)

    bx = expand(cum, HEADS)
    lix = expand(gp, 0)
    bex = _rows_broadcast(bx, L, L - 1)
    dkx = bex - bx + lix
    dkmax = [jnp.max(dkx[c * L:(c + 1) * L], axis=0, keepdims=True) for c in range(nc)]
    kw = (zm_ref[:, GW:2 * GW].astype(F32)
          * jnp.exp(dkx - jnp.concatenate([jnp.broadcast_to(d, (L, GW)) for d in dkmax], axis=0))).astype(BF16)
    rowv_t = (cum - pltpu.roll(gp, HEADS, 1)).T
    stack_keep, causal4, block_diag = _head_masks(L)
    stack_keep2 = jnp.concatenate([stack_keep, stack_keep], axis=1)
    ones_blk = jnp.ones((L, GW), BF16)

    dmaxx = []
    for c in range(nc):
        rs = slice(c * L, (c + 1) * L)
        qc = zm_ref[rs, 0:GW] * (HEAD_DIM ** -0.5)
        kc = zm_ref[rs, GW:2 * GW]
        vc = zm_ref[rs, 2 * GW:3 * GW]
        q4 = jnp.where(stack_keep, jnp.concatenate([qc] * HEADS, axis=0), jnp.zeros((), BF16))
        s4 = lax.dot_general(q4, kc, NT_DIMS, preferred_element_type=F32)
        bcol4 = jnp.concatenate([cum[rs, HEADS + h:HEADS + h + 1] for h in range(HEADS)], axis=0)
        brow4 = jnp.concatenate([jnp.broadcast_to(rowv_t[HEADS + h:HEADS + h + 1, rs], (L, L))
                                 for h in range(HEADS)], axis=0)
        d4 = jnp.where(causal4, bcol4 - brow4, NEG_INF)
        dmax4 = jnp.max(d4, axis=-1, keepdims=True)
        p4 = (s4 * jnp.exp(d4 - dmax4)).astype(BF16)
        i4 = jnp.dot(p4, jnp.concatenate([vc, ones_blk], axis=1), preferred_element_type=F32)
        ni_scr[rs, :] = _unstack_heads(i4, stack_keep2, L)
        dmaxx.append(_unstack_heads(jnp.broadcast_to(dmax4, (HEADS * L, GW)), stack_keep, L))
        up = lax.dot_general(vc, kw[rs], TN_DIMS, preferred_element_type=F32)
        up_scr[c] = jnp.where(block_diag, up, 0.0)

    ct = ct_scr[...]
    n_row = nm_scr[0:1, :]
    m_prev = nm_scr[1:2, :]
    hv_rows = []
    for c in range(nc):
        rs = slice(c * L, (c + 1) * L)
        qc = zm_ref[rs, 0:GW] * (HEAD_DIM ** -0.5)
        nmat = jnp.where(block_diag, jnp.broadcast_to(n_row, (GW, GW)), 0.0).astype(BF16)
        num_inter = lax.dot_general(qc, ct.astype(BF16), NT_DIMS, preferred_element_type=F32)
        den_inter = lax.dot_general(qc, nmat, NT_DIMS, preferred_element_type=F32)
        inter = bx[rs] + m_prev
        m_row = jnp.maximum(inter, dmaxx[c])
        e1 = jnp.exp(dmaxx[c] - m_row)
        e2 = jnp.exp(inter - m_row)
        num = e1 * ni_scr[rs, 0:GW] + e2 * num_inter
        den = e1 * ni_scr[rs, GW:2 * GW] + e2 * den_inter
        hv_rows.append(num / jnp.maximum(jnp.abs(den), jnp.exp(-m_row)))
        b_end = bex[c * L:c * L + 1, :]
        m_new = jnp.maximum(b_end + m_prev, dkmax[c])
        decay = jnp.exp(b_end + m_prev - m_new)
        f2 = jnp.exp(dkmax[c] - m_new)
        ct = ct * decay + up_scr[c] * f2
        n_row = n_row * decay + f2 * jnp.sum(kw[rs].astype(F32), axis=0, keepdims=True)
        m_prev = m_new
    ct_scr[...] = ct
    nm_scr[0:1, :] = n_row
    nm_scr[1:2, :] = m_prev

    hn = _head_rms_norm(jnp.concatenate(hv_rows, axis=0), block_diag)
    o_ref[...] = (hn * g_ref[...] * _sigmoid(zm_ref[:, 3 * GW:4 * GW].astype(F32))).astype(BF16)


def _mlstm_call(zm, zs, bif, gain, bsz, seq):
    tb = TB_MLSTM
    tps = seq // tb
    return pl.pallas_call(
        _mlstm_kernel,
        grid=(bsz, tps),
        in_specs=[pl.BlockSpec((tb, ZM), lambda b, t: (b * tps + t, 0)),
                  pl.BlockSpec((tb, ZS), lambda b, t: (b * tps + t, 0)),
                  pl.BlockSpec((1, LANES), lambda b, t: (0, 0)),
                  pl.BlockSpec((1, GW), lambda b, t: (0, 0))],
        out_specs=pl.BlockSpec((tb, GW), lambda b, t: (b * tps + t, 0)),
        out_shape=jax.ShapeDtypeStruct((bsz * seq, GW), BF16),
        scratch_shapes=[pltpu.VMEM((GW, GW), F32),
                        pltpu.VMEM((8, GW), F32),
                        pltpu.VMEM((tb, 2 * GW), F32),
                        pltpu.VMEM((tb // CHUNK, GW, GW), F32)],
        compiler_params=_cparams(("parallel", "arbitrary")),
        name="mlstm",
    )(zm, zs, bif, jnp.tile(gain, (1, HEADS)))


def _gla_kernel(zg_ref, zs_ref, wa_ref, ba_ref, g_ref, o_ref, st_scr, oi_scr, up_scr):
    @pl.when(pl.program_id(1) == 0)
    def _():
        st_scr[...] = jnp.zeros_like(st_scr)

    tb = zg_ref.shape[0]
    lg = GLA_CHUNK
    nc = tb // lg
    la = _log_sigmoid(jnp.dot(zs_ref[...].astype(BF16), wa_ref[...], preferred_element_type=F32)
                      + ba_ref[...]) * (1.0 / GLA_TAU)
    bc = _chunk_cumsum(la, lg)
    be = _rows_broadcast(bc, lg, lg - 1)
    bm = _rows_broadcast(bc, lg, lg // 2)
    q = zg_ref[:, 0:GW].astype(F32) * (HEAD_DIM ** -0.5)
    k = zg_ref[:, GW:2 * GW].astype(F32)
    qm = (q * jnp.exp(bc - bm)).astype(BF16)
    km = (k * jnp.exp(bm - bc)).astype(BF16)
    qt = (q * jnp.exp(bc)).astype(BF16)
    ks = (k * jnp.exp(be - bc)).astype(BF16)
    dec = jnp.exp(be)
    stack_keep, causal4, block_diag = _head_masks(lg)

    for c in range(nc):
        rs = slice(c * lg, (c + 1) * lg)
        vc = zg_ref[rs, 2 * GW:3 * GW]
        q4 = jnp.where(stack_keep, jnp.concatenate([qm[rs]] * HEADS, axis=0), jnp.zeros((), BF16))
        att4 = lax.dot_general(q4, km[rs], NT_DIMS, preferred_element_type=F32)
        att4 = jnp.where(causal4, att4, 0.0).astype(BF16)
        oi_scr[rs, :] = _unstack_heads(jnp.dot(att4, vc, preferred_element_type=F32), stack_keep, lg)
        up = lax.dot_general(vc, ks[rs], TN_DIMS, preferred_element_type=F32)
        up_scr[c] = jnp.where(block_diag, up, 0.0)

    st = st_scr[...]
    inter = []
    for c in range(nc):
        rs = slice(c * lg, (c + 1) * lg)
        inter.append(lax.dot_general(qt[rs], st.astype(BF16), NT_DIMS, preferred_element_type=F32))
        st = st * dec[c * lg:c * lg + 1, :] + up_scr[c]
    st_scr[...] = st

    on = _head_rms_norm(oi_scr[...] + jnp.concatenate(inter, axis=0), block_diag)
    r = zg_ref[:, 3 * GW:4 * GW].astype(F32)
    o_ref[...] = (on * g_ref[...] * (r * _sigmoid(r))).astype(BF16)


def _gla_call(zg, zs, wa, ba, gain, bsz, seq):
    tb = TB_GLA
    tps = seq // tb
    return pl.pallas_call(
        _gla_kernel,
        grid=(bsz, tps),
        in_specs=[pl.BlockSpec((tb, ZG), lambda b, t: (b * tps + t, 0)),
                  pl.BlockSpec((tb, ZS), lambda b, t: (b * tps + t, 0)),
                  pl.BlockSpec((ZS, GW), lambda b, t: (0, 0)),
                  pl.BlockSpec((1, GW), lambda b, t: (0, 0)),
                  pl.BlockSpec((1, GW), lambda b, t: (0, 0))],
        out_specs=pl.BlockSpec((tb, GW), lambda b, t: (b * tps + t, 0)),
        out_shape=jax.ShapeDtypeStruct((bsz * seq, GW), BF16),
        scratch_shapes=[pltpu.VMEM((GW, GW), F32),
                        pltpu.VMEM((tb, GW), F32),
                        pltpu.VMEM((tb // GLA_CHUNK, GW, GW), F32)],
        compiler_params=_cparams(("parallel", "arbitrary")),
        name="gla",
    )(zg, zs, wa, ba, jnp.tile(gain, (1, HEADS)))


def _attn_kernel(lam_init, q_ref, k_ref, vt_ref, lamp_ref, g_ref, o_ref, qm_scr, m_scr, acc_scr, st_scr):
    tq = q_ref.shape[0]
    qi = pl.program_id(1)
    n_hc = 2 * HEADS
    lane = lax.broadcasted_iota(jnp.int32, (1, LANES), 1)
    for hc in range(n_hc):
        h, c = divmod(hc, 2)
        pair = h // 2
        lo = (h % 2) * HEAD_DIM + c * DIFF_DIM
        qp = q_ref[:, pair * LANES:(pair + 1) * LANES]
        qm_scr[hc] = jnp.where((lane >= lo) & (lane < lo + DIFF_DIM), qp, jnp.zeros_like(qp))
    m_scr[...] = jnp.full(m_scr.shape, NEG_INF, F32)
    acc_scr[...] = jnp.zeros_like(acc_scr)

    def sublane_allmax(x):
        for shift in (4, 2, 1):
            x = jnp.maximum(x, pltpu.roll(x, shift, 0))
        return x

    def kv_block(j, keep_t):
        key0 = pl.multiple_of(j * tq, tq)
        for hc in range(n_hc):
            pair = (hc // 2) // 2
            kb = k_ref[pl.ds(key0, tq), pair * LANES:(pair + 1) * LANES]
            st_scr[hc] = lax.dot_general(kb, qm_scr[hc], NT_DIMS, preferred_element_type=F32)
        for hc in range(n_hc):
            h = hc // 2
            vt = vt_ref[h * LANES:(h + 1) * LANES, pl.ds(key0, tq)]
            st = st_scr[hc]
            if keep_t is not None:
                st = jnp.where(keep_t, st, NEG_INF)
            st3 = st.reshape(tq // SUBLANES, SUBLANES, tq)
            m_old = m_scr[hc]
            m_new = jnp.maximum(m_old, sublane_allmax(jnp.max(st3, axis=0)))
            alpha = jnp.exp2(m_old - m_new)
            p = jnp.exp2(st3 - m_new[None]).reshape(tq, tq).astype(BF16)
            acc = acc_scr[hc].reshape(LANES // SUBLANES, SUBLANES, tq) * alpha[None]
            acc_scr[hc] = acc.reshape(LANES, tq) + jnp.dot(vt, p, preferred_element_type=F32)
            m_scr[hc] = m_new

    def full_block(j, carry):
        kv_block(j, None)
        return carry

    lax.fori_loop(0, qi, full_block, 0)
    key_idx = lax.broadcasted_iota(jnp.int32, (tq, tq), 0)
    query_idx = lax.broadcasted_iota(jnp.int32, (tq, tq), 1)
    kv_block(qi, key_idx <= query_idx)

    lamp = lamp_ref[...]
    lam = (jnp.exp(jnp.sum(lamp[0:1, :] * lamp[1:2, :], axis=-1, keepdims=True))
           - jnp.exp(jnp.sum(lamp[2:3, :] * lamp[3:4, :], axis=-1, keepdims=True)) + lam_init)
    gain = g_ref[...] * (1.0 - lam_init)
    outs = []
    for h in range(HEADS):
        a1 = acc_scr[2 * h]
        a2 = acc_scr[2 * h + 1]
        o = a1[:HEAD_DIM] / a1[HEAD_DIM:HEAD_DIM + 1] - lam * (a2[:HEAD_DIM] / a2[HEAD_DIM:HEAD_DIM + 1])
        on = o * lax.rsqrt(jnp.mean(o * o, axis=0, keepdims=True) + EPS)
        outs.append(on.T * gain)
    o_ref[...] = jnp.concatenate(outs, axis=1).astype(BF16)


def _attn_call(zq, zk, zvt, lamp, gain, lam_init, bsz, seq):
    tq = TQ_ATTN
    nq = seq // tq
    return pl.pallas_call(
        functools.partial(_attn_kernel, lam_init),
        grid=(bsz, nq),
        in_specs=[pl.BlockSpec((tq, ZQ), lambda b, i: (b * nq + i, 0)),
                  pl.BlockSpec((seq, ZK), lambda b, i: (b, 0)),
                  pl.BlockSpec((ZV, seq), lambda b, i: (0, b)),
                  pl.BlockSpec((4, LANES), lambda b, i: (0, 0)),
                  pl.BlockSpec((1, HEAD_DIM), lambda b, i: (0, 0))],
        out_specs=pl.BlockSpec((tq, GW), lambda b, i: (b * nq + i, 0)),
        out_shape=jax.ShapeDtypeStruct((bsz * seq, GW), BF16),
        scratch_shapes=[pltpu.VMEM((2 * HEADS, tq, LANES), BF16),
                        pltpu.VMEM((2 * HEADS, SUBLANES, tq), F32),
                        pltpu.VMEM((2 * HEADS, LANES, tq), F32),
                        pltpu.VMEM((2 * HEADS, tq, tq), F32)],
        compiler_params=_cparams(("parallel", "arbitrary")),
        name="diff_attn",
    )(zq, zk, zvt, lamp, gain)


def _outproj_kernel(tiles_per_seq, ya_ref, zc_ref, halo_ref, yc_ref, yd_ref, x_ref, mod_ref, wo_ref, cw_ref,
                    cb_ref, lng_ref, lnb_ref, rw_ref, rb_ref, x1_ref, u2_ref, rt_ref, rtc_ref, cnt_ref):
    tm = x_ref.shape[0]
    mod = mod_ref[0]
    zc = zc_ref[...].astype(F32)
    halo = halo_ref[...].astype(F32)
    first = (pl.program_id(0) % tiles_per_seq) == 0
    s_cur = zc[:, GW:2 * GW] * zc[:, 2 * GW:3 * GW]
    s_halo = jnp.where(first, 0.0, halo[:, GW:2 * GW] * halo[:, 2 * GW:3 * GW])
    s_ext = jnp.concatenate([s_halo, s_cur], axis=0)
    hw = halo.shape[0]
    cw = cw_ref[...]
    conv = cb_ref[...]
    for tap in range(CONV_WIDTH):
        start = hw - (CONV_WIDTH - 1) + tap
        conv = conv + cw[tap:tap + 1, :] * s_ext[start:start + tm, :]
    yb = (zc[:, 0:GW] * conv).astype(BF16)

    y = jnp.dot(jnp.concatenate([ya_ref[...], yb, yc_ref[...], yd_ref[...]], axis=1), wo_ref[...],
                preferred_element_type=F32)
    x1 = _layer_norm(ALPHA * x_ref[...] + (1.0 + mod[2:3, :]) * y, lng_ref[...], lnb_ref[...])
    x1_ref[...] = x1
    u2 = (x1 * (1.0 + mod[4:5, :]) + mod[3:4, :]).astype(BF16)
    u2_ref[...] = u2

    logits = lax.dot_general(rw_ref[...], u2, (((1,), (1,)), ((), ())), preferred_element_type=F32)
    scores = _sigmoid(logits)
    biased = scores + rb_ref[...]
    rows = [biased[e:e + 1, :] for e in range(N_EXPERTS)]
    srow = [scores[e:e + 1, :] for e in range(N_EXPERTS)]
    gscore = []
    for g in range(N_GROUPS):
        a = rows[g * GROUP_SIZE:(g + 1) * GROUP_SIZE]
        best = None
        for i in range(GROUP_SIZE):
            for j in range(i + 1, GROUP_SIZE):
                pair_sum = a[i] + a[j]
                best = pair_sum if best is None else jnp.maximum(best, pair_sum)
        gscore.append(best)
    gmax = functools.reduce(jnp.maximum, gscore)
    gsel = jnp.full(gmax.shape, N_GROUPS, jnp.int32)
    for g in reversed(range(N_GROUPS)):
        gsel = jnp.where(gscore[g] == gmax, g, gsel)
    masked = [jnp.where(gsel == (e // GROUP_SIZE), rows[e], NEG_INF) for e in range(N_EXPERTS)]
    m1 = functools.reduce(jnp.maximum, masked)
    i1 = jnp.full(m1.shape, N_EXPERTS, jnp.int32)
    for e in reversed(range(N_EXPERTS)):
        i1 = jnp.where(masked[e] == m1, e, i1)
    masked2 = [jnp.where(i1 == e, NEG_INF, masked[e]) for e in range(N_EXPERTS)]
    m2 = functools.reduce(jnp.maximum, masked2)
    i2 = jnp.full(m2.shape, N_EXPERTS, jnp.int32)
    for e in reversed(range(N_EXPERTS)):
        i2 = jnp.where((masked2[e] == m2) & (i1 != e), e, i2)
    sel1 = functools.reduce(lambda acc, e: acc + jnp.where(i1 == e, srow[e], 0.0), range(N_EXPERTS), 0.0)
    sel2 = functools.reduce(lambda acc, e: acc + jnp.where(i2 == e, srow[e], 0.0), range(N_EXPERTS), 0.0)
    tot = sel1 + sel2
    g1 = sel1 / tot
    g2 = sel2 / tot
    route = jnp.concatenate([i1.astype(F32), i2.astype(F32), g1, g2, jnp.zeros((4, tm), F32)], axis=0)
    rt_ref[0] = route
    rtc_ref[...] = jnp.concatenate([route, jnp.zeros((LANES - 8, tm), F32)], axis=0).T
    sub = lax.broadcasted_iota(jnp.int32, (N_EXPERTS, tm), 0)
    onehot = jnp.where((i1 == sub) | (i2 == sub), 1.0, 0.0)
    lane = lax.broadcasted_iota(jnp.int32, (1, LANES), 1)
    cnt = jnp.zeros((N_EXPERTS, LANES), F32)
    for j in range(tm // DISPATCH_BLOCK):
        cj = jnp.sum(onehot[:, j * DISPATCH_BLOCK:(j + 1) * DISPATCH_BLOCK], axis=-1, keepdims=True)
        cnt = jnp.where(lane == j, cj, cnt)
    cnt_ref[0] = cnt


def _outproj_call(ya, zc, yc, yd, x2, mod_l, wo, cw, cb, lng, lnb, rw, rb, seq):
    n, d = x2.shape
    tm = TM_PROJ
    tps = seq // tm
    hw = 16
    row = lambda w: pl.BlockSpec((tm, w), lambda i: (i, 0))
    full = lambda a: pl.BlockSpec(a.shape, lambda i: (0,) * a.ndim)
    return pl.pallas_call(
        functools.partial(_outproj_kernel, tps),
        grid=(n // tm,),
        in_specs=[row(GW), row(ZC),
                  pl.BlockSpec((hw, ZC), lambda i: (jnp.maximum(i * (tm // hw) - 1, 0), 0)),
                  row(GW), row(GW), row(d),
                  pl.BlockSpec((1, 6, d), lambda i: (i // tps, 0, 0)),
                  full(wo), full(cw), full(cb), full(lng), full(lnb), full(rw), full(rb)],
        out_specs=[row(d), row(d), pl.BlockSpec((1, 8, tm), lambda i: (i, 0, 0)), row(LANES),
                   pl.BlockSpec((1, N_EXPERTS, LANES), lambda i: (i, 0, 0))],
        out_shape=[jax.ShapeDtypeStruct((n, d), F32), jax.ShapeDtypeStruct((n, d), BF16),
                   jax.ShapeDtypeStruct((n // tm, 8, tm), F32), jax.ShapeDtypeStruct((n, LANES), F32),
                   jax.ShapeDtypeStruct((n // tm, N_EXPERTS, LANES), F32)],
        compiler_params=_cparams(("parallel",)),
        name="out_proj_router",
    )(ya, zc, zc, yc, yd, x2, mod_l, wo, cw, cb, lng, lnb, rw, rb)


def _pack_bf16_pairs(x):
    half = x.shape[1] // 2
    xb = x.astype(BF16).astype(F32)
    hi = lax.bitcast_convert_type(xb[:, :half], jnp.uint32)
    lo = lax.bitcast_convert_type(xb[:, half:], jnp.uint32)
    return (hi & jnp.uint32(0xFFFF0000)) | (lo >> 16)


def _unpack_bf16_pairs(w):
    hi = lax.bitcast_convert_type(w & jnp.uint32(0xFFFF0000), F32).astype(BF16)
    lo = lax.bitcast_convert_type(w << 16, F32).astype(BF16)
    return jnp.concatenate([hi, lo], axis=1)


def _segment_pieces(count, fn):
    for b in reversed(range(SEG_PIECE_BITS)):
        size = SEG_ALIGN << b
        shift = SEG_ALIGN.bit_length() - 1 + b
        hi = lax.shift_left(lax.shift_right_logical(count, shift + 1), shift + 1)

        @pl.when((lax.shift_right_logical(count, shift) & 1) == 1)
        def _():
            fn(hi, size)


def _slot_positions(onehot_bf16, strict_lower_bf16, transposed):
    if transposed:
        return jnp.dot(strict_lower_bf16, onehot_bf16, preferred_element_type=F32)
    return lax.dot_general(onehot_bf16, strict_lower_bf16, (((1,), (1,)), ((), ())), preferred_element_type=F32)


def _dispatch_kernel(off_s, pc_s, gb_s, rows_s, gs_s, gl_s, tail_s, u_ref, rt_ref, xs_ref, slab, zbuf, sem):
    blk = pl.program_id(0)
    tb = u_ref.shape[0]
    n_slots = slab.shape[1]
    e1 = rt_ref[0, 0:1, :]
    e2 = rt_ref[0, 1:2, :]
    sub_e = lax.broadcasted_iota(jnp.int32, (N_EXPERTS, tb), 0).astype(F32)
    hit1 = e1 == sub_e
    hit2 = e2 == sub_e
    onehot = jnp.where(hit1 | hit2, 1.0, 0.0).astype(BF16)
    r = lax.broadcasted_iota(jnp.int32, (tb, tb), 0)
    c = lax.broadcasted_iota(jnp.int32, (tb, tb), 1)
    lower = jnp.where(c < r, 1.0, 0.0).astype(BF16)
    rank = _slot_positions(onehot, lower, transposed=False)
    sub1 = lax.broadcasted_iota(jnp.int32, (N_EXPERTS, 1), 0)
    offv = jnp.zeros((N_EXPERTS, 1), F32)
    for e in range(N_EXPERTS):
        offv = jnp.where(sub1 == e, off_s[blk * N_EXPERTS + e].astype(F32), offv)
    pos = offv + rank
    pos1 = jnp.sum(jnp.where(hit1, pos, 0.0), axis=0, keepdims=True)
    pos2 = jnp.sum(jnp.where(hit2, pos, 0.0), axis=0, keepdims=True)
    slot = lax.broadcasted_iota(jnp.int32, (n_slots, tb), 0).astype(F32)
    perm = jnp.where((slot == pos1) | (slot == pos2), 1.0, 0.0).astype(BF16)
    cur = blk % 2
    slab[cur] = _pack_bf16_pairs(jnp.dot(perm, u_ref[...], preferred_element_type=F32))

    @pl.when(blk == 0)
    def _():
        zbuf[...] = jnp.zeros_like(zbuf)

    def start_segments(b, slot):
        for e in range(N_EXPERTS):
            idx = b * N_EXPERTS + e
            src0, dst0 = off_s[idx], gb_s[idx]

            def piece(hi, size, src0=src0, dst0=dst0):
                pltpu.make_async_copy(slab.at[slot, pl.ds(pl.multiple_of(src0 + hi, SEG_ALIGN), size)],
                                      xs_ref.at[pl.ds(pl.multiple_of(dst0 + hi, SEG_ALIGN), size)],
                                      sem.at[slot]).start()
            _segment_pieces(pc_s[idx], piece)

    def wait_segments(b, slot):
        def wait_rows(n_static):
            pltpu.make_async_copy(slab.at[slot, pl.ds(0, n_static)], xs_ref.at[pl.ds(0, n_static)],
                                  sem.at[slot]).wait()
        wait_rows(2 * DISPATCH_BLOCK)
        _segment_pieces(rows_s[b] - 2 * DISPATCH_BLOCK, lambda hi, size: wait_rows(size))

    def fill_copies(slot, go):
        for e in range(N_EXPERTS):
            dst0 = gs_s[e]

            def piece(hi, size, dst0=dst0):
                go(pltpu.make_async_copy(zbuf.at[pl.ds(0, size)],
                                         xs_ref.at[pl.ds(pl.multiple_of(dst0 + hi, SEG_ALIGN), size)], sem.at[slot]))
            _segment_pieces(gl_s[e], piece)

        zrows = zbuf.shape[0]

        def tail_chunk(i, carry):
            go(pltpu.make_async_copy(zbuf, xs_ref.at[pl.ds(pl.multiple_of(tail_s[0] + i * zrows, zrows), zrows)],
                                     sem.at[slot]))
            return carry
        lax.fori_loop(0, tail_s[1], tail_chunk, 0)

    last = pl.num_programs(0) - 1
    start_segments(blk, cur)

    @pl.when(blk == 0)
    def _():
        fill_copies(cur, lambda cp: cp.start())

    @pl.when(blk > 0)
    def _():
        wait_segments(blk - 1, 1 - cur)

    @pl.when(blk == jnp.minimum(1, last))
    def _():
        fill_copies(0, lambda cp: cp.wait())

    @pl.when(blk == last)
    def _():
        wait_segments(blk, cur)


def _dispatch_call(meta, u2, rt, n_rows):
    n, d = u2.shape
    tb = DISPATCH_BLOCK
    per_tile = rt.shape[2] // tb
    grid_spec = pltpu.PrefetchScalarGridSpec(
        num_scalar_prefetch=7,
        grid=(n // tb,),
        in_specs=[pl.BlockSpec((tb, d), lambda i, *_: (i, 0)),
                  pl.BlockSpec((1, 8, tb), lambda i, *_: (i // per_tile, 0, i % per_tile))],
        out_specs=pl.BlockSpec(memory_space=pl.ANY),
        scratch_shapes=[pltpu.VMEM((2, DISPATCH_SLOTS, d // 2), jnp.uint32),
                        pltpu.VMEM((SEG_ALIGN << (SEG_PIECE_BITS - 1), d // 2), jnp.uint32),
                        pltpu.SemaphoreType.DMA((2,))],
    )
    return pl.pallas_call(
        _dispatch_kernel,
        grid_spec=grid_spec,
        out_shape=jax.ShapeDtypeStruct((n_rows, d // 2), jnp.uint32),
        compiler_params=_cparams(("arbitrary",)),
        name="moe_dispatch",
    )(meta["off"], meta["pc"], meta["gbase"], meta["rows"], meta["gap_start"], meta["gap_len"], meta["tail"], u2, rt)


def _experts_kernel(te_s, na_s, x_ref, wg_ref, wu_ref, wd_ref, y_ref, wg_bf, wu_bf, wd_bf):
    i = pl.program_id(0)
    active = i < na_s[0]

    @pl.when(active & ((i == 0) | (te_s[i] != te_s[jnp.maximum(i - 1, 0)])))
    def _():
        wg_bf[...] = wg_ref[0, 0].astype(BF16)
        wu_bf[...] = wu_ref[0, 0].astype(BF16)
        wd_bf[...] = wd_ref[0, 0].astype(BF16)

    @pl.when(active)
    def _():
        x = _unpack_bf16_pairs(x_ref[...])
        gate = jnp.dot(x, wg_bf[...], preferred_element_type=F32)
        up = jnp.dot(x, wu_bf[...], preferred_element_type=F32)
        hidden = (gate * _sigmoid(gate) * up).astype(BF16)
        y_ref[...] = _pack_bf16_pairs(jnp.dot(hidden, wd_bf[...], preferred_element_type=F32))

    @pl.when(jnp.logical_not(active))
    def _():
        y_ref[...] = jnp.zeros_like(y_ref)


def _experts_call(meta, xs, layer, wg, wu, wd):
    n_rows = xs.shape[0]
    tm = TM_MOE
    _, _, d, de = wg.shape
    tile = lambda i, te, na: jnp.minimum(i, na[0] - 1)
    grid_spec = pltpu.PrefetchScalarGridSpec(
        num_scalar_prefetch=2,
        grid=(n_rows // tm,),
        in_specs=[pl.BlockSpec((tm, d // 2), lambda i, te, na: (tile(i, te, na), 0)),
                  pl.BlockSpec((1, 1, d, de), lambda i, te, na: (layer, te[tile(i, te, na)], 0, 0)),
                  pl.BlockSpec((1, 1, d, de), lambda i, te, na: (layer, te[tile(i, te, na)], 0, 0)),
                  pl.BlockSpec((1, 1, de, d), lambda i, te, na: (layer, te[tile(i, te, na)], 0, 0))],
        out_specs=pl.BlockSpec((tm, d // 2), lambda i, te, na: (i, 0)),
        scratch_shapes=[pltpu.VMEM((d, de), BF16), pltpu.VMEM((d, de), BF16), pltpu.VMEM((de, d), BF16)],
    )
    return pl.pallas_call(
        _experts_kernel,
        grid_spec=grid_spec,
        out_shape=jax.ShapeDtypeStruct((n_rows, d // 2), jnp.uint32),
        compiler_params=_cparams(("arbitrary",)),
        name="experts",
    )(meta["tile_expert"], meta["n_active"], xs, wg, wu, wd)


def _combine_kernel(off_s, pc_s, gb_s, rows_s, rtc_ref, x1_ref, mod_ref, lng_ref, lnb_ref, ys_ref, o_ref, slab, sem):
    blk = pl.program_id(0)
    tb = x1_ref.shape[0]
    n_slots = slab.shape[1]
    cur = blk % 2

    def start_segments(b, slot):
        for e in range(N_EXPERTS):
            idx = b * N_EXPERTS + e
            dst0, src0 = off_s[idx], gb_s[idx]

            def piece(hi, size, src0=src0, dst0=dst0):
                pltpu.make_async_copy(ys_ref.at[pl.ds(pl.multiple_of(src0 + hi, SEG_ALIGN), size)],
                                      slab.at[slot, pl.ds(pl.multiple_of(dst0 + hi, SEG_ALIGN), size)],
                                      sem.at[slot]).start()
            _segment_pieces(pc_s[idx], piece)

    def wait_segments(b, slot):
        def wait_rows(n_static):
            pltpu.make_async_copy(ys_ref.at[pl.ds(0, n_static)], slab.at[slot, pl.ds(0, n_static)],
                                  sem.at[slot]).wait()
        wait_rows(2 * DISPATCH_BLOCK)
        _segment_pieces(rows_s[b] - 2 * DISPATCH_BLOCK, lambda hi, size: wait_rows(size))

    @pl.when(blk == 0)
    def _():
        slab[...] = jnp.zeros_like(slab)
        start_segments(blk, cur)

    @pl.when(blk + 1 < pl.num_programs(0))
    def _():
        start_segments(blk + 1, 1 - cur)

    rtc = rtc_ref[...]
    e1, e2, g1, g2 = rtc[:, 0:1], rtc[:, 1:2], rtc[:, 2:3], rtc[:, 3:4]
    lane_e = lax.broadcasted_iota(jnp.int32, (1, LANES), 1).astype(F32)
    hit1 = e1 == lane_e
    hit2 = e2 == lane_e
    onehot = jnp.where(hit1 | hit2, 1.0, 0.0).astype(BF16)
    r = lax.broadcasted_iota(jnp.int32, (tb, tb), 0)
    c = lax.broadcasted_iota(jnp.int32, (tb, tb), 1)
    lower = jnp.where(c < r, 1.0, 0.0).astype(BF16)
    rank = _slot_positions(onehot, lower, transposed=True)
    lane1 = lax.broadcasted_iota(jnp.int32, (1, LANES), 1)
    offv = jnp.zeros((1, LANES), F32)
    for e in range(N_EXPERTS):
        offv = jnp.where(lane1 == e, off_s[blk * N_EXPERTS + e].astype(F32), offv)
    pos = offv + rank
    pos1 = jnp.sum(jnp.where(hit1, pos, 0.0), axis=-1, keepdims=True)
    pos2 = jnp.sum(jnp.where(hit2, pos, 0.0), axis=-1, keepdims=True)
    slot = lax.broadcasted_iota(jnp.int32, (1, n_slots), 1).astype(F32)
    sel1 = jnp.where(slot == pos1, 1.0, 0.0).astype(BF16)
    sel2 = jnp.where(slot == pos2, 1.0, 0.0).astype(BF16)

    wait_segments(blk, cur)
    ys = _unpack_bf16_pairs(slab[cur])
    y = (g1 * jnp.dot(sel1, ys, preferred_element_type=F32) + g2 * jnp.dot(sel2, ys, preferred_element_type=F32))
    mod = mod_ref[0]
    o_ref[...] = _layer_norm(ALPHA * x1_ref[...] + (1.0 + mod[5:6, :]) * y, lng_ref[...], lnb_ref[...])


def _combine_call(meta, ys, rtc, x1, mod_l, lng, lnb, seq):
    n, d = x1.shape
    tb = DISPATCH_BLOCK
    bps = seq // tb
    grid_spec = pltpu.PrefetchScalarGridSpec(
        num_scalar_prefetch=4,
        grid=(n // tb,),
        in_specs=[pl.BlockSpec((tb, LANES), lambda i, *_: (i, 0)),
                  pl.BlockSpec((tb, d), lambda i, *_: (i, 0)),
                  pl.BlockSpec((1, 6, d), lambda i, *_: (i // bps, 0, 0)),
                  pl.BlockSpec((1, d), lambda i, *_: (0, 0)),
                  pl.BlockSpec((1, d), lambda i, *_: (0, 0)),
                  pl.BlockSpec(memory_space=pl.ANY)],
        out_specs=pl.BlockSpec((tb, d), lambda i, *_: (i, 0)),
        scratch_shapes=[pltpu.VMEM((2, DISPATCH_SLOTS, d // 2), jnp.uint32), pltpu.SemaphoreType.DMA((2,))],
    )
    return pl.pallas_call(
        _combine_kernel,
        grid_spec=grid_spec,
        out_shape=jax.ShapeDtypeStruct((n, d), F32),
        compiler_params=_cparams(("arbitrary",)),
        name="moe_combine_ln2",
    )(meta["off"], meta["pc"], meta["gbase"], meta["rows"], rtc, x1, mod_l, lng, lnb, ys)


def _sorted_rows_bound(n_tokens):
    n_blocks = n_tokens // DISPATCH_BLOCK
    worst = 2 * n_tokens + n_blocks * N_EXPERTS * (SEG_ALIGN - 1) + N_EXPERTS * (TM_MOE - 1)
    return -(-worst // TM_MOE) * TM_MOE


def _route_meta(cnt, n_rows):
    pc = (cnt + SEG_ALIGN - 1) // SEG_ALIGN * SEG_ALIGN
    off = jnp.cumsum(pc, axis=1) - pc
    tot = pc.sum(axis=0)
    reg = (tot + TM_MOE - 1) // TM_MOE * TM_MOE
    reg_end = jnp.cumsum(reg)
    ebase = reg_end - reg
    gbase = ebase[None, :] + jnp.cumsum(pc, axis=0) - pc
    tiles = jnp.arange(n_rows // TM_MOE, dtype=jnp.int32)
    tile_expert = jnp.minimum(jnp.sum(tiles[:, None] >= (reg_end // TM_MOE)[None, :], axis=1), N_EXPERTS - 1)
    i32 = lambda a: a.astype(jnp.int32).reshape(-1)
    used = reg_end[-1:]
    tail_chunk = SEG_ALIGN << (SEG_PIECE_BITS - 1)
    return {"off": i32(off), "pc": i32(pc), "gbase": i32(gbase), "rows": i32(pc.sum(axis=1)),
            "gap_start": i32(ebase + tot),
            "gap_len": i32(reg - tot), "tail": i32(jnp.concatenate([used, (n_rows - used) // tail_chunk])),
            "tile_expert": i32(tile_expert), "n_active": i32(used // TM_MOE)}


def _rearranged_w_in(w_in_l):
    d = w_in_l.shape[0]
    o_gate, o_cb, o_ga = ZM, ZM + 2 * HEADS, ZM + 2 * HEADS + ZC + ZQ + ZK + ZVP + ZG
    cols = [w_in_l[:, 0:ZM], w_in_l[:, o_cb:o_ga],
            w_in_l[:, o_gate:o_gate + 2 * HEADS], w_in_l[:, o_ga:o_ga + GLA_RANK],
            jnp.zeros((d, ZS - 2 * HEADS - GLA_RANK), w_in_l.dtype)]
    return jnp.concatenate(cols, axis=1).astype(BF16)


def _rope_tables(positions):
    inv_freq = ROPE_THETA ** (-jnp.arange(0, ROPE_DIM, 2, dtype=F32) / ROPE_DIM)
    ang = positions.astype(F32).reshape(-1, 1) * inv_freq
    cos, sin = jnp.cos(ang), jnp.sin(ang)
    d = np.arange(ZQ) % DIFF_DIM
    half = ROPE_DIM // 2
    idx = d % half
    cosf = jnp.where(d < ROPE_DIM, cos[:, idx], 1.0)
    sinf = jnp.where(d < half, -sin[:, idx], jnp.where(d < ROPE_DIM, sin[:, idx], 0.0))
    return cosf, sinf


def _pad_lanes(v, width=LANES):
    v = v.reshape(1, -1)
    return jnp.pad(v, ((0, 0), (0, width - v.shape[1])))


def kernel(x, c, positions, w_ada, b_ada, w_in, w_out, mlstm_b_i, mlstm_b_f, mlstm_norm_g, conv_w, conv_b, diff_lambda_q1, diff_lambda_k1, diff_lambda_q2, diff_lambda_k2, diff_norm_g, gla_w_a2, gla_b_a, gla_norm_g, ln1_g, ln1_b, ln2_g, ln2_b, router_w, router_bias, exp_w_gate, exp_w_up, exp_w_down):
    bsz, seq, d = x.shape
    depth = w_in.shape[0]
    n = bsz * seq
    x2 = x.reshape(n, d)
    mod = _mod_call(c, w_ada, b_ada)
    cosf, sinf = _rope_tables(positions)
    rw_t = router_w.T.astype(BF16)
    rb = router_bias.reshape(N_EXPERTS, 1).astype(F32)
    n_sorted = _sorted_rows_bound(n)
    for l in range(depth):
        lam_init = 0.8 - 0.6 * math.exp(-0.3 * l)
        w_l = _rearranged_w_in(w_in[l])
        zm, zc, zq, zk, zvt, zg, zs = _inproj_call(x2, mod[l], w_l, cosf, sinf, seq)
        bif = _pad_lanes(jnp.concatenate([mlstm_b_i[l], mlstm_b_f[l]]))
        ya = _mlstm_call(zm, zs, bif, mlstm_norm_g[l].reshape(1, -1), bsz, seq)
        lamp = jnp.concatenate([_pad_lanes(diff_lambda_q1[l]), _pad_lanes(diff_lambda_k1[l]),
                                _pad_lanes(diff_lambda_q2[l]), _pad_lanes(diff_lambda_k2[l])], axis=0)
        yc = _attn_call(zq, zk, zvt, lamp, diff_norm_g[l].reshape(1, -1), lam_init, bsz, seq)
        wa = jnp.pad(gla_w_a2[l], ((2 * HEADS, ZS - 2 * HEADS - GLA_RANK), (0, 0))).astype(BF16)
        yd = _gla_call(zg, zs, wa, gla_b_a[l].reshape(1, -1), gla_norm_g[l].reshape(1, -1), bsz, seq)
        x1, u2, rt, rtc, cnt = _outproj_call(ya, zc, yc, yd, x2, mod[l], w_out[l].astype(BF16), conv_w[l],
                                             conv_b[l].reshape(1, -1), ln1_g[l].reshape(1, -1),
                                             ln1_b[l].reshape(1, -1), rw_t, rb, seq)
        per_tile = TM_PROJ // DISPATCH_BLOCK
        cnt_blk = cnt[:, :, :per_tile].transpose(0, 2, 1).reshape(-1, N_EXPERTS).astype(jnp.int32)
        meta = _route_meta(cnt_blk, n_sorted)
        xs = _dispatch_call(meta, u2, rt, n_sorted)
        ys = _experts_call(meta, xs, l, exp_w_gate, exp_w_up, exp_w_down)
        x2 = _combine_call(meta, ys, rtc, x1, mod[l], ln2_g[l].reshape(1, -1), ln2_b[l].reshape(1, -1), seq)
    return x2.reshape(bsz, seq, d)
```

```python
import functools
import math

import numpy as np
import jax
import jax.numpy as jnp
from jax import lax
from jax.experimental import pallas as pl
from jax.experimental.pallas import tpu as pltpu

F32 = jnp.float32
BF16 = jnp.bfloat16

DEPTH = 4
HEADS = 4
GW = 256
HEAD_DIM = 64
DIFF_DIM = 32
ROPE_DIM = 8
ROPE_THETA = 500000.0
CONV_WIDTH = 3
GLA_RANK = 16
GLA_TAU = 16.0
CHUNK = 64
GLA_CHUNK = 64
N_EXPERTS = 16
N_GROUPS = 4
GROUP_SIZE = 4
ALPHA = (2 * DEPTH) ** 0.25
EPS = 1e-6
LANES = 128
SUBLANES = 8
NEG_INF = float("-inf")
LOG2_E = math.log2(math.e)

ZM, ZC, ZQ, ZK, ZVP, ZG, ZS = 1024, 768, 256, 256, 256, 1024, 128
ZV = HEADS * LANES
Z_OFFS = np.cumsum([0, ZM, ZC, ZQ, ZK, ZVP, ZG, ZS])
Z_TOTAL = int(Z_OFFS[-1])

TM_PROJ = 512
TB_MLSTM = 256
TB_GLA = 1024
CUM_SPAN = 128
TQ_ATTN = 512
TM_MOE = 512
DISPATCH_BLOCK = 256
SEG_ALIGN = 8
SEG_PIECE_BITS = 6
DISPATCH_SLOTS = 2 * DISPATCH_BLOCK + N_EXPERTS * SEG_ALIGN
assert SEG_ALIGN << (SEG_PIECE_BITS - 1) >= DISPATCH_BLOCK and SEG_ALIGN << SEG_PIECE_BITS >= TM_MOE
VMEM_LIMIT = 56 * 1024 * 1024


def _cparams(sem):
    return pltpu.CompilerParams(dimension_semantics=sem, vmem_limit_bytes=VMEM_LIMIT)


def _log_sigmoid(x):
    return jnp.minimum(x, 0.0) - jnp.log(1.0 + jnp.exp(-jnp.abs(x)))


def _sigmoid(x):
    return 1.0 / (1.0 + jnp.exp(-x))


def _layer_norm(v, g, b):
    mu = jnp.mean(v, axis=-1, keepdims=True)
    d = v - mu
    var = jnp.mean(d * d, axis=-1, keepdims=True)
    return d * lax.rsqrt(var + EPS) * g + b


def _mod_kernel(c_ref, w_ref, b_ref, o_ref):
    c = c_ref[...]
    c_act = (c * _sigmoid(c)).astype(BF16)
    o_ref[0, 0] = jnp.dot(c_act, w_ref[0].astype(BF16), preferred_element_type=F32) + b_ref[0, 0]


def _mod_call(c, w_ada, b_ada):
    depth, d, _ = w_ada.shape
    bsz = c.shape[0]
    b3 = b_ada.reshape(depth, 6, 1, d)
    out = pl.pallas_call(
        _mod_kernel,
        grid=(depth, 6),
        in_specs=[pl.BlockSpec((bsz, d), lambda l, j: (0, 0)),
                  pl.BlockSpec((1, d, d), lambda l, j: (l, 0, j)),
                  pl.BlockSpec((1, 1, 1, d), lambda l, j: (l, j, 0, 0))],
        out_specs=pl.BlockSpec((1, 1, bsz, d), lambda l, j: (l, j, 0, 0)),
        out_shape=jax.ShapeDtypeStruct((depth, 6, bsz, d), F32),
        compiler_params=_cparams(("arbitrary", "arbitrary")),
        name="adaln_mod",
    )(c, w_ada, b3)
    return out.transpose(0, 2, 1, 3)


def _inproj_kernel(x_ref, mod_ref, w_ref, cos_ref, sin_ref, zm_ref, zc_ref, zq_ref, zk_ref, zvt_ref, zg_ref,
                   zs_ref):
    mod = mod_ref[0]
    u = (x_ref[...] * (1.0 + mod[1:2, :]) + mod[0:1, :]).astype(BF16)

    def proj(idx):
        return jnp.dot(u, w_ref[:, int(Z_OFFS[idx]):int(Z_OFFS[idx + 1])], preferred_element_type=F32)

    zm_ref[...] = proj(0).astype(BF16)
    zc_ref[...] = proj(1).astype(BF16)

    lane = lax.broadcasted_iota(jnp.int32, (1, ZQ), 1)
    first_half = (lane % ROPE_DIM) < (ROPE_DIM // 2)
    cosf = cos_ref[...]
    sinf = sin_ref[...]

    def rope(r):
        swapped = jnp.where(first_half, pltpu.roll(r, ZQ - ROPE_DIM // 2, 1), pltpu.roll(r, ROPE_DIM // 2, 1))
        return r * cosf + swapped * sinf

    zq_ref[...] = (rope(proj(2)) * (LOG2_E * DIFF_DIM ** -0.5)).astype(BF16)
    zk_ref[...] = rope(proj(3)).astype(BF16)
    vt = proj(4).T
    tm = vt.shape[1]
    ones_rows = jnp.where(lax.broadcasted_iota(jnp.int32, (LANES - HEAD_DIM, tm), 0) == 0, 1.0, 0.0)
    zvt_ref[...] = jnp.concatenate([blk for h in range(HEADS)
                                    for blk in (vt[h * HEAD_DIM:(h + 1) * HEAD_DIM], ones_rows)], axis=0).astype(BF16)
    zg_ref[...] = proj(5).astype(BF16)
    zs_ref[...] = proj(6)


def _inproj_call(x2, mod_l, w_l, cosf, sinf, seq):
    n, d = x2.shape
    tm = TM_PROJ
    tps = seq // tm
    row = lambda w: pl.BlockSpec((tm, w), lambda i: (i, 0))
    outs = [(ZM, BF16), (ZC, BF16), (ZQ, BF16), (ZK, BF16), (ZV, BF16), (ZG, BF16), (ZS, F32)]
    return pl.pallas_call(
        _inproj_kernel,
        grid=(n // tm,),
        in_specs=[row(d),
                  pl.BlockSpec((1, 6, d), lambda i: (i // tps, 0, 0)),
                  pl.BlockSpec((d, Z_TOTAL), lambda i: (0, 0)),
                  row(ZQ), row(ZQ)],
        out_specs=[pl.BlockSpec((ZV, tm), lambda i: (0, i)) if k == 4 else row(w) for k, (w, _) in enumerate(outs)],
        out_shape=[jax.ShapeDtypeStruct((ZV, n) if k == 4 else (n, w), dt) for k, (w, dt) in enumerate(outs)],
        compiler_params=_cparams(("parallel",)),
        name="in_proj",
    )(x2, mod_l, w_l, cosf, sinf)


NT_DIMS = (((1,), (1,)), ((), ()))
TN_DIMS = (((0,), (0,)), ((), ()))


def _head_of(idx):
    return idx // HEAD_DIM


def _cumsum_matrix(span, chunk):
    r = lax.broadcasted_iota(jnp.int32, (span, span), 0)
    c = lax.broadcasted_iota(jnp.int32, (span, span), 1)
    return jnp.where(((r // chunk) == (c // chunk)) & (c <= r), 1.0, 0.0).astype(BF16)


def _bf16_terms(x, n_terms):
    terms, rest = [], x
    for _ in range(n_terms):
        t = rest.astype(BF16)
        terms.append(t)
        rest = rest - t.astype(F32)
    return terms


def _exact_matmul(lhs_f32, rhs01, n_terms=3):
    return functools.reduce(lambda a, b: a + b, [jnp.dot(t, rhs01, preferred_element_type=F32)
                                                 for t in reversed(_bf16_terms(lhs_f32, n_terms))])


def _chunk_cumsum(x, chunk):
    cm = _cumsum_matrix(CUM_SPAN, chunk)
    outs = []
    for i in range(x.shape[0] // CUM_SPAN):
        terms = _bf16_terms(x[i * CUM_SPAN:(i + 1) * CUM_SPAN], 3)
        outs.append(functools.reduce(lambda a, b: a + b,
                                     [jnp.dot(cm, t, preferred_element_type=F32) for t in reversed(terms)]))
    return jnp.concatenate(outs, axis=0)


def _rows_broadcast(x, chunk, row):
    n = x.shape[0] // chunk
    return jnp.concatenate([jnp.broadcast_to(x[c * chunk + row:c * chunk + row + 1, :], (chunk, x.shape[1]))
                            for c in range(n)], axis=0)


def _head_masks(chunk):
    lane_head = _head_of(lax.broadcasted_iota(jnp.int32, (1, GW), 1))
    row_head = lax.broadcasted_iota(jnp.int32, (HEADS * chunk, 1), 0) // chunk
    stack_keep = row_head == lane_head
    rr = lax.broadcasted_iota(jnp.int32, (HEADS * chunk, chunk), 0) % chunk
    cc = lax.broadcasted_iota(jnp.int32, (HEADS * chunk, chunk), 1)
    block_diag = (_head_of(lax.broadcasted_iota(jnp.int32, (GW, GW), 0))
                  == _head_of(lax.broadcasted_iota(jnp.int32, (GW, GW), 1)))
    return stack_keep, cc <= rr, block_diag


def _unstack_heads(x4, keep, chunk):
    x4 = jnp.where(keep, x4, 0.0)
    return functools.reduce(lambda a, b: a + b, [x4[h * chunk:(h + 1) * chunk] for h in range(HEADS)])


def _head_rms_norm(x, block_diag):
    seg = jnp.where(block_diag, 1.0, 0.0).astype(BF16)
    return x * lax.rsqrt(_exact_matmul(x * x, seg, 2) * (1.0 / HEAD_DIM) + EPS)


def _mlstm_kernel(zm_ref, zs_ref, bif_ref, g_ref, o_ref, ct_scr, nm_scr, ni_scr, up_scr):
    @pl.when(pl.program_id(1) == 0)
    def _():
        ct_scr[...] = jnp.zeros_like(ct_scr)
        nm_scr[...] = jnp.zeros_like(nm_scr)

    tb = zm_ref.shape[0]
    L = CHUNK
    nc = tb // L
    lane = lax.broadcasted_iota(jnp.int32, (1, LANES), 1)
    graw = zs_ref[...] + bif_ref[...]
    gp = jnp.where(lane < HEADS, graw, jnp.where(lane < 2 * HEADS, _log_sigmoid(graw), 0.0))
    cum = _chunk_cumsum(gp, L)

    def expand(x, first_lane):
        sel = jnp.where((lax.broadcasted_iota(jnp.int32, (LANES, GW), 0) - first_lane)
                        == _head_of(lax.broadcasted_iota(jnp.int32, (LANES, GW), 1)), 1.0, 0.0).astype(BF16)
        return _exact_matmul(x, sel)

    bx = expand(cum, HEADS)
    lix = expand(gp, 0)
    bex = _rows_broadcast(bx, L, L - 1)
    dkx = bex - bx + lix
    dkmax = [jnp.max(dkx[c * L:(c + 1) * L], axis=0, keepdims=True) for c in range(nc)]
    kw = (zm_ref[:, GW:2 * GW].astype(F32)
          * jnp.exp(dkx - jnp.concatenate([jnp.broadcast_to(d, (L, GW)) for d in dkmax], axis=0))).astype(BF16)
    cum_t = cum.T
    gfull = bx - lix
    stack_keep, _, block_diag = _head_masks(L)
    stack_keep2 = jnp.concatenate([stack_keep, stack_keep], axis=1)
    causal_t = (lax.broadcasted_iota(jnp.int32, (L, HEADS * L), 0)
                <= lax.broadcasted_iota(jnp.int32, (L, HEADS * L), 1) % L)
    ones_blk = jnp.ones((L, GW), BF16)

    def chunk_cummax(x):
        shift = 1
        while shift < L:
            x = jnp.maximum(x, jnp.concatenate([jnp.full((shift, x.shape[1]), NEG_INF, F32), x[:L - shift]], axis=0))
            shift *= 2
        return x

    dmaxx = []
    for c in range(nc):
        rs = slice(c * L, (c + 1) * L)
        qc = zm_ref[rs, 0:GW] * (HEAD_DIM ** -0.5)
        kc = zm_ref[rs, GW:2 * GW]
        vc = zm_ref[rs, 2 * GW:3 * GW]
        q4 = jnp.where(stack_keep, jnp.concatenate([qc] * HEADS, axis=0), jnp.zeros((), BF16))
        st = lax.dot_general(kc, q4, NT_DIMS, preferred_element_type=F32)
        b_query = jnp.concatenate([cum_t[HEADS + h:HEADS + h + 1, rs] for h in range(HEADS)], axis=1)
        dt = jnp.where(causal_t, b_query - gfull[rs], NEG_INF)
        dmax_t = jnp.max(dt, axis=0, keepdims=True)
        pt = (st * jnp.exp(dt - dmax_t)).astype(BF16)
        i4 = lax.dot_general(pt, jnp.concatenate([vc, ones_blk], axis=1), TN_DIMS, preferred_element_type=F32)
        ni_scr[rs, :] = _unstack_heads(i4, stack_keep2, L)
        dmaxx.append(bx[rs] + chunk_cummax(-gfull[rs]))
        up = lax.dot_general(vc, kw[rs], TN_DIMS, preferred_element_type=F32)
        up_scr[c] = jnp.where(block_diag, up, 0.0)

    ct = ct_scr[...]
    n_row = nm_scr[0:1, :]
    m_prev = nm_scr[1:2, :]
    hv_rows = []
    for c in range(nc):
        rs = slice(c * L, (c + 1) * L)
        qc = zm_ref[rs, 0:GW] * (HEAD_DIM ** -0.5)
        nmat = jnp.where(block_diag, jnp.broadcast_to(n_row, (GW, GW)), 0.0).astype(BF16)
        num_inter = lax.dot_general(qc, ct.astype(BF16), NT_DIMS, preferred_element_type=F32)
        den_inter = lax.dot_general(qc, nmat, NT_DIMS, preferred_element_type=F32)
        inter = bx[rs] + m_prev
        m_row = jnp.maximum(inter, dmaxx[c])
        e1 = jnp.exp(dmaxx[c] - m_row)
        e2 = jnp.exp(inter - m_row)
        num = e1 * ni_scr[rs, 0:GW] + e2 * num_inter
        den = e1 * ni_scr[rs, GW:2 * GW] + e2 * den_inter
        hv_rows.append(num / jnp.maximum(jnp.abs(den), jnp.exp(-m_row)))
        b_end = bex[c * L:c * L + 1, :]
        m_new = jnp.maximum(b_end + m_prev, dkmax[c])
        decay = jnp.exp(b_end + m_prev - m_new)
        f2 = jnp.exp(dkmax[c] - m_new)
        ct = ct * decay + up_scr[c] * f2
        n_row = n_row * decay + f2 * jnp.sum(kw[rs].astype(F32), axis=0, keepdims=True)
        m_prev = m_new
    ct_scr[...] = ct
    nm_scr[0:1, :] = n_row
    nm_scr[1:2, :] = m_prev

    hn = _head_rms_norm(jnp.concatenate(hv_rows, axis=0), block_diag)
    o_ref[...] = (hn * g_ref[...] * _sigmoid(zm_ref[:, 3 * GW:4 * GW].astype(F32))).astype(BF16)


def _mlstm_call(zm, zs, bif, gain, bsz, seq):
    tb = TB_MLSTM
    tps = seq // tb
    return pl.pallas_call(
        _mlstm_kernel,
        grid=(bsz, tps),
        in_specs=[pl.BlockSpec((tb, ZM), lambda b, t: (b * tps + t, 0)),
                  pl.BlockSpec((tb, ZS), lambda b, t: (b * tps + t, 0)),
                  pl.BlockSpec((1, LANES), lambda b, t: (0, 0)),
                  pl.BlockSpec((1, GW), lambda b, t: (0, 0))],
        out_specs=pl.BlockSpec((tb, GW), lambda b, t: (b * tps + t, 0)),
        out_shape=jax.ShapeDtypeStruct((bsz * seq, GW), BF16),
        scratch_shapes=[pltpu.VMEM((GW, GW), F32),
                        pltpu.VMEM((8, GW), F32),
                        pltpu.VMEM((tb, 2 * GW), F32),
                        pltpu.VMEM((tb // CHUNK, GW, GW), F32)],
        compiler_params=_cparams(("parallel", "arbitrary")),
        name="mlstm",
    )(zm, zs, bif, jnp.tile(gain, (1, HEADS)))


def _gla_kernel(zg_ref, zs_ref, wa_ref, ba_ref, g_ref, o_ref, st_scr, oi_scr, up_scr):
    @pl.when(pl.program_id(1) == 0)
    def _():
        st_scr[...] = jnp.zeros_like(st_scr)

    tb = zg_ref.shape[0]
    lg = GLA_CHUNK
    nc = tb // lg
    la = _log_sigmoid(jnp.dot(zs_ref[...].astype(BF16), wa_ref[...], preferred_element_type=F32)
                      + ba_ref[...]) * (1.0 / GLA_TAU)
    bc = _chunk_cumsum(la, lg)
    be = _rows_broadcast(bc, lg, lg - 1)
    bm = _rows_broadcast(bc, lg, lg // 2)
    q = zg_ref[:, 0:GW].astype(F32) * (HEAD_DIM ** -0.5)
    k = zg_ref[:, GW:2 * GW].astype(F32)
    qm = (q * jnp.exp(bc - bm)).astype(BF16)
    km = (k * jnp.exp(bm - bc)).astype(BF16)
    qt = (q * jnp.exp(bc)).astype(BF16)
    ks = (k * jnp.exp(be - bc)).astype(BF16)
    dec = jnp.exp(be)
    stack_keep, causal4, block_diag = _head_masks(lg)

    for c in range(nc):
        rs = slice(c * lg, (c + 1) * lg)
        vc = zg_ref[rs, 2 * GW:3 * GW]
        q4 = jnp.where(stack_keep, jnp.concatenate([qm[rs]] * HEADS, axis=0), jnp.zeros((), BF16))
        att4 = lax.dot_general(q4, km[rs], NT_DIMS, preferred_element_type=F32)
        att4 = jnp.where(causal4, att4, 0.0).astype(BF16)
        oi_scr[rs, :] = _unstack_heads(jnp.dot(att4, vc, preferred_element_type=F32), stack_keep, lg)
        up = lax.dot_general(vc, ks[rs], TN_DIMS, preferred_element_type=F32)
        up_scr[c] = jnp.where(block_diag, up, 0.0)

    st = st_scr[...]
    inter = []
    for c in range(nc):
        rs = slice(c * lg, (c + 1) * lg)
        inter.append(lax.dot_general(qt[rs], st.astype(BF16), NT_DIMS, preferred_element_type=F32))
        st = st * dec[c * lg:c * lg + 1, :] + up_scr[c]
    st_scr[...] = st

    on = _head_rms_norm(oi_scr[...] + jnp.concatenate(inter, axis=0), block_diag)
    r = zg_ref[:, 3 * GW:4 * GW].astype(F32)
    o_ref[...] = (on * g_ref[...] * (r * _sigmoid(r))).astype(BF16)


def _gla_call(zg, zs, wa, ba, gain, bsz, seq):
    tb = TB_GLA
    tps = seq // tb
    return pl.pallas_call(
        _gla_kernel,
        grid=(bsz, tps),
        in_specs=[pl.BlockSpec((tb, ZG), lambda b, t: (b * tps + t, 0)),
                  pl.BlockSpec((tb, ZS), lambda b, t: (b * tps + t, 0)),
                  pl.BlockSpec((ZS, GW), lambda b, t: (0, 0)),
                  pl.BlockSpec((1, GW), lambda b, t: (0, 0)),
                  pl.BlockSpec((1, GW), lambda b, t: (0, 0))],
        out_specs=pl.BlockSpec((tb, GW), lambda b, t: (b * tps + t, 0)),
        out_shape=jax.ShapeDtypeStruct((bsz * seq, GW), BF16),
        scratch_shapes=[pltpu.VMEM((GW, GW), F32),
                        pltpu.VMEM((tb, GW), F32),
                        pltpu.VMEM((tb // GLA_CHUNK, GW, GW), F32)],
        compiler_params=_cparams(("parallel", "arbitrary")),
        name="gla",
    )(zg, zs, wa, ba, jnp.tile(gain, (1, HEADS)))


def _attn_kernel(lam_init, q_ref, k_ref, vt_ref, lamp_ref, g_ref, o_ref, qm_scr, m_scr, acc_scr, st_scr):
    tq = q_ref.shape[0]
    qi = pl.program_id(1)
    n_hc = 2 * HEADS
    lane = lax.broadcasted_iota(jnp.int32, (1, LANES), 1)
    for hc in range(n_hc):
        h, c = divmod(hc, 2)
        pair = h // 2
        lo = (h % 2) * HEAD_DIM + c * DIFF_DIM
        qp = q_ref[:, pair * LANES:(pair + 1) * LANES]
        qm_scr[hc] = jnp.where((lane >= lo) & (lane < lo + DIFF_DIM), qp, jnp.zeros_like(qp))
    m_scr[...] = jnp.full(m_scr.shape, NEG_INF, F32)
    acc_scr[...] = jnp.zeros_like(acc_scr)

    def sublane_allmax(x):
        for shift in (4, 2, 1):
            x = jnp.maximum(x, pltpu.roll(x, shift, 0))
        return x

    def kv_block(j, keep_t):
        key0 = pl.multiple_of(j * tq, tq)
        for hc in range(n_hc):
            pair = (hc // 2) // 2
            kb = k_ref[pl.ds(key0, tq), pair * LANES:(pair + 1) * LANES]
            st_scr[hc] = lax.dot_general(kb, qm_scr[hc], NT_DIMS, preferred_element_type=F32)
        for hc in range(n_hc):
            h = hc // 2
            vt = vt_ref[h * LANES:(h + 1) * LANES, pl.ds(key0, tq)]
            st = st_scr[hc]
            if keep_t is not None:
                st = jnp.where(keep_t, st, NEG_INF)
            st3 = st.reshape(tq // SUBLANES, SUBLANES, tq)
            m_old = m_scr[hc]
            m_new = jnp.maximum(m_old, sublane_allmax(jnp.max(st3, axis=0)))
            alpha = jnp.exp2(m_old - m_new)
            p = jnp.exp2(st3 - m_new[None]).reshape(tq, tq).astype(BF16)
            acc = acc_scr[hc].reshape(LANES // SUBLANES, SUBLANES, tq) * alpha[None]
            acc_scr[hc] = acc.reshape(LANES, tq) + jnp.dot(vt, p, preferred_element_type=F32)
            m_scr[hc] = m_new

    def full_block(j, carry):
        kv_block(j, None)
        return carry

    lax.fori_loop(0, qi, full_block, 0)
    key_idx = lax.broadcasted_iota(jnp.int32, (tq, tq), 0)
    query_idx = lax.broadcasted_iota(jnp.int32, (tq, tq), 1)
    kv_block(qi, key_idx <= query_idx)

    lamp = lamp_ref[...]
    lam = (jnp.exp(jnp.sum(lamp[0:1, :] * lamp[1:2, :], axis=-1, keepdims=True))
           - jnp.exp(jnp.sum(lamp[2:3, :] * lamp[3:4, :], axis=-1, keepdims=True)) + lam_init)
    gain = g_ref[...] * (1.0 - lam_init)
    outs = []
    for h in range(HEADS):
        a1 = acc_scr[2 * h]
        a2 = acc_scr[2 * h + 1]
        o = a1[:HEAD_DIM] / a1[HEAD_DIM:HEAD_DIM + 1] - lam * (a2[:HEAD_DIM] / a2[HEAD_DIM:HEAD_DIM + 1])
        on = o * lax.rsqrt(jnp.mean(o * o, axis=0, keepdims=True) + EPS)
        outs.append(on.T * gain)
    o_ref[...] = jnp.concatenate(outs, axis=1).astype(BF16)


def _attn_call(zq, zk, zvt, lamp, gain, lam_init, bsz, seq):
    tq = TQ_ATTN
    nq = seq // tq
    return pl.pallas_call(
        functools.partial(_attn_kernel, lam_init),
        grid=(bsz, nq),
        in_specs=[pl.BlockSpec((tq, ZQ), lambda b, i: (b * nq + i, 0)),
                  pl.BlockSpec((seq, ZK), lambda b, i: (b, 0)),
                  pl.BlockSpec((ZV, seq), lambda b, i: (0, b)),
                  pl.BlockSpec((4, LANES), lambda b, i: (0, 0)),
                  pl.BlockSpec((1, HEAD_DIM), lambda b, i: (0, 0))],
        out_specs=pl.BlockSpec((tq, GW), lambda b, i: (b * nq + i, 0)),
        out_shape=jax.ShapeDtypeStruct((bsz * seq, GW), BF16),
        scratch_shapes=[pltpu.VMEM((2 * HEADS, tq, LANES), BF16),
                        pltpu.VMEM((2 * HEADS, SUBLANES, tq), F32),
                        pltpu.VMEM((2 * HEADS, LANES, tq), F32),
                        pltpu.VMEM((2 * HEADS, tq, tq), F32)],
        compiler_params=_cparams(("parallel", "arbitrary")),
        name="diff_attn",
    )(zq, zk, zvt, lamp, gain)


def _outproj_kernel(tiles_per_seq, ya_ref, zc_ref, halo_ref, yc_ref, yd_ref, x_ref, mod_ref, wo_ref, cw_ref,
                    cb_ref, lng_ref, lnb_ref, rw_ref, rb_ref, x1_ref, u2_ref, rt_ref, rtc_ref, cnt_ref):
    tm = x_ref.shape[0]
    mod = mod_ref[0]
    zc = zc_ref[...].astype(F32)
    halo = halo_ref[...].astype(F32)
    first = (pl.program_id(0) % tiles_per_seq) == 0
    s_cur = zc[:, GW:2 * GW] * zc[:, 2 * GW:3 * GW]
    s_halo = jnp.where(first, 0.0, halo[:, GW:2 * GW] * halo[:, 2 * GW:3 * GW])
    s_ext = jnp.concatenate([s_halo, s_cur], axis=0)
    hw = halo.shape[0]
    cw = cw_ref[...]
    conv = cb_ref[...]
    for tap in range(CONV_WIDTH):
        start = hw - (CONV_WIDTH - 1) + tap
        conv = conv + cw[tap:tap + 1, :] * s_ext[start:start + tm, :]
    yb = (zc[:, 0:GW] * conv).astype(BF16)

    y = jnp.dot(jnp.concatenate([ya_ref[...], yb, yc_ref[...], yd_ref[...]], axis=1), wo_ref[...],
                preferred_element_type=F32)
    x1 = _layer_norm(ALPHA * x_ref[...] + (1.0 + mod[2:3, :]) * y, lng_ref[...], lnb_ref[...])
    x1_ref[...] = x1
    u2 = (x1 * (1.0 + mod[4:5, :]) + mod[3:4, :]).astype(BF16)
    u2_ref[...] = u2

    logits = lax.dot_general(rw_ref[...], u2, (((1,), (1,)), ((), ())), preferred_element_type=F32)
    scores = _sigmoid(logits)
    biased = scores + rb_ref[...]
    rows = [biased[e:e + 1, :] for e in range(N_EXPERTS)]
    srow = [scores[e:e + 1, :] for e in range(N_EXPERTS)]
    gscore = []
    for g in range(N_GROUPS):
        a = rows[g * GROUP_SIZE:(g + 1) * GROUP_SIZE]
        best = None
        for i in range(GROUP_SIZE):
            for j in range(i + 1, GROUP_SIZE):
                pair_sum = a[i] + a[j]
                best = pair_sum if best is None else jnp.maximum(best, pair_sum)
        gscore.append(best)
    gmax = functools.reduce(jnp.maximum, gscore)
    gsel = jnp.full(gmax.shape, N_GROUPS, jnp.int32)
    for g in reversed(range(N_GROUPS)):
        gsel = jnp.where(gscore[g] == gmax, g, gsel)
    masked = [jnp.where(gsel == (e // GROUP_SIZE), rows[e], NEG_INF) for e in range(N_EXPERTS)]
    m1 = functools.reduce(jnp.maximum, masked)
    i1 = jnp.full(m1.shape, N_EXPERTS, jnp.int32)
    for e in reversed(range(N_EXPERTS)):
        i1 = jnp.where(masked[e] == m1, e, i1)
    masked2 = [jnp.where(i1 == e, NEG_INF, masked[e]) for e in range(N_EXPERTS)]
    m2 = functools.reduce(jnp.maximum, masked2)
    i2 = jnp.full(m2.shape, N_EXPERTS, jnp.int32)
    for e in reversed(range(N_EXPERTS)):
        i2 = jnp.where((masked2[e] == m2) & (i1 != e), e, i2)
    sel1 = functools.reduce(lambda acc, e: acc + jnp.where(i1 == e, srow[e], 0.0), range(N_EXPERTS), 0.0)
    sel2 = functools.reduce(lambda acc, e: acc + jnp.where(i2 == e, srow[e], 0.0), range(N_EXPERTS), 0.0)
    tot = sel1 + sel2
    g1 = sel1 / tot
    g2 = sel2 / tot
    route = jnp.concatenate([i1.astype(F32), i2.astype(F32), g1, g2, jnp.zeros((4, tm), F32)], axis=0)
    rt_ref[0] = route
    rtc_ref[...] = jnp.concatenate([route, jnp.zeros((LANES - 8, tm), F32)], axis=0).T
    sub = lax.broadcasted_iota(jnp.int32, (N_EXPERTS, tm), 0)
    onehot = jnp.where((i1 == sub) | (i2 == sub), 1.0, 0.0)
    lane = lax.broadcasted_iota(jnp.int32, (1, LANES), 1)
    cnt = jnp.zeros((N_EXPERTS, LANES), F32)
    for j in range(tm // DISPATCH_BLOCK):
        cj = jnp.sum(onehot[:, j * DISPATCH_BLOCK:(j + 1) * DISPATCH_BLOCK], axis=-1, keepdims=True)
        cnt = jnp.where(lane == j, cj, cnt)
    cnt_ref[0] = cnt


def _outproj_call(ya, zc, yc, yd, x2, mod_l, wo, cw, cb, lng, lnb, rw, rb, seq):
    n, d = x2.shape
    tm = TM_PROJ
    tps = seq // tm
    hw = 16
    row = lambda w: pl.BlockSpec((tm, w), lambda i: (i, 0))
    full = lambda a: pl.BlockSpec(a.shape, lambda i: (0,) * a.ndim)
    return pl.pallas_call(
        functools.partial(_outproj_kernel, tps),
        grid=(n // tm,),
        in_specs=[row(GW), row(ZC),
                  pl.BlockSpec((hw, ZC), lambda i: (jnp.maximum(i * (tm // hw) - 1, 0), 0)),
                  row(GW), row(GW), row(d),
                  pl.BlockSpec((1, 6, d), lambda i: (i // tps, 0, 0)),
                  full(wo), full(cw), full(cb), full(lng), full(lnb), full(rw), full(rb)],
        out_specs=[row(d), row(d), pl.BlockSpec((1, 8, tm), lambda i: (i, 0, 0)), row(LANES),
                   pl.BlockSpec((1, N_EXPERTS, LANES), lambda i: (i, 0, 0))],
        out_shape=[jax.ShapeDtypeStruct((n, d), F32), jax.ShapeDtypeStruct((n, d), BF16),
                   jax.ShapeDtypeStruct((n // tm, 8, tm), F32), jax.ShapeDtypeStruct((n, LANES), F32),
                   jax.ShapeDtypeStruct((n // tm, N_EXPERTS, LANES), F32)],
        compiler_params=_cparams(("parallel",)),
        name="out_proj_router",
    )(ya, zc, zc, yc, yd, x2, mod_l, wo, cw, cb, lng, lnb, rw, rb)


def _pack_bf16_pairs(x):
    half = x.shape[1] // 2
    xb = x.astype(BF16).astype(F32)
    hi = lax.bitcast_convert_type(xb[:, :half], jnp.uint32)
    lo = lax.bitcast_convert_type(xb[:, half:], jnp.uint32)
    return (hi & jnp.uint32(0xFFFF0000)) | (lo >> 16)


def _unpack_bf16_pairs(w):
    hi = lax.bitcast_convert_type(w & jnp.uint32(0xFFFF0000), F32).astype(BF16)
    lo = lax.bitcast_convert_type(w << 16, F32).astype(BF16)
    return jnp.concatenate([hi, lo], axis=1)


def _segment_pieces(count, fn):
    for b in reversed(range(SEG_PIECE_BITS)):
        size = SEG_ALIGN << b
        shift = SEG_ALIGN.bit_length() - 1 + b
        hi = lax.shift_left(lax.shift_right_logical(count, shift + 1), shift + 1)

        @pl.when((lax.shift_right_logical(count, shift) & 1) == 1)
        def _():
            fn(hi, size)


def _slot_positions(onehot_bf16, strict_lower_bf16, transposed):
    if transposed:
        return jnp.dot(strict_lower_bf16, onehot_bf16, preferred_element_type=F32)
    return lax.dot_general(onehot_bf16, strict_lower_bf16, (((1,), (1,)), ((), ())), preferred_element_type=F32)


def _dispatch_kernel(off_s, pc_s, gb_s, rows_s, gs_s, gl_s, tail_s, u_ref, rt_ref, xs_ref, slab, zbuf, sem):
    blk = pl.program_id(0)
    tb = u_ref.shape[0]
    n_slots = slab.shape[1]
    e1 = rt_ref[0, 0:1, :]
    e2 = rt_ref[0, 1:2, :]
    sub_e = lax.broadcasted_iota(jnp.int32, (N_EXPERTS, tb), 0).astype(F32)
    hit1 = e1 == sub_e
    hit2 = e2 == sub_e
    onehot = jnp.where(hit1 | hit2, 1.0, 0.0).astype(BF16)
    r = lax.broadcasted_iota(jnp.int32, (tb, tb), 0)
    c = lax.broadcasted_iota(jnp.int32, (tb, tb), 1)
    lower = jnp.where(c < r, 1.0, 0.0).astype(BF16)
    rank = _slot_positions(onehot, lower, transposed=False)
    sub1 = lax.broadcasted_iota(jnp.int32, (N_EXPERTS, 1), 0)
    offv = jnp.zeros((N_EXPERTS, 1), F32)
    for e in range(N_EXPERTS):
        offv = jnp.where(sub1 == e, off_s[blk * N_EXPERTS + e].astype(F32), offv)
    pos = offv + rank
    pos1 = jnp.sum(jnp.where(hit1, pos, 0.0), axis=0, keepdims=True)
    pos2 = jnp.sum(jnp.where(hit2, pos, 0.0), axis=0, keepdims=True)
    slot = lax.broadcasted_iota(jnp.int32, (n_slots, tb), 0).astype(F32)
    perm = jnp.where((slot == pos1) | (slot == pos2), 1.0, 0.0).astype(BF16)
    cur = blk % 2
    slab[cur] = _pack_bf16_pairs(jnp.dot(perm, u_ref[...], preferred_element_type=F32))

    @pl.when(blk == 0)
    def _():
        zbuf[...] = jnp.zeros_like(zbuf)

    def start_segments(b, slot):
        for e in range(N_EXPERTS):
            idx = b * N_EXPERTS + e
            src0, dst0 = off_s[idx], gb_s[idx]

            def piece(hi, size, src0=src0, dst0=dst0):
                pltpu.make_async_copy(slab.at[slot, pl.ds(pl.multiple_of(src0 + hi, SEG_ALIGN), size)],
                                      xs_ref.at[pl.ds(pl.multiple_of(dst0 + hi, SEG_ALIGN), size)],
                                      sem.at[slot]).start()
            _segment_pieces(pc_s[idx], piece)

    def wait_segments(b, slot):
        def wait_rows(n_static):
            pltpu.make_async_copy(slab.at[slot, pl.ds(0, n_static)], xs_ref.at[pl.ds(0, n_static)],
                                  sem.at[slot]).wait()
        wait_rows(2 * DISPATCH_BLOCK)
        _segment_pieces(rows_s[b] - 2 * DISPATCH_BLOCK, lambda hi, size: wait_rows(size))

    def fill_copies(slot, go):
        for e in range(N_EXPERTS):
            dst0 = gs_s[e]

            def piece(hi, size, dst0=dst0):
                go(pltpu.make_async_copy(zbuf.at[pl.ds(0, size)],
                                         xs_ref.at[pl.ds(pl.multiple_of(dst0 + hi, SEG_ALIGN), size)], sem.at[slot]))
            _segment_pieces(gl_s[e], piece)

        zrows = zbuf.shape[0]

        def tail_chunk(i, carry):
            go(pltpu.make_async_copy(zbuf, xs_ref.at[pl.ds(pl.multiple_of(tail_s[0] + i * zrows, zrows), zrows)],
                                     sem.at[slot]))
            return carry
        lax.fori_loop(0, tail_s[1], tail_chunk, 0)

    last = pl.num_programs(0) - 1
    start_segments(blk, cur)

    @pl.when(blk == 0)
    def _():
        fill_copies(cur, lambda cp: cp.start())

    @pl.when(blk > 0)
    def _():
        wait_segments(blk - 1, 1 - cur)

    @pl.when(blk == jnp.minimum(1, last))
    def _():
        fill_copies(0, lambda cp: cp.wait())

    @pl.when(blk == last)
    def _():
        wait_segments(blk, cur)


def _dispatch_call(meta, u2, rt, n_rows):
    n, d = u2.shape
    tb = DISPATCH_BLOCK
    per_tile = rt.shape[2] // tb
    grid_spec = pltpu.PrefetchScalarGridSpec(
        num_scalar_prefetch=7,
        grid=(n // tb,),
        in_specs=[pl.BlockSpec((tb, d), lambda i, *_: (i, 0)),
                  pl.BlockSpec((1, 8, tb), lambda i, *_: (i // per_tile, 0, i % per_tile))],
        out_specs=pl.BlockSpec(memory_space=pl.ANY),
        scratch_shapes=[pltpu.VMEM((2, DISPATCH_SLOTS, d // 2), jnp.uint32),
                        pltpu.VMEM((SEG_ALIGN << (SEG_PIECE_BITS - 1), d // 2), jnp.uint32),
                        pltpu.SemaphoreType.DMA((2,))],
    )
    return pl.pallas_call(
        _dispatch_kernel,
        grid_spec=grid_spec,
        out_shape=jax.ShapeDtypeStruct((n_rows, d // 2), jnp.uint32),
        compiler_params=_cparams(("arbitrary",)),
        name="moe_dispatch",
    )(meta["off"], meta["pc"], meta["gbase"], meta["rows"], meta["gap_start"], meta["gap_len"], meta["tail"], u2, rt)


def _experts_kernel(te_s, na_s, x_ref, wg_ref, wu_ref, wd_ref, y_ref, wg_bf, wu_bf, wd_bf):
    i = pl.program_id(0)
    active = i < na_s[0]

    @pl.when(active & ((i == 0) | (te_s[i] != te_s[jnp.maximum(i - 1, 0)])))
    def _():
        wg_bf[...] = wg_ref[0, 0].astype(BF16)
        wu_bf[...] = wu_ref[0, 0].astype(BF16)
        wd_bf[...] = wd_ref[0, 0].astype(BF16)

    @pl.when(active)
    def _():
        x = _unpack_bf16_pairs(x_ref[...])
        gate = jnp.dot(x, wg_bf[...], preferred_element_type=F32)
        up = jnp.dot(x, wu_bf[...], preferred_element_type=F32)
        hidden = (gate * _sigmoid(gate) * up).astype(BF16)
        y_ref[...] = _pack_bf16_pairs(jnp.dot(hidden, wd_bf[...], preferred_element_type=F32))

    @pl.when(jnp.logical_not(active))
    def _():
        y_ref[...] = jnp.zeros_like(y_ref)


def _experts_call(meta, xs, layer, wg, wu, wd):
    n_rows = xs.shape[0]
    tm = TM_MOE
    _, _, d, de = wg.shape
    tile = lambda i, te, na: jnp.minimum(i, na[0] - 1)
    grid_spec = pltpu.PrefetchScalarGridSpec(
        num_scalar_prefetch=2,
        grid=(n_rows // tm,),
        in_specs=[pl.BlockSpec((tm, d // 2), lambda i, te, na: (tile(i, te, na), 0)),
                  pl.BlockSpec((1, 1, d, de), lambda i, te, na: (layer, te[tile(i, te, na)], 0, 0)),
                  pl.BlockSpec((1, 1, d, de), lambda i, te, na: (layer, te[tile(i, te, na)], 0, 0)),
                  pl.BlockSpec((1, 1, de, d), lambda i, te, na: (layer, te[tile(i, te, na)], 0, 0))],
        out_specs=pl.BlockSpec((tm, d // 2), lambda i, te, na: (i, 0)),
        scratch_shapes=[pltpu.VMEM((d, de), BF16), pltpu.VMEM((d, de), BF16), pltpu.VMEM((de, d), BF16)],
    )
    return pl.pallas_call(
        _experts_kernel,
        grid_spec=grid_spec,
        out_shape=jax.ShapeDtypeStruct((n_rows, d // 2), jnp.uint32),
        compiler_params=_cparams(("arbitrary",)),
        name="experts",
    )(meta["tile_expert"], meta["n_active"], xs, wg, wu, wd)


def _combine_kernel(off_s, pc_s, gb_s, rows_s, rtc_ref, x1_ref, mod_ref, lng_ref, lnb_ref, ys_ref, o_ref, slab, sem):
    blk = pl.program_id(0)
    tb = x1_ref.shape[0]
    n_slots = slab.shape[1]
    cur = blk % 2

    def start_segments(b, slot):
        for e in range(N_EXPERTS):
            idx = b * N_EXPERTS + e
            dst0, src0 = off_s[idx], gb_s[idx]

            def piece(hi, size, src0=src0, dst0=dst0):
                pltpu.make_async_copy(ys_ref.at[pl.ds(pl.multiple_of(src0 + hi, SEG_ALIGN), size)],
                                      slab.at[slot, pl.ds(pl.multiple_of(dst0 + hi, SEG_ALIGN), size)],
                                      sem.at[slot]).start()
            _segment_pieces(pc_s[idx], piece)

    def wait_segments(b, slot):
        def wait_rows(n_static):
            pltpu.make_async_copy(ys_ref.at[pl.ds(0, n_static)], slab.at[slot, pl.ds(0, n_static)],
                                  sem.at[slot]).wait()
        wait_rows(2 * DISPATCH_BLOCK)
        _segment_pieces(rows_s[b] - 2 * DISPATCH_BLOCK, lambda hi, size: wait_rows(size))

    @pl.when(blk == 0)
    def _():
        slab[...] = jnp.zeros_like(slab)
        start_segments(blk, cur)

    @pl.when(blk + 1 < pl.num_programs(0))
    def _():
        start_segments(blk + 1, 1 - cur)

    rtc = rtc_ref[...]
    e1, e2, g1, g2 = rtc[:, 0:1], rtc[:, 1:2], rtc[:, 2:3], rtc[:, 3:4]
    lane_e = lax.broadcasted_iota(jnp.int32, (1, LANES), 1).astype(F32)
    hit1 = e1 == lane_e
    hit2 = e2 == lane_e
    onehot = jnp.where(hit1 | hit2, 1.0, 0.0).astype(BF16)
    r = lax.broadcasted_iota(jnp.int32, (tb, tb), 0)
    c = lax.broadcasted_iota(jnp.int32, (tb, tb), 1)
    lower = jnp.where(c < r, 1.0, 0.0).astype(BF16)
    rank = _slot_positions(onehot, lower, transposed=True)
    lane1 = lax.broadcasted_iota(jnp.int32, (1, LANES), 1)
    offv = jnp.zeros((1, LANES), F32)
    for e in range(N_EXPERTS):
        offv = jnp.where(lane1 == e, off_s[blk * N_EXPERTS + e].astype(F32), offv)
    pos = offv + rank
    pos1 = jnp.sum(jnp.where(hit1, pos, 0.0), axis=-1, keepdims=True)
    pos2 = jnp.sum(jnp.where(hit2, pos, 0.0), axis=-1, keepdims=True)
    slot = lax.broadcasted_iota(jnp.int32, (1, n_slots), 1).astype(F32)
    sel1 = jnp.where(slot == pos1, 1.0, 0.0).astype(BF16)
    sel2 = jnp.where(slot == pos2, 1.0, 0.0).astype(BF16)

    wait_segments(blk, cur)
    ys = _unpack_bf16_pairs(slab[cur])
    y = (g1 * jnp.dot(sel1, ys, preferred_element_type=F32) + g2 * jnp.dot(sel2, ys, preferred_element_type=F32))
    mod = mod_ref[0]
    o_ref[...] = _layer_norm(ALPHA * x1_ref[...] + (1.0 + mod[5:6, :]) * y, lng_ref[...], lnb_ref[...])


def _combine_call(meta, ys, rtc, x1, mod_l, lng, lnb, seq):
    n, d = x1.shape
    tb = DISPATCH_BLOCK
    bps = seq // tb
    grid_spec = pltpu.PrefetchScalarGridSpec(
        num_scalar_prefetch=4,
        grid=(n // tb,),
        in_specs=[pl.BlockSpec((tb, LANES), lambda i, *_: (i, 0)),
                  pl.BlockSpec((tb, d), lambda i, *_: (i, 0)),
                  pl.BlockSpec((1, 6, d), lambda i, *_: (i // bps, 0, 0)),
                  pl.BlockSpec((1, d), lambda i, *_: (0, 0)),
                  pl.BlockSpec((1, d), lambda i, *_: (0, 0)),
                  pl.BlockSpec(memory_space=pl.ANY)],
        out_specs=pl.BlockSpec((tb, d), lambda i, *_: (i, 0)),
        scratch_shapes=[pltpu.VMEM((2, DISPATCH_SLOTS, d // 2), jnp.uint32), pltpu.SemaphoreType.DMA((2,))],
    )
    return pl.pallas_call(
        _combine_kernel,
        grid_spec=grid_spec,
        out_shape=jax.ShapeDtypeStruct((n, d), F32),
        compiler_params=_cparams(("arbitrary",)),
        name="moe_combine_ln2",
    )(meta["off"], meta["pc"], meta["gbase"], meta["rows"], rtc, x1, mod_l, lng, lnb, ys)


def _sorted_rows_bound(n_tokens):
    n_blocks = n_tokens // DISPATCH_BLOCK
    worst = 2 * n_tokens + n_blocks * N_EXPERTS * (SEG_ALIGN - 1) + N_EXPERTS * (TM_MOE - 1)
    return -(-worst // TM_MOE) * TM_MOE


def _route_meta(cnt, n_rows):
    pc = (cnt + SEG_ALIGN - 1) // SEG_ALIGN * SEG_ALIGN
    off = jnp.cumsum(pc, axis=1) - pc
    tot = pc.sum(axis=0)
    reg = (tot + TM_MOE - 1) // TM_MOE * TM_MOE
    reg_end = jnp.cumsum(reg)
    ebase = reg_end - reg
    gbase = ebase[None, :] + jnp.cumsum(pc, axis=0) - pc
    tiles = jnp.arange(n_rows // TM_MOE, dtype=jnp.int32)
    tile_expert = jnp.minimum(jnp.sum(tiles[:, None] >= (reg_end // TM_MOE)[None, :], axis=1), N_EXPERTS - 1)
    i32 = lambda a: a.astype(jnp.int32).reshape(-1)
    used = reg_end[-1:]
    tail_chunk = SEG_ALIGN << (SEG_PIECE_BITS - 1)
    return {"off": i32(off), "pc": i32(pc), "gbase": i32(gbase), "rows": i32(pc.sum(axis=1)),
            "gap_start": i32(ebase + tot),
            "gap_len": i32(reg - tot), "tail": i32(jnp.concatenate([used, (n_rows - used) // tail_chunk])),
            "tile_expert": i32(tile_expert), "n_active": i32(used // TM_MOE)}


def _rearranged_w_in(w_in_l):
    d = w_in_l.shape[0]
    o_gate, o_cb, o_ga = ZM, ZM + 2 * HEADS, ZM + 2 * HEADS + ZC + ZQ + ZK + ZVP + ZG
    cols = [w_in_l[:, 0:ZM], w_in_l[:, o_cb:o_ga],
            w_in_l[:, o_gate:o_gate + 2 * HEADS], w_in_l[:, o_ga:o_ga + GLA_RANK],
            jnp.zeros((d, ZS - 2 * HEADS - GLA_RANK), w_in_l.dtype)]
    return jnp.concatenate(cols, axis=1).astype(BF16)


def _rope_tables(positions):
    inv_freq = ROPE_THETA ** (-jnp.arange(0, ROPE_DIM, 2, dtype=F32) / ROPE_DIM)
    ang = positions.astype(F32).reshape(-1, 1) * inv_freq
    cos, sin = jnp.cos(ang), jnp.sin(ang)
    d = np.arange(ZQ) % DIFF_DIM
    half = ROPE_DIM // 2
    idx = d % half
    cosf = jnp.where(d < ROPE_DIM, cos[:, idx], 1.0)
    sinf = jnp.where(d < half, -sin[:, idx], jnp.where(d < ROPE_DIM, sin[:, idx], 0.0))
    return cosf, sinf


def _pad_lanes(v, width=LANES):
    v = v.reshape(1, -1)
    return jnp.pad(v, ((0, 0), (0, width - v.shape[1])))


def kernel(x, c, positions, w_ada, b_ada, w_in, w_out, mlstm_b_i, mlstm_b_f, mlstm_norm_g, conv_w, conv_b, diff_lambda_q1, diff_lambda_k1, diff_lambda_q2, diff_lambda_k2, diff_norm_g, gla_w_a2, gla_b_a, gla_norm_g, ln1_g, ln1_b, ln2_g, ln2_b, router_w, router_bias, exp_w_gate, exp_w_up, exp_w_down):
    bsz, seq, d = x.shape
    depth = w_in.shape[0]
    n = bsz * seq
    x2 = x.reshape(n, d)
    mod = _mod_call(c, w_ada, b_ada)
    cosf, sinf = _rope_tables(positions)
    rw_t = router_w.T.astype(BF16)
    rb = router_bias.reshape(N_EXPERTS, 1).astype(F32)
    n_sorted = _sorted_rows_bound(n)
    for l in range(depth):
        lam_init = 0.8 - 0.6 * math.exp(-0.3 * l)
        w_l = _rearranged_w_in(w_in[l])
        zm, zc, zq, zk, zvt, zg, zs = _inproj_call(x2, mod[l], w_l, cosf, sinf, seq)
        bif = _pad_lanes(jnp.concatenate([mlstm_b_i[l], mlstm_b_f[l]]))
        ya = _mlstm_call(zm, zs, bif, mlstm_norm_g[l].reshape(1, -1), bsz, seq)
        lamp = jnp.concatenate([_pad_lanes(diff_lambda_q1[l]), _pad_lanes(diff_lambda_k1[l]),
                                _pad_lanes(diff_lambda_q2[l]), _pad_lanes(diff_lambda_k2[l])], axis=0)
        yc = _attn_call(zq, zk, zvt, lamp, diff_norm_g[l].reshape(1, -1), lam_init, bsz, seq)
        wa = jnp.pad(gla_w_a2[l], ((2 * HEADS, ZS - 2 * HEADS - GLA_RANK), (0, 0))).astype(BF16)
        yd = _gla_call(zg, zs, wa, gla_b_a[l].reshape(1, -1), gla_norm_g[l].reshape(1, -1), bsz, seq)
        x1, u2, rt, rtc, cnt = _outproj_call(ya, zc, yc, yd, x2, mod[l], w_out[l].astype(BF16), conv_w[l],
                                             conv_b[l].reshape(1, -1), ln1_g[l].reshape(1, -1),
                                             ln1_b[l].reshape(1, -1), rw_t, rb, seq)
        per_tile = TM_PROJ // DISPATCH_BLOCK
        cnt_blk = cnt[:, :, :per_tile].transpose(0, 2, 1).reshape(-1, N_EXPERTS).astype(jnp.int32)
        meta = _route_meta(cnt_blk, n_sorted)
        xs = _dispatch_call(meta, u2, rt, n_sorted)
        ys = _experts_call(meta, xs, l, exp_w_gate, exp_w_up, exp_w_down)
        x2 = _combine_call(meta, ys, rtc, x1, mod[l], ln2_g[l].reshape(1, -1), ln2_b[l].reshape(1, -1), seq)
    return x2.reshape(bsz, seq, d)
```

```python
import functools
import math

import numpy as np
import jax
import jax.numpy as jnp
from jax import lax
from jax.experimental import pallas as pl
from jax.experimental.pallas import tpu as pltpu

F32 = jnp.float32
BF16 = jnp.bfloat16

DEPTH = 4
HEADS = 4
GW = 256
HEAD_DIM = 64
DIFF_DIM = 32
ROPE_DIM = 8
ROPE_THETA = 500000.0
CONV_WIDTH = 3
GLA_RANK = 16
GLA_TAU = 16.0
CHUNK = 64
GLA_CHUNK = 64
N_EXPERTS = 16
N_GROUPS = 4
GROUP_SIZE = 4
ALPHA = (2 * DEPTH) ** 0.25
EPS = 1e-6
LANES = 128
SUBLANES = 8
NEG_INF = float("-inf")
LOG2_E = math.log2(math.e)

ZM, ZC, ZQ, ZK, ZVP, ZG, ZS = 1024, 768, 256, 256, 256, 1024, 128
ZV = HEADS * LANES
Z_OFFS = np.cumsum([0, ZM, ZC, ZQ, ZK, ZVP, ZG, ZS])
Z_TOTAL = int(Z_OFFS[-1])

TM_PROJ = 1024
TB_MLSTM = 256
TB_GLA = 1024
CUM_SPAN = 128
TQ_ATTN = 512
TM_MOE = 512
DISPATCH_BLOCK = 256
SEG_ALIGN = 8
SEG_PIECE_BITS = 6
DISPATCH_SLOTS = 2 * DISPATCH_BLOCK + N_EXPERTS * SEG_ALIGN
assert SEG_ALIGN << (SEG_PIECE_BITS - 1) >= DISPATCH_BLOCK and SEG_ALIGN << SEG_PIECE_BITS >= TM_MOE
VMEM_LIMIT = 56 * 1024 * 1024


def _cparams(sem):
    return pltpu.CompilerParams(dimension_semantics=sem, vmem_limit_bytes=VMEM_LIMIT)


def _log_sigmoid(x):
    return jnp.minimum(x, 0.0) - jnp.log(1.0 + jnp.exp(-jnp.abs(x)))


def _sigmoid(x):
    return 1.0 / (1.0 + jnp.exp(-x))


def _layer_norm(v, g, b):
    mu = jnp.mean(v, axis=-1, keepdims=True)
    d = v - mu
    var = jnp.mean(d * d, axis=-1, keepdims=True)
    return d * lax.rsqrt(var + EPS) * g + b


def _mod_kernel(c_ref, w_ref, b_ref, o_ref):
    c = c_ref[...]
    c_act = (c * _sigmoid(c)).astype(BF16)
    o_ref[0, 0] = jnp.dot(c_act, w_ref[0].astype(BF16), preferred_element_type=F32) + b_ref[0, 0]


def _mod_call(c, w_ada, b_ada):
    depth, d, _ = w_ada.shape
    bsz = c.shape[0]
    b3 = b_ada.reshape(depth, 6, 1, d)
    out = pl.pallas_call(
        _mod_kernel,
        grid=(depth, 6),
        in_specs=[pl.BlockSpec((bsz, d), lambda l, j: (0, 0)),
                  pl.BlockSpec((1, d, d), lambda l, j: (l, 0, j)),
                  pl.BlockSpec((1, 1, 1, d), lambda l, j: (l, j, 0, 0))],
        out_specs=pl.BlockSpec((1, 1, bsz, d), lambda l, j: (l, j, 0, 0)),
        out_shape=jax.ShapeDtypeStruct((depth, 6, bsz, d), F32),
        compiler_params=_cparams(("arbitrary", "arbitrary")),
        name="adaln_mod",
    )(c, w_ada, b3)
    return out.transpose(0, 2, 1, 3)


def _inproj_kernel(x_ref, mod_ref, w_ref, cos_ref, sin_ref, zm_ref, zc_ref, zq_ref, zk_ref, zvt_ref, zg_ref,
                   zs_ref):
    mod = mod_ref[0]
    u = (x_ref[...] * (1.0 + mod[1:2, :]) + mod[0:1, :]).astype(BF16)

    def proj(idx):
        return jnp.dot(u, w_ref[:, int(Z_OFFS[idx]):int(Z_OFFS[idx + 1])], preferred_element_type=F32)

    zm_ref[...] = proj(0).astype(BF16)
    zc_ref[...] = proj(1).astype(BF16)

    lane = lax.broadcasted_iota(jnp.int32, (1, ZQ), 1)
    first_half = (lane % ROPE_DIM) < (ROPE_DIM // 2)
    cosf = cos_ref[...]
    sinf = sin_ref[...]

    def rope(r):
        swapped = jnp.where(first_half, pltpu.roll(r, ZQ - ROPE_DIM // 2, 1), pltpu.roll(r, ROPE_DIM // 2, 1))
        return r * cosf + swapped * sinf

    zq_ref[...] = (rope(proj(2)) * (LOG2_E * DIFF_DIM ** -0.5)).astype(BF16)
    zk_ref[...] = rope(proj(3)).astype(BF16)
    vt = proj(4).T
    tm = vt.shape[1]
    ones_rows = jnp.where(lax.broadcasted_iota(jnp.int32, (LANES - HEAD_DIM, tm), 0) == 0, 1.0, 0.0)
    zvt_ref[...] = jnp.concatenate([blk for h in range(HEADS)
                                    for blk in (vt[h * HEAD_DIM:(h + 1) * HEAD_DIM], ones_rows)], axis=0).astype(BF16)
    zg_ref[...] = proj(5).astype(BF16)
    zs_ref[...] = proj(6)


def _inproj_call(x2, mod_l, w_l, cosf, sinf, seq):
    n, d = x2.shape
    tm = TM_PROJ
    tps = seq // tm
    row = lambda w: pl.BlockSpec((tm, w), lambda i: (i, 0))
    outs = [(ZM, BF16), (ZC, BF16), (ZQ, BF16), (ZK, BF16), (ZV, BF16), (ZG, BF16), (ZS, F32)]
    return pl.pallas_call(
        _inproj_kernel,
        grid=(n // tm,),
        in_specs=[row(d),
                  pl.BlockSpec((1, 6, d), lambda i: (i // tps, 0, 0)),
                  pl.BlockSpec((d, Z_TOTAL), lambda i: (0, 0)),
                  row(ZQ), row(ZQ)],
        out_specs=[pl.BlockSpec((ZV, tm), lambda i: (0, i)) if k == 4 else row(w) for k, (w, _) in enumerate(outs)],
        out_shape=[jax.ShapeDtypeStruct((ZV, n) if k == 4 else (n, w), dt) for k, (w, dt) in enumerate(outs)],
        compiler_params=_cparams(("parallel",)),
        name="in_proj",
    )(x2, mod_l, w_l, cosf, sinf)


NT_DIMS = (((1,), (1,)), ((), ()))
TN_DIMS = (((0,), (0,)), ((), ()))


def _head_of(idx):
    return idx // HEAD_DIM


def _cumsum_matrix(span, chunk):
    r = lax.broadcasted_iota(jnp.int32, (span, span), 0)
    c = lax.broadcasted_iota(jnp.int32, (span, span), 1)
    return jnp.where(((r // chunk) == (c // chunk)) & (c <= r), 1.0, 0.0).astype(BF16)


def _bf16_terms(x, n_terms):
    terms, rest = [], x
    for _ in range(n_terms):
        t = rest.astype(BF16)
        terms.append(t)
        rest = rest - t.astype(F32)
    return terms


def _exact_matmul(lhs_f32, rhs01, n_terms=3):
    return functools.reduce(lambda a, b: a + b, [jnp.dot(t, rhs01, preferred_element_type=F32)
                                                 for t in reversed(_bf16_terms(lhs_f32, n_terms))])


def _chunk_cumsum(x, chunk):
    cm = _cumsum_matrix(CUM_SPAN, chunk)
    outs = []
    for i in range(x.shape[0] // CUM_SPAN):
        terms = _bf16_terms(x[i * CUM_SPAN:(i + 1) * CUM_SPAN], 3)
        outs.append(functools.reduce(lambda a, b: a + b,
                                     [jnp.dot(cm, t, preferred_element_type=F32) for t in reversed(terms)]))
    return jnp.concatenate(outs, axis=0)


def _rows_broadcast(x, chunk, row):
    n = x.shape[0] // chunk
    return jnp.concatenate([jnp.broadcast_to(x[c * chunk + row:c * chunk + row + 1, :], (chunk, x.shape[1]))
                            for c in range(n)], axis=0)


def _head_masks(chunk):
    lane_head = _head_of(lax.broadcasted_iota(jnp.int32, (1, GW), 1))
    row_head = lax.broadcasted_iota(jnp.int32, (HEADS * chunk, 1), 0) // chunk
    stack_keep = row_head == lane_head
    rr = lax.broadcasted_iota(jnp.int32, (HEADS * chunk, chunk), 0) % chunk
    cc = lax.broadcasted_iota(jnp.int32, (HEADS * chunk, chunk), 1)
    block_diag = (_head_of(lax.broadcasted_iota(jnp.int32, (GW, GW), 0))
                  == _head_of(lax.broadcasted_iota(jnp.int32, (GW, GW), 1)))
    return stack_keep, cc <= rr, block_diag


def _unstack_heads(x4, keep, chunk):
    x4 = jnp.where(keep, x4, 0.0)
    return functools.reduce(lambda a, b: a + b, [x4[h * chunk:(h + 1) * chunk] for h in range(HEADS)])


def _head_rms_norm(x, block_diag):
    seg = jnp.where(block_diag, 1.0, 0.0).astype(BF16)
    return x * lax.rsqrt(_exact_matmul(x * x, seg, 2) * (1.0 / HEAD_DIM) + EPS)


def _mlstm_kernel(zm_ref, zs_ref, bif_ref, g_ref, o_ref, ct_scr, nm_scr, ni_scr, up_scr):
    @pl.when(pl.program_id(1) == 0)
    def _():
        ct_scr[...] = jnp.zeros_like(ct_scr)
        nm_scr[...] = jnp.zeros_like(nm_scr)

    tb = zm_ref.shape[0]
    L = CHUNK
    nc = tb // L
    lane = lax.broadcasted_iota(jnp.int32, (1, LANES), 1)
    graw = zs_ref[...] + bif_ref[...]
    gp = jnp.where(lane < HEADS, graw, jnp.where(lane < 2 * HEADS, _log_sigmoid(graw), 0.0))
    cum = _chunk_cumsum(gp, L)

    def expand(x, first_lane):
        sel = jnp.where((lax.broadcasted_iota(jnp.int32, (LANES, GW), 0) - first_lane)
                        == _head_of(lax.broadcasted_iota(jnp.int32, (LANES, GW), 1)), 1.0, 0.0).astype(BF16)
        return _exact_matmul(x, sel)

    bx = expand(cum, HEADS)
    lix = expand(gp, 0)
    bex = _rows_broadcast(bx, L, L - 1)
    dkx = bex - bx + lix
    dkmax = [jnp.max(dkx[c * L:(c + 1) * L], axis=0, keepdims=True) for c in range(nc)]
    kw = (zm_ref[:, GW:2 * GW].astype(F32)
          * jnp.exp(dkx - jnp.concatenate([jnp.broadcast_to(d, (L, GW)) for d in dkmax], axis=0))).astype(BF16)
    cum_t = cum.T
    gfull = bx - lix
    stack_keep, _, block_diag = _head_masks(L)
    stack_keep2 = jnp.concatenate([stack_keep, stack_keep], axis=1)
    causal_t = (lax.broadcasted_iota(jnp.int32, (L, HEADS * L), 0)
                <= lax.broadcasted_iota(jnp.int32, (L, HEADS * L), 1) % L)
    ones_blk = jnp.ones((L, GW), BF16)

    def chunk_cummax(x):
        shift = 1
        while shift < L:
            x = jnp.maximum(x, jnp.concatenate([jnp.full((shift, x.shape[1]), NEG_INF, F32), x[:L - shift]], axis=0))
            shift *= 2
        return x

    dmaxx = []
    for c in range(nc):
        rs = slice(c * L, (c + 1) * L)
        qc = zm_ref[rs, 0:GW] * (HEAD_DIM ** -0.5)
        kc = zm_ref[rs, GW:2 * GW]
        vc = zm_ref[rs, 2 * GW:3 * GW]
        q4 = jnp.where(stack_keep, jnp.concatenate([qc] * HEADS, axis=0), jnp.zeros((), BF16))
        st = lax.dot_general(kc, q4, NT_DIMS, preferred_element_type=F32)
        b_query = jnp.concatenate([cum_t[HEADS + h:HEADS + h + 1, rs] for h in range(HEADS)], axis=1)
        dt = jnp.where(causal_t, b_query - gfull[rs], NEG_INF)
        dmax_t = jnp.max(dt, axis=0, keepdims=True)
        pt = (st * jnp.exp(dt - dmax_t)).astype(BF16)
        i4 = lax.dot_general(pt, jnp.concatenate([vc, ones_blk], axis=1), TN_DIMS, preferred_element_type=F32)
        ni_scr[rs, :] = _unstack_heads(i4, stack_keep2, L)
        dmaxx.append(bx[rs] + chunk_cummax(-gfull[rs]))
        up = lax.dot_general(vc, kw[rs], TN_DIMS, preferred_element_type=F32)
        up_scr[c] = jnp.where(block_diag, up, 0.0)

    ct = ct_scr[...]
    n_row = nm_scr[0:1, :]
    m_prev = nm_scr[1:2, :]
    hv_rows = []
    for c in range(nc):
        rs = slice(c * L, (c + 1) * L)
        qc = zm_ref[rs, 0:GW] * (HEAD_DIM ** -0.5)
        nmat = jnp.where(block_diag, jnp.broadcast_to(n_row, (GW, GW)), 0.0).astype(BF16)
        num_inter = lax.dot_general(qc, ct.astype(BF16), NT_DIMS, preferred_element_type=F32)
        den_inter = lax.dot_general(qc, nmat, NT_DIMS, preferred_element_type=F32)
        inter = bx[rs] + m_prev
        m_row = jnp.maximum(inter, dmaxx[c])
        e1 = jnp.exp(dmaxx[c] - m_row)
        e2 = jnp.exp(inter - m_row)
        num = e1 * ni_scr[rs, 0:GW] + e2 * num_inter
        den = e1 * ni_scr[rs, GW:2 * GW] + e2 * den_inter
        hv_rows.append(num / jnp.maximum(jnp.abs(den), jnp.exp(-m_row)))
        b_end = bex[c * L:c * L + 1, :]
        m_new = jnp.maximum(b_end + m_prev, dkmax[c])
        decay = jnp.exp(b_end + m_prev - m_new)
        f2 = jnp.exp(dkmax[c] - m_new)
        ct = ct * decay + up_scr[c] * f2
        n_row = n_row * decay + f2 * jnp.sum(kw[rs].astype(F32), axis=0, keepdims=True)
        m_prev = m_new
    ct_scr[...] = ct
    nm_scr[0:1, :] = n_row
    nm_scr[1:2, :] = m_prev

    hn = _head_rms_norm(jnp.concatenate(hv_rows, axis=0), block_diag)
    o_ref[...] = (hn * g_ref[...] * _sigmoid(zm_ref[:, 3 * GW:4 * GW].astype(F32))).astype(BF16)


def _mlstm_call(zm, zs, bif, gain, bsz, seq):
    tb = TB_MLSTM
    tps = seq // tb
    return pl.pallas_call(
        _mlstm_kernel,
        grid=(bsz, tps),
        in_specs=[pl.BlockSpec((tb, ZM), lambda b, t: (b * tps + t, 0)),
                  pl.BlockSpec((tb, ZS), lambda b, t: (b * tps + t, 0)),
                  pl.BlockSpec((1, LANES), lambda b, t: (0, 0)),
                  pl.BlockSpec((1, GW), lambda b, t: (0, 0))],
        out_specs=pl.BlockSpec((tb, GW), lambda b, t: (b * tps + t, 0)),
        out_shape=jax.ShapeDtypeStruct((bsz * seq, GW), BF16),
        scratch_shapes=[pltpu.VMEM((GW, GW), F32),
                        pltpu.VMEM((8, GW), F32),
                        pltpu.VMEM((tb, 2 * GW), F32),
                        pltpu.VMEM((tb // CHUNK, GW, GW), F32)],
        compiler_params=_cparams(("parallel", "arbitrary")),
        name="mlstm",
    )(zm, zs, bif, jnp.tile(gain, (1, HEADS)))


def _gla_kernel(zg_ref, zs_ref, wa_ref, ba_ref, g_ref, o_ref, st_scr, oi_scr, up_scr):
    @pl.when(pl.program_id(1) == 0)
    def _():
        st_scr[...] = jnp.zeros_like(st_scr)

    tb = zg_ref.shape[0]
    lg = GLA_CHUNK
    nc = tb // lg
    la = _log_sigmoid(jnp.dot(zs_ref[...].astype(BF16), wa_ref[...], preferred_element_type=F32)
                      + ba_ref[...]) * (1.0 / GLA_TAU)
    bc = _chunk_cumsum(la, lg)
    be = _rows_broadcast(bc, lg, lg - 1)
    bm = _rows_broadcast(bc, lg, lg // 2)
    q = zg_ref[:, 0:GW].astype(F32) * (HEAD_DIM ** -0.5)
    k = zg_ref[:, GW:2 * GW].astype(F32)
    qm = (q * jnp.exp(bc - bm)).astype(BF16)
    km = (k * jnp.exp(bm - bc)).astype(BF16)
    qt = (q * jnp.exp(bc)).astype(BF16)
    ks = (k * jnp.exp(be - bc)).astype(BF16)
    dec = jnp.exp(be)
    stack_keep, causal4, block_diag = _head_masks(lg)

    for c in range(nc):
        rs = slice(c * lg, (c + 1) * lg)
        vc = zg_ref[rs, 2 * GW:3 * GW]
        q4 = jnp.where(stack_keep, jnp.concatenate([qm[rs]] * HEADS, axis=0), jnp.zeros((), BF16))
        att4 = lax.dot_general(q4, km[rs], NT_DIMS, preferred_element_type=F32)
        att4 = jnp.where(causal4, att4, 0.0).astype(BF16)
        oi_scr[rs, :] = _unstack_heads(jnp.dot(att4, vc, preferred_element_type=F32), stack_keep, lg)
        up = lax.dot_general(vc, ks[rs], TN_DIMS, preferred_element_type=F32)
        up_scr[c] = jnp.where(block_diag, up, 0.0)

    st = st_scr[...]
    inter = []
    for c in range(nc):
        rs = slice(c * lg, (c + 1) * lg)
        inter.append(lax.dot_general(qt[rs], st.astype(BF16), NT_DIMS, preferred_element_type=F32))
        st = st * dec[c * lg:c * lg + 1, :] + up_scr[c]
    st_scr[...] = st

    on = _head_rms_norm(oi_scr[...] + jnp.concatenate(inter, axis=0), block_diag)
    r = zg_ref[:, 3 * GW:4 * GW].astype(F32)
    o_ref[...] = (on * g_ref[...] * (r * _sigmoid(r))).astype(BF16)


def _gla_call(zg, zs, wa, ba, gain, bsz, seq):
    tb = TB_GLA
    tps = seq // tb
    return pl.pallas_call(
        _gla_kernel,
        grid=(bsz, tps),
        in_specs=[pl.BlockSpec((tb, ZG), lambda b, t: (b * tps + t, 0)),
                  pl.BlockSpec((tb, ZS), lambda b, t: (b * tps + t, 0)),
                  pl.BlockSpec((ZS, GW), lambda b, t: (0, 0)),
                  pl.BlockSpec((1, GW), lambda b, t: (0, 0)),
                  pl.BlockSpec((1, GW), lambda b, t: (0, 0))],
        out_specs=pl.BlockSpec((tb, GW), lambda b, t: (b * tps + t, 0)),
        out_shape=jax.ShapeDtypeStruct((bsz * seq, GW), BF16),
        scratch_shapes=[pltpu.VMEM((GW, GW), F32),
                        pltpu.VMEM((tb, GW), F32),
                        pltpu.VMEM((tb // GLA_CHUNK, GW, GW), F32)],
        compiler_params=_cparams(("parallel", "arbitrary")),
        name="gla",
    )(zg, zs, wa, ba, jnp.tile(gain, (1, HEADS)))


def _attn_kernel(lam_init, q_ref, k_ref, vt_ref, lamp_ref, g_ref, o_ref, qm_scr, m_scr, acc_scr, st_scr):
    tq = q_ref.shape[0]
    qi = pl.program_id(1)
    n_hc = 2 * HEADS
    lane = lax.broadcasted_iota(jnp.int32, (1, LANES), 1)
    for hc in range(n_hc):
        h, c = divmod(hc, 2)
        pair = h // 2
        lo = (h % 2) * HEAD_DIM + c * DIFF_DIM
        qp = q_ref[:, pair * LANES:(pair + 1) * LANES]
        qm_scr[hc] = jnp.where((lane >= lo) & (lane < lo + DIFF_DIM), qp, jnp.zeros_like(qp))
    m_scr[...] = jnp.full(m_scr.shape, NEG_INF, F32)
    acc_scr[...] = jnp.zeros_like(acc_scr)

    def sublane_allmax(x):
        for shift in (4, 2, 1):
            x = jnp.maximum(x, pltpu.roll(x, shift, 0))
        return x

    def kv_block(j, keep_t):
        key0 = pl.multiple_of(j * tq, tq)
        for hc in range(n_hc):
            pair = (hc // 2) // 2
            kb = k_ref[pl.ds(key0, tq), pair * LANES:(pair + 1) * LANES]
            st_scr[hc] = lax.dot_general(kb, qm_scr[hc], NT_DIMS, preferred_element_type=F32)
        for hc in range(n_hc):
            h = hc // 2
            vt = vt_ref[h * LANES:(h + 1) * LANES, pl.ds(key0, tq)]
            st = st_scr[hc]
            if keep_t is not None:
                st = jnp.where(keep_t, st, NEG_INF)
            st3 = st.reshape(tq // SUBLANES, SUBLANES, tq)
            m_old = m_scr[hc]
            m_new = jnp.maximum(m_old, sublane_allmax(jnp.max(st3, axis=0)))
            alpha = jnp.exp2(m_old - m_new)
            p = jnp.exp2(st3 - m_new[None]).reshape(tq, tq).astype(BF16)
            acc = acc_scr[hc].reshape(LANES // SUBLANES, SUBLANES, tq) * alpha[None]
            acc_scr[hc] = acc.reshape(LANES, tq) + jnp.dot(vt, p, preferred_element_type=F32)
            m_scr[hc] = m_new

    def full_block(j, carry):
        kv_block(j, None)
        return carry

    lax.fori_loop(0, qi, full_block, 0)
    key_idx = lax.broadcasted_iota(jnp.int32, (tq, tq), 0)
    query_idx = lax.broadcasted_iota(jnp.int32, (tq, tq), 1)
    kv_block(qi, key_idx <= query_idx)

    lamp = lamp_ref[...]
    lam = (jnp.exp(jnp.sum(lamp[0:1, :] * lamp[1:2, :], axis=-1, keepdims=True))
           - jnp.exp(jnp.sum(lamp[2:3, :] * lamp[3:4, :], axis=-1, keepdims=True)) + lam_init)
    gain = g_ref[...] * (1.0 - lam_init)
    outs = []
    for h in range(HEADS):
        a1 = acc_scr[2 * h]
        a2 = acc_scr[2 * h + 1]
        o = a1[:HEAD_DIM] / a1[HEAD_DIM:HEAD_DIM + 1] - lam * (a2[:HEAD_DIM] / a2[HEAD_DIM:HEAD_DIM + 1])
        on = o * lax.rsqrt(jnp.mean(o * o, axis=0, keepdims=True) + EPS)
        outs.append(on.T * gain)
    o_ref[...] = jnp.concatenate(outs, axis=1).astype(BF16)


def _attn_call(zq, zk, zvt, lamp, gain, lam_init, bsz, seq):
    tq = TQ_ATTN
    nq = seq // tq
    return pl.pallas_call(
        functools.partial(_attn_kernel, lam_init),
        grid=(bsz, nq),
        in_specs=[pl.BlockSpec((tq, ZQ), lambda b, i: (b * nq + i, 0)),
                  pl.BlockSpec((seq, ZK), lambda b, i: (b, 0)),
                  pl.BlockSpec((ZV, seq), lambda b, i: (0, b)),
                  pl.BlockSpec((4, LANES), lambda b, i: (0, 0)),
                  pl.BlockSpec((1, HEAD_DIM), lambda b, i: (0, 0))],
        out_specs=pl.BlockSpec((tq, GW), lambda b, i: (b * nq + i, 0)),
        out_shape=jax.ShapeDtypeStruct((bsz * seq, GW), BF16),
        scratch_shapes=[pltpu.VMEM((2 * HEADS, tq, LANES), BF16),
                        pltpu.VMEM((2 * HEADS, SUBLANES, tq), F32),
                        pltpu.VMEM((2 * HEADS, LANES, tq), F32),
                        pltpu.VMEM((2 * HEADS, tq, tq), F32)],
        compiler_params=_cparams(("parallel", "arbitrary")),
        name="diff_attn",
    )(zq, zk, zvt, lamp, gain)


def _outproj_kernel(tiles_per_seq, ya_ref, zc_ref, halo_ref, yc_ref, yd_ref, x_ref, mod_ref, wo_ref, cw_ref,
                    cb_ref, lng_ref, lnb_ref, rw_ref, rb_ref, x1_ref, u2_ref, rt_ref, rtc_ref, cnt_ref):
    tm = x_ref.shape[0]
    mod = mod_ref[0]
    zc = zc_ref[...].astype(F32)
    halo = halo_ref[...].astype(F32)
    first = (pl.program_id(0) % tiles_per_seq) == 0
    s_cur = zc[:, GW:2 * GW] * zc[:, 2 * GW:3 * GW]
    s_halo = jnp.where(first, 0.0, halo[:, GW:2 * GW] * halo[:, 2 * GW:3 * GW])
    s_ext = jnp.concatenate([s_halo, s_cur], axis=0)
    hw = halo.shape[0]
    cw = cw_ref[...]
    conv = cb_ref[...]
    for tap in range(CONV_WIDTH):
        start = hw - (CONV_WIDTH - 1) + tap
        conv = conv + cw[tap:tap + 1, :] * s_ext[start:start + tm, :]
    yb = (zc[:, 0:GW] * conv).astype(BF16)

    y = jnp.dot(jnp.concatenate([ya_ref[...], yb, yc_ref[...], yd_ref[...]], axis=1), wo_ref[...],
                preferred_element_type=F32)
    x1 = _layer_norm(ALPHA * x_ref[...] + (1.0 + mod[2:3, :]) * y, lng_ref[...], lnb_ref[...])
    x1_ref[...] = x1
    u2 = (x1 * (1.0 + mod[4:5, :]) + mod[3:4, :]).astype(BF16)
    u2_ref[...] = u2

    logits = lax.dot_general(rw_ref[...], u2, (((1,), (1,)), ((), ())), preferred_element_type=F32)
    scores = _sigmoid(logits)
    biased = scores + rb_ref[...]
    rows = [biased[e:e + 1, :] for e in range(N_EXPERTS)]
    srow = [scores[e:e + 1, :] for e in range(N_EXPERTS)]
    gscore = []
    for g in range(N_GROUPS):
        a = rows[g * GROUP_SIZE:(g + 1) * GROUP_SIZE]
        best = None
        for i in range(GROUP_SIZE):
            for j in range(i + 1, GROUP_SIZE):
                pair_sum = a[i] + a[j]
                best = pair_sum if best is None else jnp.maximum(best, pair_sum)
        gscore.append(best)
    gmax = functools.reduce(jnp.maximum, gscore)
    gsel = jnp.full(gmax.shape, N_GROUPS, jnp.int32)
    for g in reversed(range(N_GROUPS)):
        gsel = jnp.where(gscore[g] == gmax, g, gsel)
    masked = [jnp.where(gsel == (e // GROUP_SIZE), rows[e], NEG_INF) for e in range(N_EXPERTS)]
    m1 = functools.reduce(jnp.maximum, masked)
    i1 = jnp.full(m1.shape, N_EXPERTS, jnp.int32)
    for e in reversed(range(N_EXPERTS)):
        i1 = jnp.where(masked[e] == m1, e, i1)
    masked2 = [jnp.where(i1 == e, NEG_INF, masked[e]) for e in range(N_EXPERTS)]
    m2 = functools.reduce(jnp.maximum, masked2)
    i2 = jnp.full(m2.shape, N_EXPERTS, jnp.int32)
    for e in reversed(range(N_EXPERTS)):
        i2 = jnp.where((masked2[e] == m2) & (i1 != e), e, i2)
    sel1 = functools.reduce(lambda acc, e: acc + jnp.where(i1 == e, srow[e], 0.0), range(N_EXPERTS), 0.0)
    sel2 = functools.reduce(lambda acc, e: acc + jnp.where(i2 == e, srow[e], 0.0), range(N_EXPERTS), 0.0)
    tot = sel1 + sel2
    g1 = sel1 / tot
    g2 = sel2 / tot
    route = jnp.concatenate([i1.astype(F32), i2.astype(F32), g1, g2, jnp.zeros((4, tm), F32)], axis=0)
    rt_ref[0] = route
    rtc_ref[...] = jnp.concatenate([route, jnp.zeros((LANES - 8, tm), F32)], axis=0).T
    sub = lax.broadcasted_iota(jnp.int32, (N_EXPERTS, tm), 0)
    onehot = jnp.where((i1 == sub) | (i2 == sub), 1.0, 0.0)
    lane = lax.broadcasted_iota(jnp.int32, (1, LANES), 1)
    cnt = jnp.zeros((N_EXPERTS, LANES), F32)
    for j in range(tm // DISPATCH_BLOCK):
        cj = jnp.sum(onehot[:, j * DISPATCH_BLOCK:(j + 1) * DISPATCH_BLOCK], axis=-1, keepdims=True)
        cnt = jnp.where(lane == j, cj, cnt)
    cnt_ref[0] = cnt


def _outproj_call(ya, zc, yc, yd, x2, mod_l, wo, cw, cb, lng, lnb, rw, rb, seq):
    n, d = x2.shape
    tm = TM_PROJ
    tps = seq // tm
    hw = 16
    row = lambda w: pl.BlockSpec((tm, w), lambda i: (i, 0))
    full = lambda a: pl.BlockSpec(a.shape, lambda i: (0,) * a.ndim)
    return pl.pallas_call(
        functools.partial(_outproj_kernel, tps),
        grid=(n // tm,),
        in_specs=[row(GW), row(ZC),
                  pl.BlockSpec((hw, ZC), lambda i: (jnp.maximum(i * (tm // hw) - 1, 0), 0)),
                  row(GW), row(GW), row(d),
                  pl.BlockSpec((1, 6, d), lambda i: (i // tps, 0, 0)),
                  full(wo), full(cw), full(cb), full(lng), full(lnb), full(rw), full(rb)],
        out_specs=[row(d), row(d), pl.BlockSpec((1, 8, tm), lambda i: (i, 0, 0)), row(LANES),
                   pl.BlockSpec((1, N_EXPERTS, LANES), lambda i: (i, 0, 0))],
        out_shape=[jax.ShapeDtypeStruct((n, d), F32), jax.ShapeDtypeStruct((n, d), BF16),
                   jax.ShapeDtypeStruct((n // tm, 8, tm), F32), jax.ShapeDtypeStruct((n, LANES), F32),
                   jax.ShapeDtypeStruct((n // tm, N_EXPERTS, LANES), F32)],
        compiler_params=_cparams(("parallel",)),
        name="out_proj_router",
    )(ya, zc, zc, yc, yd, x2, mod_l, wo, cw, cb, lng, lnb, rw, rb)


def _pack_bf16_pairs(x):
    half = x.shape[1] // 2
    xb = x.astype(BF16).astype(F32)
    hi = lax.bitcast_convert_type(xb[:, :half], jnp.uint32)
    lo = lax.bitcast_convert_type(xb[:, half:], jnp.uint32)
    return (hi & jnp.uint32(0xFFFF0000)) | (lo >> 16)


def _unpack_bf16_pairs(w):
    hi = lax.bitcast_convert_type(w & jnp.uint32(0xFFFF0000), F32).astype(BF16)
    lo = lax.bitcast_convert_type(w << 16, F32).astype(BF16)
    return jnp.concatenate([hi, lo], axis=1)


def _segment_pieces(count, fn):
    for b in reversed(range(SEG_PIECE_BITS)):
        size = SEG_ALIGN << b
        shift = SEG_ALIGN.bit_length() - 1 + b
        hi = lax.shift_left(lax.shift_right_logical(count, shift + 1), shift + 1)

        @pl.when((lax.shift_right_logical(count, shift) & 1) == 1)
        def _():
            fn(hi, size)


def _slot_positions(onehot_bf16, strict_lower_bf16, transposed):
    if transposed:
        return jnp.dot(strict_lower_bf16, onehot_bf16, preferred_element_type=F32)
    return lax.dot_general(onehot_bf16, strict_lower_bf16, (((1,), (1,)), ((), ())), preferred_element_type=F32)


def _dispatch_kernel(off_s, pc_s, gb_s, rows_s, gs_s, gl_s, tail_s, u_ref, rt_ref, xs_ref, slab, zbuf, sem):
    blk = pl.program_id(0)
    tb = u_ref.shape[0]
    n_slots = slab.shape[1]
    e1 = rt_ref[0, 0:1, :]
    e2 = rt_ref[0, 1:2, :]
    sub_e = lax.broadcasted_iota(jnp.int32, (N_EXPERTS, tb), 0).astype(F32)
    hit1 = e1 == sub_e
    hit2 = e2 == sub_e
    onehot = jnp.where(hit1 | hit2, 1.0, 0.0).astype(BF16)
    r = lax.broadcasted_iota(jnp.int32, (tb, tb), 0)
    c = lax.broadcasted_iota(jnp.int32, (tb, tb), 1)
    lower = jnp.where(c < r, 1.0, 0.0).astype(BF16)
    rank = _slot_positions(onehot, lower, transposed=False)
    sub1 = lax.broadcasted_iota(jnp.int32, (N_EXPERTS, 1), 0)
    offv = jnp.zeros((N_EXPERTS, 1), F32)
    for e in range(N_EXPERTS):
        offv = jnp.where(sub1 == e, off_s[blk * N_EXPERTS + e].astype(F32), offv)
    pos = offv + rank
    pos1 = jnp.sum(jnp.where(hit1, pos, 0.0), axis=0, keepdims=True)
    pos2 = jnp.sum(jnp.where(hit2, pos, 0.0), axis=0, keepdims=True)
    slot = lax.broadcasted_iota(jnp.int32, (n_slots, tb), 0).astype(F32)
    perm = jnp.where((slot == pos1) | (slot == pos2), 1.0, 0.0).astype(BF16)
    cur = blk % 2
    slab[cur] = _pack_bf16_pairs(jnp.dot(perm, u_ref[...], preferred_element_type=F32))

    @pl.when(blk == 0)
    def _():
        zbuf[...] = jnp.zeros_like(zbuf)

    def start_segments(b, slot):
        for e in range(N_EXPERTS):
            idx = b * N_EXPERTS + e
            src0, dst0 = off_s[idx], gb_s[idx]

            def piece(hi, size, src0=src0, dst0=dst0):
                pltpu.make_async_copy(slab.at[slot, pl.ds(pl.multiple_of(src0 + hi, SEG_ALIGN), size)],
                                      xs_ref.at[pl.ds(pl.multiple_of(dst0 + hi, SEG_ALIGN), size)],
                                      sem.at[slot]).start()
            _segment_pieces(pc_s[idx], piece)

    def wait_segments(b, slot):
        def wait_rows(n_static):
            pltpu.make_async_copy(slab.at[slot, pl.ds(0, n_static)], xs_ref.at[pl.ds(0, n_static)],
                                  sem.at[slot]).wait()
        wait_rows(2 * DISPATCH_BLOCK)
        _segment_pieces(rows_s[b] - 2 * DISPATCH_BLOCK, lambda hi, size: wait_rows(size))

    def fill_copies(slot, go):
        for e in range(N_EXPERTS):
            dst0 = gs_s[e]

            def piece(hi, size, dst0=dst0):
                go(pltpu.make_async_copy(zbuf.at[pl.ds(0, size)],
                                         xs_ref.at[pl.ds(pl.multiple_of(dst0 + hi, SEG_ALIGN), size)], sem.at[slot]))
            _segment_pieces(gl_s[e], piece)

        zrows = zbuf.shape[0]

        def tail_chunk(i, carry):
            go(pltpu.make_async_copy(zbuf, xs_ref.at[pl.ds(pl.multiple_of(tail_s[0] + i * zrows, zrows), zrows)],
                                     sem.at[slot]))
            return carry
        lax.fori_loop(0, tail_s[1], tail_chunk, 0)

    last = pl.num_programs(0) - 1
    start_segments(blk, cur)

    @pl.when(blk == 0)
    def _():
        fill_copies(cur, lambda cp: cp.start())

    @pl.when(blk > 0)
    def _():
        wait_segments(blk - 1, 1 - cur)

    @pl.when(blk == jnp.minimum(1, last))
    def _():
        fill_copies(0, lambda cp: cp.wait())

    @pl.when(blk == last)
    def _():
        wait_segments(blk, cur)


def _dispatch_call(meta, u2, rt, n_rows):
    n, d = u2.shape
    tb = DISPATCH_BLOCK
    per_tile = rt.shape[2] // tb
    grid_spec = pltpu.PrefetchScalarGridSpec(
        num_scalar_prefetch=7,
        grid=(n // tb,),
        in_specs=[pl.BlockSpec((tb, d), lambda i, *_: (i, 0)),
                  pl.BlockSpec((1, 8, tb), lambda i, *_: (i // per_tile, 0, i % per_tile))],
        out_specs=pl.BlockSpec(memory_space=pl.ANY),
        scratch_shapes=[pltpu.VMEM((2, DISPATCH_SLOTS, d // 2), jnp.uint32),
                        pltpu.VMEM((SEG_ALIGN << (SEG_PIECE_BITS - 1), d // 2), jnp.uint32),
                        pltpu.SemaphoreType.DMA((2,))],
    )
    return pl.pallas_call(
        _dispatch_kernel,
        grid_spec=grid_spec,
        out_shape=jax.ShapeDtypeStruct((n_rows, d // 2), jnp.uint32),
        compiler_params=_cparams(("arbitrary",)),
        name="moe_dispatch",
    )(meta["off"], meta["pc"], meta["gbase"], meta["rows"], meta["gap_start"], meta["gap_len"], meta["tail"], u2, rt)


def _experts_kernel(te_s, na_s, x_ref, wg_ref, wu_ref, wd_ref, y_ref, wg_bf, wu_bf, wd_bf):
    i = pl.program_id(0)
    active = i < na_s[0]

    @pl.when(active & ((i == 0) | (te_s[i] != te_s[jnp.maximum(i - 1, 0)])))
    def _():
        wg_bf[...] = wg_ref[0, 0].astype(BF16)
        wu_bf[...] = wu_ref[0, 0].astype(BF16)
        wd_bf[...] = wd_ref[0, 0].astype(BF16)

    @pl.when(active)
    def _():
        x = _unpack_bf16_pairs(x_ref[...])
        gate = jnp.dot(x, wg_bf[...], preferred_element_type=F32)
        up = jnp.dot(x, wu_bf[...], preferred_element_type=F32)
        hidden = (gate * _sigmoid(gate) * up).astype(BF16)
        y_ref[...] = _pack_bf16_pairs(jnp.dot(hidden, wd_bf[...], preferred_element_type=F32))

    @pl.when(jnp.logical_not(active))
    def _():
        y_ref[...] = jnp.zeros_like(y_ref)


def _experts_call(meta, xs, layer, wg, wu, wd):
    n_rows = xs.shape[0]
    tm = TM_MOE
    _, _, d, de = wg.shape
    tile = lambda i, te, na: jnp.minimum(i, na[0] - 1)
    grid_spec = pltpu.PrefetchScalarGridSpec(
        num_scalar_prefetch=2,
        grid=(n_rows // tm,),
        in_specs=[pl.BlockSpec((tm, d // 2), lambda i, te, na: (tile(i, te, na), 0)),
                  pl.BlockSpec((1, 1, d, de), lambda i, te, na: (layer, te[tile(i, te, na)], 0, 0)),
                  pl.BlockSpec((1, 1, d, de), lambda i, te, na: (layer, te[tile(i, te, na)], 0, 0)),
                  pl.BlockSpec((1, 1, de, d), lambda i, te, na: (layer, te[tile(i, te, na)], 0, 0))],
        out_specs=pl.BlockSpec((tm, d // 2), lambda i, te, na: (i, 0)),
        scratch_shapes=[pltpu.VMEM((d, de), BF16), pltpu.VMEM((d, de), BF16), pltpu.VMEM((de, d), BF16)],
    )
    return pl.pallas_call(
        _experts_kernel,
        grid_spec=grid_spec,
        out_shape=jax.ShapeDtypeStruct((n_rows, d // 2), jnp.uint32),
        compiler_params=_cparams(("arbitrary",)),
        name="experts",
    )(meta["tile_expert"], meta["n_active"], xs, wg, wu, wd)


def _combine_kernel(off_s, pc_s, gb_s, rows_s, rtc_ref, x1_ref, mod_ref, lng_ref, lnb_ref, ys_ref, o_ref, slab, sem):
    blk = pl.program_id(0)
    tb = x1_ref.shape[0]
    n_slots = slab.shape[1]
    cur = blk % 2

    def start_segments(b, slot):
        for e in range(N_EXPERTS):
            idx = b * N_EXPERTS + e
            dst0, src0 = off_s[idx], gb_s[idx]

            def piece(hi, size, src0=src0, dst0=dst0):
                pltpu.make_async_copy(ys_ref.at[pl.ds(pl.multiple_of(src0 + hi, SEG_ALIGN), size)],
                                      slab.at[slot, pl.ds(pl.multiple_of(dst0 + hi, SEG_ALIGN), size)],
                                      sem.at[slot]).start()
            _segment_pieces(pc_s[idx], piece)

    def wait_segments(b, slot):
        def wait_rows(n_static):
            pltpu.make_async_copy(ys_ref.at[pl.ds(0, n_static)], slab.at[slot, pl.ds(0, n_static)],
                                  sem.at[slot]).wait()
        wait_rows(2 * DISPATCH_BLOCK)
        _segment_pieces(rows_s[b] - 2 * DISPATCH_BLOCK, lambda hi, size: wait_rows(size))

    @pl.when(blk == 0)
    def _():
        slab[...] = jnp.zeros_like(slab)
        start_segments(blk, cur)

    @pl.when(blk + 1 < pl.num_programs(0))
    def _():
        start_segments(blk + 1, 1 - cur)

    rtc = rtc_ref[...]
    e1, e2, g1, g2 = rtc[:, 0:1], rtc[:, 1:2], rtc[:, 2:3], rtc[:, 3:4]
    lane_e = lax.broadcasted_iota(jnp.int32, (1, LANES), 1).astype(F32)
    hit1 = e1 == lane_e
    hit2 = e2 == lane_e
    onehot = jnp.where(hit1 | hit2, 1.0, 0.0).astype(BF16)
    r = lax.broadcasted_iota(jnp.int32, (tb, tb), 0)
    c = lax.broadcasted_iota(jnp.int32, (tb, tb), 1)
    lower = jnp.where(c < r, 1.0, 0.0).astype(BF16)
    rank = _slot_positions(onehot, lower, transposed=True)
    lane1 = lax.broadcasted_iota(jnp.int32, (1, LANES), 1)
    offv = jnp.zeros((1, LANES), F32)
    for e in range(N_EXPERTS):
        offv = jnp.where(lane1 == e, off_s[blk * N_EXPERTS + e].astype(F32), offv)
    pos = offv + rank
    pos1 = jnp.sum(jnp.where(hit1, pos, 0.0), axis=-1, keepdims=True)
    pos2 = jnp.sum(jnp.where(hit2, pos, 0.0), axis=-1, keepdims=True)
    slot = lax.broadcasted_iota(jnp.int32, (1, n_slots), 1).astype(F32)
    sel1 = jnp.where(slot == pos1, 1.0, 0.0).astype(BF16)
    sel2 = jnp.where(slot == pos2, 1.0, 0.0).astype(BF16)

    wait_segments(blk, cur)
    ys = _unpack_bf16_pairs(slab[cur])
    y = (g1 * jnp.dot(sel1, ys, preferred_element_type=F32) + g2 * jnp.dot(sel2, ys, preferred_element_type=F32))
    mod = mod_ref[0]
    o_ref[...] = _layer_norm(ALPHA * x1_ref[...] + (1.0 + mod[5:6, :]) * y, lng_ref[...], lnb_ref[...])


def _combine_call(meta, ys, rtc, x1, mod_l, lng, lnb, seq):
    n, d = x1.shape
    tb = DISPATCH_BLOCK
    bps = seq // tb
    grid_spec = pltpu.PrefetchScalarGridSpec(
        num_scalar_prefetch=4,
        grid=(n // tb,),
        in_specs=[pl.BlockSpec((tb, LANES), lambda i, *_: (i, 0)),
                  pl.BlockSpec((tb, d), lambda i, *_: (i, 0)),
                  pl.BlockSpec((1, 6, d), lambda i, *_: (i // bps, 0, 0)),
                  pl.BlockSpec((1, d), lambda i, *_: (0, 0)),
                  pl.BlockSpec((1, d), lambda i, *_: (0, 0)),
                  pl.BlockSpec(memory_space=pl.ANY)],
        out_specs=pl.BlockSpec((tb, d), lambda i, *_: (i, 0)),
        scratch_shapes=[pltpu.VMEM((2, DISPATCH_SLOTS, d // 2), jnp.uint32), pltpu.SemaphoreType.DMA((2,))],
    )
    return pl.pallas_call(
        _combine_kernel,
        grid_spec=grid_spec,
        out_shape=jax.ShapeDtypeStruct((n, d), F32),
        compiler_params=_cparams(("arbitrary",)),
        name="moe_combine_ln2",
    )(meta["off"], meta["pc"], meta["gbase"], meta["rows"], rtc, x1, mod_l, lng, lnb, ys)


def _sorted_rows_bound(n_tokens):
    n_blocks = n_tokens // DISPATCH_BLOCK
    worst = 2 * n_tokens + n_blocks * N_EXPERTS * (SEG_ALIGN - 1) + N_EXPERTS * (TM_MOE - 1)
    return -(-worst // TM_MOE) * TM_MOE


def _route_meta(cnt, n_rows):
    pc = (cnt + SEG_ALIGN - 1) // SEG_ALIGN * SEG_ALIGN
    off = jnp.cumsum(pc, axis=1) - pc
    tot = pc.sum(axis=0)
    reg = (tot + TM_MOE - 1) // TM_MOE * TM_MOE
    reg_end = jnp.cumsum(reg)
    ebase = reg_end - reg
    gbase = ebase[None, :] + jnp.cumsum(pc, axis=0) - pc
    tiles = jnp.arange(n_rows // TM_MOE, dtype=jnp.int32)
    tile_expert = jnp.minimum(jnp.sum(tiles[:, None] >= (reg_end // TM_MOE)[None, :], axis=1), N_EXPERTS - 1)
    i32 = lambda a: a.astype(jnp.int32).reshape(-1)
    used = reg_end[-1:]
    tail_chunk = SEG_ALIGN << (SEG_PIECE_BITS - 1)
    return {"off": i32(off), "pc": i32(pc), "gbase": i32(gbase), "rows": i32(pc.sum(axis=1)),
            "gap_start": i32(ebase + tot),
            "gap_len": i32(reg - tot), "tail": i32(jnp.concatenate([used, (n_rows - used) // tail_chunk])),
            "tile_expert": i32(tile_expert), "n_active": i32(used // TM_MOE)}


def _rearranged_w_in(w_in_l):
    d = w_in_l.shape[0]
    o_gate, o_cb, o_ga = ZM, ZM + 2 * HEADS, ZM + 2 * HEADS + ZC + ZQ + ZK + ZVP + ZG
    cols = [w_in_l[:, 0:ZM], w_in_l[:, o_cb:o_ga],
            w_in_l[:, o_gate:o_gate + 2 * HEADS], w_in_l[:, o_ga:o_ga + GLA_RANK],
            jnp.zeros((d, ZS - 2 * HEADS - GLA_RANK), w_in_l.dtype)]
    return jnp.concatenate(cols, axis=1).astype(BF16)


def _rope_tables(positions):
    inv_freq = ROPE_THETA ** (-jnp.arange(0, ROPE_DIM, 2, dtype=F32) / ROPE_DIM)
    ang = positions.astype(F32).reshape(-1, 1) * inv_freq
    cos, sin = jnp.cos(ang), jnp.sin(ang)
    d = np.arange(ZQ) % DIFF_DIM
    half = ROPE_DIM // 2
    idx = d % half
    cosf = jnp.where(d < ROPE_DIM, cos[:, idx], 1.0)
    sinf = jnp.where(d < half, -sin[:, idx], jnp.where(d < ROPE_DIM, sin[:, idx], 0.0))
    return cosf, sinf


def _pad_lanes(v, width=LANES):
    v = v.reshape(1, -1)
    return jnp.pad(v, ((0, 0), (0, width - v.shape[1])))


def kernel(x, c, positions, w_ada, b_ada, w_in, w_out, mlstm_b_i, mlstm_b_f, mlstm_norm_g, conv_w, conv_b, diff_lambda_q1, diff_lambda_k1, diff_lambda_q2, diff_lambda_k2, diff_norm_g, gla_w_a2, gla_b_a, gla_norm_g, ln1_g, ln1_b, ln2_g, ln2_b, router_w, router_bias, exp_w_gate, exp_w_up, exp_w_down):
    bsz, seq, d = x.shape
    depth = w_in.shape[0]
    n = bsz * seq
    x2 = x.reshape(n, d)
    mod = _mod_call(c, w_ada, b_ada)
    cosf, sinf = _rope_tables(positions)
    rw_t = router_w.T.astype(BF16)
    rb = router_bias.reshape(N_EXPERTS, 1).astype(F32)
    n_sorted = _sorted_rows_bound(n)
    for l in range(depth):
        lam_init = 0.8 - 0.6 * math.exp(-0.3 * l)
        w_l = _rearranged_w_in(w_in[l])
        zm, zc, zq, zk, zvt, zg, zs = _inproj_call(x2, mod[l], w_l, cosf, sinf, seq)
        bif = _pad_lanes(jnp.concatenate([mlstm_b_i[l], mlstm_b_f[l]]))
        ya = _mlstm_call(zm, zs, bif, mlstm_norm_g[l].reshape(1, -1), bsz, seq)
        lamp = jnp.concatenate([_pad_lanes(diff_lambda_q1[l]), _pad_lanes(diff_lambda_k1[l]),
                                _pad_lanes(diff_lambda_q2[l]), _pad_lanes(diff_lambda_k2[l])], axis=0)
        yc = _attn_call(zq, zk, zvt, lamp, diff_norm_g[l].reshape(1, -1), lam_init, bsz, seq)
        wa = jnp.pad(gla_w_a2[l], ((2 * HEADS, ZS - 2 * HEADS - GLA_RANK), (0, 0))).astype(BF16)
        yd = _gla_call(zg, zs, wa, gla_b_a[l].reshape(1, -1), gla_norm_g[l].reshape(1, -1), bsz, seq)
        x1, u2, rt, rtc, cnt = _outproj_call(ya, zc, yc, yd, x2, mod[l], w_out[l].astype(BF16), conv_w[l],
                                             conv_b[l].reshape(1, -1), ln1_g[l].reshape(1, -1),
                                             ln1_b[l].reshape(1, -1), rw_t, rb, seq)
        per_tile = TM_PROJ // DISPATCH_BLOCK
        cnt_blk = cnt[:, :, :per_tile].transpose(0, 2, 1).reshape(-1, N_EXPERTS).astype(jnp.int32)
        meta = _route_meta(cnt_blk, n_sorted)
        xs = _dispatch_call(meta, u2, rt, n_sorted)
        ys = _experts_call(meta, xs, l, exp_w_gate, exp_w_up, exp_w_down)
        x2 = _combine_call(meta, ys, rtc, x1, mod[l], ln2_g[l].reshape(1, -1), ln2_b[l].reshape(1, -1), seq)
    return x2.reshape(bsz, seq, d)
```

```python
import functools
import math

import numpy as np
import jax
import jax.numpy as jnp
from jax import lax
from jax.experimental import pallas as pl
from jax.experimental.pallas import tpu as pltpu

F32 = jnp.float32
BF16 = jnp.bfloat16

DEPTH = 4
HEADS = 4
GW = 256
HEAD_DIM = 64
DIFF_DIM = 32
ROPE_DIM = 8
ROPE_THETA = 500000.0
CONV_WIDTH = 3
GLA_RANK = 16
GLA_TAU = 16.0
CHUNK = 64
GLA_CHUNK = 64
N_EXPERTS = 16
N_GROUPS = 4
GROUP_SIZE = 4
ALPHA = (2 * DEPTH) ** 0.25
EPS = 1e-6
LANES = 128
SUBLANES = 8
NEG_INF = float("-inf")
LOG2_E = math.log2(math.e)

ZM, ZC, ZQ, ZK, ZVP, ZG, ZS = 1024, 768, 256, 256, 256, 1024, 128
ZV = HEADS * LANES
Z_OFFS = np.cumsum([0, ZM, ZC, ZQ, ZK, ZVP, ZG, ZS])
Z_TOTAL = int(Z_OFFS[-1])

TM_PROJ = 1024
TB_MLSTM = 256
TB_GLA = 1024
CUM_SPAN = 128
TQ_ATTN = 512
TM_MOE = 512
DISPATCH_BLOCK = 256
SEG_ALIGN = 8
SEG_PIECE_BITS = 6
DISPATCH_SLOTS = 2 * DISPATCH_BLOCK + N_EXPERTS * SEG_ALIGN
assert SEG_ALIGN << (SEG_PIECE_BITS - 1) >= DISPATCH_BLOCK and SEG_ALIGN << SEG_PIECE_BITS >= TM_MOE
VMEM_LIMIT = 56 * 1024 * 1024


def _cparams(sem):
    return pltpu.CompilerParams(dimension_semantics=sem, vmem_limit_bytes=VMEM_LIMIT)


def _log_sigmoid(x):
    return jnp.minimum(x, 0.0) - jnp.log(1.0 + jnp.exp(-jnp.abs(x)))


def _sigmoid(x):
    return 1.0 / (1.0 + jnp.exp(-x))


def _layer_norm(v, g, b):
    mu = jnp.mean(v, axis=-1, keepdims=True)
    d = v - mu
    var = jnp.mean(d * d, axis=-1, keepdims=True)
    return d * lax.rsqrt(var + EPS) * g + b


def _mod_kernel(c_ref, w_ref, b_ref, o_ref):
    c = c_ref[...]
    c_act = (c * _sigmoid(c)).astype(BF16)
    o_ref[0, 0] = jnp.dot(c_act, w_ref[0].astype(BF16), preferred_element_type=F32) + b_ref[0, 0]


def _mod_call(c, w_ada, b_ada):
    depth, d, _ = w_ada.shape
    bsz = c.shape[0]
    b3 = b_ada.reshape(depth, 6, 1, d)
    out = pl.pallas_call(
        _mod_kernel,
        grid=(depth, 6),
        in_specs=[pl.BlockSpec((bsz, d), lambda l, j: (0, 0)),
                  pl.BlockSpec((1, d, d), lambda l, j: (l, 0, j)),
                  pl.BlockSpec((1, 1, 1, d), lambda l, j: (l, j, 0, 0))],
        out_specs=pl.BlockSpec((1, 1, bsz, d), lambda l, j: (l, j, 0, 0)),
        out_shape=jax.ShapeDtypeStruct((depth, 6, bsz, d), F32),
        compiler_params=_cparams(("arbitrary", "arbitrary")),
        name="adaln_mod",
    )(c, w_ada, b3)
    return out.transpose(0, 2, 1, 3)


def _inproj_kernel(x_ref, mod_ref, w_ref, cos_ref, sin_ref, zm_ref, zc_ref, zq_ref, zk_ref, zvt_ref, zg_ref,
                   zs_ref):
    mod = mod_ref[0]
    u = (x_ref[...] * (1.0 + mod[1:2, :]) + mod[0:1, :]).astype(BF16)

    def proj(idx):
        return jnp.dot(u, w_ref[:, int(Z_OFFS[idx]):int(Z_OFFS[idx + 1])], preferred_element_type=F32)

    zm_ref[...] = proj(0).astype(BF16)
    zc_ref[...] = proj(1).astype(BF16)

    lane = lax.broadcasted_iota(jnp.int32, (1, ZQ), 1)
    first_half = (lane % ROPE_DIM) < (ROPE_DIM // 2)
    cosf = cos_ref[...]
    sinf = sin_ref[...]

    def rope(r):
        swapped = jnp.where(first_half, pltpu.roll(r, ZQ - ROPE_DIM // 2, 1), pltpu.roll(r, ROPE_DIM // 2, 1))
        return r * cosf + swapped * sinf

    zq_ref[...] = (rope(proj(2)) * (LOG2_E * DIFF_DIM ** -0.5)).astype(BF16)
    zk_ref[...] = rope(proj(3)).astype(BF16)
    vt = proj(4).T
    tm = vt.shape[1]
    ones_rows = jnp.where(lax.broadcasted_iota(jnp.int32, (LANES - HEAD_DIM, tm), 0) == 0, 1.0, 0.0)
    zvt_ref[...] = jnp.concatenate([blk for h in range(HEADS)
                                    for blk in (vt[h * HEAD_DIM:(h + 1) * HEAD_DIM], ones_rows)], axis=0).astype(BF16)
    zg_ref[...] = proj(5).astype(BF16)
    zs_ref[...] = proj(6)


def _inproj_call(x2, mod_l, w_l, cosf, sinf, seq):
    n, d = x2.shape
    tm = TM_PROJ
    tps = seq // tm
    row = lambda w: pl.BlockSpec((tm, w), lambda i: (i, 0))
    outs = [(ZM, BF16), (ZC, BF16), (ZQ, BF16), (ZK, BF16), (ZV, BF16), (ZG, BF16), (ZS, F32)]
    return pl.pallas_call(
        _inproj_kernel,
        grid=(n // tm,),
        in_specs=[row(d),
                  pl.BlockSpec((1, 6, d), lambda i: (i // tps, 0, 0)),
                  pl.BlockSpec((d, Z_TOTAL), lambda i: (0, 0)),
                  row(ZQ), row(ZQ)],
        out_specs=[pl.BlockSpec((ZV, tm), lambda i: (0, i)) if k == 4 else row(w) for k, (w, _) in enumerate(outs)],
        out_shape=[jax.ShapeDtypeStruct((ZV, n) if k == 4 else (n, w), dt) for k, (w, dt) in enumerate(outs)],
        compiler_params=_cparams(("parallel",)),
        name="in_proj",
    )(x2, mod_l, w_l, cosf, sinf)


NT_DIMS = (((1,), (1,)), ((), ()))
TN_DIMS = (((0,), (0,)), ((), ()))


def _head_of(idx):
    return idx // HEAD_DIM


def _cumsum_matrix(span, chunk):
    r = lax.broadcasted_iota(jnp.int32, (span, span), 0)
    c = lax.broadcasted_iota(jnp.int32, (span, span), 1)
    return jnp.where(((r // chunk) == (c // chunk)) & (c <= r), 1.0, 0.0).astype(BF16)


def _bf16_terms(x, n_terms):
    terms, rest = [], x
    for _ in range(n_terms):
        t = rest.astype(BF16)
        terms.append(t)
        rest = rest - t.astype(F32)
    return terms


def _exact_matmul(lhs_f32, rhs01, n_terms=3):
    return functools.reduce(lambda a, b: a + b, [jnp.dot(t, rhs01, preferred_element_type=F32)
                                                 for t in reversed(_bf16_terms(lhs_f32, n_terms))])


def _chunk_cumsum(x, chunk):
    cm = _cumsum_matrix(CUM_SPAN, chunk)
    outs = []
    for i in range(x.shape[0] // CUM_SPAN):
        terms = _bf16_terms(x[i * CUM_SPAN:(i + 1) * CUM_SPAN], 3)
        outs.append(functools.reduce(lambda a, b: a + b,
                                     [jnp.dot(cm, t, preferred_element_type=F32) for t in reversed(terms)]))
    return jnp.concatenate(outs, axis=0)


def _rows_broadcast(x, chunk, row):
    n = x.shape[0] // chunk
    return jnp.concatenate([jnp.broadcast_to(x[c * chunk + row:c * chunk + row + 1, :], (chunk, x.shape[1]))
                            for c in range(n)], axis=0)


def _head_masks(chunk):
    lane_head = _head_of(lax.broadcasted_iota(jnp.int32, (1, GW), 1))
    row_head = lax.broadcasted_iota(jnp.int32, (HEADS * chunk, 1), 0) // chunk
    stack_keep = row_head == lane_head
    rr = lax.broadcasted_iota(jnp.int32, (HEADS * chunk, chunk), 0) % chunk
    cc = lax.broadcasted_iota(jnp.int32, (HEADS * chunk, chunk), 1)
    block_diag = (_head_of(lax.broadcasted_iota(jnp.int32, (GW, GW), 0))
                  == _head_of(lax.broadcasted_iota(jnp.int32, (GW, GW), 1)))
    return stack_keep, cc <= rr, block_diag


def _unstack_heads(x4, keep, chunk):
    x4 = jnp.where(keep, x4, 0.0)
    return functools.reduce(lambda a, b: a + b, [x4[h * chunk:(h + 1) * chunk] for h in range(HEADS)])


def _head_rms_norm(x, block_diag):
    seg = jnp.where(block_diag, 1.0, 0.0).astype(BF16)
    return x * lax.rsqrt(_exact_matmul(x * x, seg, 2) * (1.0 / HEAD_DIM) + EPS)


def _mlstm_kernel(zm_ref, zs_ref, bif_ref, g_ref, o_ref, ct_scr, nm_scr, ni_scr, up_scr):
    @pl.when(pl.program_id(1) == 0)
    def _():
        ct_scr[...] = jnp.zeros_like(ct_scr)
        nm_scr[...] = jnp.zeros_like(nm_scr)

    tb = zm_ref.shape[0]
    L = CHUNK
    nc = tb // L
    lane = lax.broadcasted_iota(jnp.int32, (1, LANES), 1)
    graw = zs_ref[...] + bif_ref[...]
    gp = jnp.where(lane < HEADS, graw, jnp.where(lane < 2 * HEADS, _log_sigmoid(graw), 0.0))
    cum = _chunk_cumsum(gp, L)

    def expand(x, first_lane):
        sel = jnp.where((lax.broadcasted_iota(jnp.int32, (LANES, GW), 0) - first_lane)
                        == _head_of(lax.broadcasted_iota(jnp.int32, (LANES, GW), 1)), 1.0, 0.0).astype(BF16)
        return _exact_matmul(x, sel)

    bx = expand(cum, HEADS)
    lix = expand(gp, 0)
    bex = _rows_broadcast(bx, L, L - 1)
    dkx = bex - bx + lix
    dkmax = [jnp.max(dkx[c * L:(c + 1) * L], axis=0, keepdims=True) for c in range(nc)]
    kw = (zm_ref[:, GW:2 * GW].astype(F32)
          * jnp.exp(dkx - jnp.concatenate([jnp.broadcast_to(d, (L, GW)) for d in dkmax], axis=0))).astype(BF16)
    cum_t = cum.T
    gfull = bx - lix
    stack_keep, _, block_diag = _head_masks(L)
    stack_keep2 = jnp.concatenate([stack_keep, stack_keep], axis=1)
    causal_t = (lax.broadcasted_iota(jnp.int32, (L, HEADS * L), 0)
                <= lax.broadcasted_iota(jnp.int32, (L, HEADS * L), 1) % L)
    ones_blk = jnp.ones((L, GW), BF16)

    def chunk_cummax(x):
        shift = 1
        while shift < L:
            x = jnp.maximum(x, jnp.concatenate([jnp.full((shift, x.shape[1]), NEG_INF, F32), x[:L - shift]], axis=0))
            shift *= 2
        return x

    dmaxx = []
    for c in range(nc):
        rs = slice(c * L, (c + 1) * L)
        qc = zm_ref[rs, 0:GW] * (HEAD_DIM ** -0.5)
        kc = zm_ref[rs, GW:2 * GW]
        vc = zm_ref[rs, 2 * GW:3 * GW]
        q4 = jnp.where(stack_keep, jnp.concatenate([qc] * HEADS, axis=0), jnp.zeros((), BF16))
        st = lax.dot_general(kc, q4, NT_DIMS, preferred_element_type=F32)
        b_query = jnp.concatenate([cum_t[HEADS + h:HEADS + h + 1, rs] for h in range(HEADS)], axis=1)
        dt = jnp.where(causal_t, b_query - gfull[rs], NEG_INF)
        dmax_t = jnp.max(dt, axis=0, keepdims=True)
        pt = (st * jnp.exp(dt - dmax_t)).astype(BF16)
        i4 = lax.dot_general(pt, jnp.concatenate([vc, ones_blk], axis=1), TN_DIMS, preferred_element_type=F32)
        ni_scr[rs, :] = _unstack_heads(i4, stack_keep2, L)
        dmaxx.append(bx[rs] + chunk_cummax(-gfull[rs]))
        up = lax.dot_general(vc, kw[rs], TN_DIMS, preferred_element_type=F32)
        up_scr[c] = jnp.where(block_diag, up, 0.0)

    ct = ct_scr[...]
    n_row = nm_scr[0:1, :]
    m_prev = nm_scr[1:2, :]
    hv_rows = []
    for c in range(nc):
        rs = slice(c * L, (c + 1) * L)
        qc = zm_ref[rs, 0:GW] * (HEAD_DIM ** -0.5)
        nmat = jnp.where(block_diag, jnp.broadcast_to(n_row, (GW, GW)), 0.0).astype(BF16)
        num_inter = lax.dot_general(qc, ct.astype(BF16), NT_DIMS, preferred_element_type=F32)
        den_inter = lax.dot_general(qc, nmat, NT_DIMS, preferred_element_type=F32)
        inter = bx[rs] + m_prev
        m_row = jnp.maximum(inter, dmaxx[c])
        e1 = jnp.exp(dmaxx[c] - m_row)
        e2 = jnp.exp(inter - m_row)
        num = e1 * ni_scr[rs, 0:GW] + e2 * num_inter
        den = e1 * ni_scr[rs, GW:2 * GW] + e2 * den_inter
        hv_rows.append(num / jnp.maximum(jnp.abs(den), jnp.exp(-m_row)))
        b_end = bex[c * L:c * L + 1, :]
        m_new = jnp.maximum(b_end + m_prev, dkmax[c])
        decay = jnp.exp(b_end + m_prev - m_new)
        f2 = jnp.exp(dkmax[c] - m_new)
        ct = ct * decay + up_scr[c] * f2
        n_row = n_row * decay + f2 * jnp.sum(kw[rs].astype(F32), axis=0, keepdims=True)
        m_prev = m_new
    ct_scr[...] = ct
    nm_scr[0:1, :] = n_row
    nm_scr[1:2, :] = m_prev

    hn = _head_rms_norm(jnp.concatenate(hv_rows, axis=0), block_diag)
    o_ref[...] = (hn * g_ref[...] * _sigmoid(zm_ref[:, 3 * GW:4 * GW].astype(F32))).astype(BF16)


def _mlstm_call(zm, zs, bif, gain, bsz, seq):
    tb = TB_MLSTM
    tps = seq // tb
    return pl.pallas_call(
        _mlstm_kernel,
        grid=(bsz, tps),
        in_specs=[pl.BlockSpec((tb, ZM), lambda b, t: (b * tps + t, 0)),
                  pl.BlockSpec((tb, ZS), lambda b, t: (b * tps + t, 0)),
                  pl.BlockSpec((1, LANES), lambda b, t: (0, 0)),
                  pl.BlockSpec((1, GW), lambda b, t: (0, 0))],
        out_specs=pl.BlockSpec((tb, GW), lambda b, t: (b * tps + t, 0)),
        out_shape=jax.ShapeDtypeStruct((bsz * seq, GW), BF16),
        scratch_shapes=[pltpu.VMEM((GW, GW), F32),
                        pltpu.VMEM((8, GW), F32),
                        pltpu.VMEM((tb, 2 * GW), F32),
                        pltpu.VMEM((tb // CHUNK, GW, GW), F32)],
        compiler_params=_cparams(("parallel", "arbitrary")),
        name="mlstm",
    )(zm, zs, bif, jnp.tile(gain, (1, HEADS)))


def _gla_kernel(zg_ref, zs_ref, wa_ref, ba_ref, g_ref, o_ref, st_scr, oi_scr, up_scr):
    @pl.when(pl.program_id(1) == 0)
    def _():
        st_scr[...] = jnp.zeros_like(st_scr)

    tb = zg_ref.shape[0]
    lg = GLA_CHUNK
    nc = tb // lg
    la = _log_sigmoid(jnp.dot(zs_ref[...].astype(BF16), wa_ref[...], preferred_element_type=F32)
                      + ba_ref[...]) * (1.0 / GLA_TAU)
    bc = _chunk_cumsum(la, lg)
    be = _rows_broadcast(bc, lg, lg - 1)
    bm = _rows_broadcast(bc, lg, lg // 2)
    q = zg_ref[:, 0:GW].astype(F32) * (HEAD_DIM ** -0.5)
    k = zg_ref[:, GW:2 * GW].astype(F32)
    qm = (q * jnp.exp(bc - bm)).astype(BF16)
    km = (k * jnp.exp(bm - bc)).astype(BF16)
    qt = (q * jnp.exp(bc)).astype(BF16)
    ks = (k * jnp.exp(be - bc)).astype(BF16)
    dec = jnp.exp(be)
    stack_keep, causal4, block_diag = _head_masks(lg)

    for c in range(nc):
        rs = slice(c * lg, (c + 1) * lg)
        vc = zg_ref[rs, 2 * GW:3 * GW]
        q4 = jnp.where(stack_keep, jnp.concatenate([qm[rs]] * HEADS, axis=0), jnp.zeros((), BF16))
        att4 = lax.dot_general(q4, km[rs], NT_DIMS, preferred_element_type=F32)
        att4 = jnp.where(causal4, att4, 0.0).astype(BF16)
        oi_scr[rs, :] = _unstack_heads(jnp.dot(att4, vc, preferred_element_type=F32), stack_keep, lg)
        up = lax.dot_general(vc, ks[rs], TN_DIMS, preferred_element_type=F32)
        up_scr[c] = jnp.where(block_diag, up, 0.0)

    st = st_scr[...]
    inter = []
    for c in range(nc):
        rs = slice(c * lg, (c + 1) * lg)
        inter.append(lax.dot_general(qt[rs], st.astype(BF16), NT_DIMS, preferred_element_type=F32))
        st = st * dec[c * lg:c * lg + 1, :] + up_scr[c]
    st_scr[...] = st

    on = _head_rms_norm(oi_scr[...] + jnp.concatenate(inter, axis=0), block_diag)
    r = zg_ref[:, 3 * GW:4 * GW].astype(F32)
    o_ref[...] = (on * g_ref[...] * (r * _sigmoid(r))).astype(BF16)


def _gla_call(zg, zs, wa, ba, gain, bsz, seq):
    tb = TB_GLA
    tps = seq // tb
    return pl.pallas_call(
        _gla_kernel,
        grid=(bsz, tps),
        in_specs=[pl.BlockSpec((tb, ZG), lambda b, t: (b * tps + t, 0)),
                  pl.BlockSpec((tb, ZS), lambda b, t: (b * tps + t, 0)),
                  pl.BlockSpec((ZS, GW), lambda b, t: (0, 0)),
                  pl.BlockSpec((1, GW), lambda b, t: (0, 0)),
                  pl.BlockSpec((1, GW), lambda b, t: (0, 0))],
        out_specs=pl.BlockSpec((tb, GW), lambda b, t: (b * tps + t, 0)),
        out_shape=jax.ShapeDtypeStruct((bsz * seq, GW), BF16),
        scratch_shapes=[pltpu.VMEM((GW, GW), F32),
                        pltpu.VMEM((tb, GW), F32),
                        pltpu.VMEM((tb // GLA_CHUNK, GW, GW), F32)],
        compiler_params=_cparams(("parallel", "arbitrary")),
        name="gla",
    )(zg, zs, wa, ba, jnp.tile(gain, (1, HEADS)))


def _attn_kernel(lam_init, q_ref, k_ref, vt_ref, lamp_ref, g_ref, o_ref, qm_scr, m_scr, acc_scr, st_scr):
    tq = q_ref.shape[0]
    qi = pl.program_id(1)
    n_hc = 2 * HEADS
    lane = lax.broadcasted_iota(jnp.int32, (1, LANES), 1)
    for hc in range(n_hc):
        h, c = divmod(hc, 2)
        pair = h // 2
        lo = (h % 2) * HEAD_DIM + c * DIFF_DIM
        qp = q_ref[:, pair * LANES:(pair + 1) * LANES]
        qm_scr[hc] = jnp.where((lane >= lo) & (lane < lo + DIFF_DIM), qp, jnp.zeros_like(qp))
    m_scr[...] = jnp.full(m_scr.shape, NEG_INF, F32)
    acc_scr[...] = jnp.zeros_like(acc_scr)

    def sublane_allmax(x):
        for shift in (4, 2, 1):
            x = jnp.maximum(x, pltpu.roll(x, shift, 0))
        return x

    def kv_block(j, keep_t):
        key0 = pl.multiple_of(j * tq, tq)
        for hc in range(n_hc):
            pair = (hc // 2) // 2
            kb = k_ref[pl.ds(key0, tq), pair * LANES:(pair + 1) * LANES]
            st_scr[hc] = lax.dot_general(kb, qm_scr[hc], NT_DIMS, preferred_element_type=F32)
        for hc in range(n_hc):
            h = hc // 2
            vt = vt_ref[h * LANES:(h + 1) * LANES, pl.ds(key0, tq)]
            st = st_scr[hc]
            if keep_t is not None:
                st = jnp.where(keep_t, st, NEG_INF)
            st3 = st.reshape(tq // SUBLANES, SUBLANES, tq)
            m_old = m_scr[hc]
            m_new = jnp.maximum(m_old, sublane_allmax(jnp.max(st3, axis=0)))
            alpha = jnp.exp2(m_old - m_new)
            p = jnp.exp2(st3 - m_new[None]).reshape(tq, tq).astype(BF16)
            acc = acc_scr[hc].reshape(LANES // SUBLANES, SUBLANES, tq) * alpha[None]
            acc_scr[hc] = acc.reshape(LANES, tq) + jnp.dot(vt, p, preferred_element_type=F32)
            m_scr[hc] = m_new

    def full_block(j, carry):
        kv_block(j, None)
        return carry

    lax.fori_loop(0, qi, full_block, 0)
    key_idx = lax.broadcasted_iota(jnp.int32, (tq, tq), 0)
    query_idx = lax.broadcasted_iota(jnp.int32, (tq, tq), 1)
    kv_block(qi, key_idx <= query_idx)

    lamp = lamp_ref[...]
    lam = (jnp.exp(jnp.sum(lamp[0:1, :] * lamp[1:2, :], axis=-1, keepdims=True))
           - jnp.exp(jnp.sum(lamp[2:3, :] * lamp[3:4, :], axis=-1, keepdims=True)) + lam_init)
    gain = g_ref[...] * (1.0 - lam_init)
    outs = []
    for h in range(HEADS):
        a1 = acc_scr[2 * h]
        a2 = acc_scr[2 * h + 1]
        o = a1[:HEAD_DIM] / a1[HEAD_DIM:HEAD_DIM + 1] - lam * (a2[:HEAD_DIM] / a2[HEAD_DIM:HEAD_DIM + 1])
        on = o * lax.rsqrt(jnp.mean(o * o, axis=0, keepdims=True) + EPS)
        outs.append(on.T * gain)
    o_ref[...] = jnp.concatenate(outs, axis=1).astype(BF16)


def _attn_call(zq, zk, zvt, lamp, gain, lam_init, bsz, seq):
    tq = TQ_ATTN
    nq = seq // tq
    return pl.pallas_call(
        functools.partial(_attn_kernel, lam_init),
        grid=(bsz, nq),
        in_specs=[pl.BlockSpec((tq, ZQ), lambda b, i: (b * nq + i, 0)),
                  pl.BlockSpec((seq, ZK), lambda b, i: (b, 0)),
                  pl.BlockSpec((ZV, seq), lambda b, i: (0, b)),
                  pl.BlockSpec((4, LANES), lambda b, i: (0, 0)),
                  pl.BlockSpec((1, HEAD_DIM), lambda b, i: (0, 0))],
        out_specs=pl.BlockSpec((tq, GW), lambda b, i: (b * nq + i, 0)),
        out_shape=jax.ShapeDtypeStruct((bsz * seq, GW), BF16),
        scratch_shapes=[pltpu.VMEM((2 * HEADS, tq, LANES), BF16),
                        pltpu.VMEM((2 * HEADS, SUBLANES, tq), F32),
                        pltpu.VMEM((2 * HEADS, LANES, tq), F32),
                        pltpu.VMEM((2 * HEADS, tq, tq), F32)],
        compiler_params=_cparams(("parallel", "arbitrary")),
        name="diff_attn",
    )(zq, zk, zvt, lamp, gain)


def _outproj_kernel(tiles_per_seq, ya_ref, zc_ref, halo_ref, yc_ref, yd_ref, x_ref, mod_ref, wo_ref, cw_ref,
                    cb_ref, lng_ref, lnb_ref, rw_ref, rb_ref, x1_ref, u2_ref, rt_ref, rtc_ref, cnt_ref):
    tm = x_ref.shape[0]
    mod = mod_ref[0]
    zc = zc_ref[...].astype(F32)
    halo = halo_ref[...].astype(F32)
    first = (pl.program_id(0) % tiles_per_seq) == 0
    s_cur = zc[:, GW:2 * GW] * zc[:, 2 * GW:3 * GW]
    s_halo = jnp.where(first, 0.0, halo[:, GW:2 * GW] * halo[:, 2 * GW:3 * GW])
    s_ext = jnp.concatenate([s_halo, s_cur], axis=0)
    hw = halo.shape[0]
    cw = cw_ref[...]
    conv = cb_ref[...]
    for tap in range(CONV_WIDTH):
        start = hw - (CONV_WIDTH - 1) + tap
        conv = conv + cw[tap:tap + 1, :] * s_ext[start:start + tm, :]
    yb = (zc[:, 0:GW] * conv).astype(BF16)

    y = jnp.dot(jnp.concatenate([ya_ref[...], yb, yc_ref[...], yd_ref[...]], axis=1), wo_ref[...],
                preferred_element_type=F32)
    x1 = _layer_norm(ALPHA * x_ref[...] + (1.0 + mod[2:3, :]) * y, lng_ref[...], lnb_ref[...])
    x1_ref[...] = x1
    u2 = (x1 * (1.0 + mod[4:5, :]) + mod[3:4, :]).astype(BF16)
    u2_ref[...] = u2

    logits = lax.dot_general(rw_ref[...], u2, (((1,), (1,)), ((), ())), preferred_element_type=F32)
    scores = _sigmoid(logits)
    biased = scores + rb_ref[...]
    rows = [biased[e:e + 1, :] for e in range(N_EXPERTS)]
    srow = [scores[e:e + 1, :] for e in range(N_EXPERTS)]
    gscore = []
    for g in range(N_GROUPS):
        a = rows[g * GROUP_SIZE:(g + 1) * GROUP_SIZE]
        best = None
        for i in range(GROUP_SIZE):
            for j in range(i + 1, GROUP_SIZE):
                pair_sum = a[i] + a[j]
                best = pair_sum if best is None else jnp.maximum(best, pair_sum)
        gscore.append(best)
    gmax = functools.reduce(jnp.maximum, gscore)
    gsel = jnp.full(gmax.shape, N_GROUPS, jnp.int32)
    for g in reversed(range(N_GROUPS)):
        gsel = jnp.where(gscore[g] == gmax, g, gsel)
    masked = [jnp.where(gsel == (e // GROUP_SIZE), rows[e], NEG_INF) for e in range(N_EXPERTS)]
    m1 = functools.reduce(jnp.maximum, masked)
    i1 = jnp.full(m1.shape, N_EXPERTS, jnp.int32)
    for e in reversed(range(N_EXPERTS)):
        i1 = jnp.where(masked[e] == m1, e, i1)
    masked2 = [jnp.where(i1 == e, NEG_INF, masked[e]) for e in range(N_EXPERTS)]
    m2 = functools.reduce(jnp.maximum, masked2)
    i2 = jnp.full(m2.shape, N_EXPERTS, jnp.int32)
    for e in reversed(range(N_EXPERTS)):
        i2 = jnp.where((masked2[e] == m2) & (i1 != e), e, i2)
    sel1 = functools.reduce(lambda acc, e: acc + jnp.where(i1 == e, srow[e], 0.0), range(N_EXPERTS), 0.0)
    sel2 = functools.reduce(lambda acc, e: acc + jnp.where(i2 == e, srow[e], 0.0), range(N_EXPERTS), 0.0)
    tot = sel1 + sel2
    g1 = sel1 / tot
    g2 = sel2 / tot
    route = jnp.concatenate([i1.astype(F32), i2.astype(F32), g1, g2, jnp.zeros((4, tm), F32)], axis=0)
    rt_ref[0] = route
    rtc_ref[...] = jnp.concatenate([route, jnp.zeros((LANES - 8, tm), F32)], axis=0).T
    sub = lax.broadcasted_iota(jnp.int32, (N_EXPERTS, tm), 0)
    onehot = jnp.where((i1 == sub) | (i2 == sub), 1.0, 0.0)
    lane = lax.broadcasted_iota(jnp.int32, (1, LANES), 1)
    cnt = jnp.zeros((N_EXPERTS, LANES), F32)
    for j in range(tm // DISPATCH_BLOCK):
        cj = jnp.sum(onehot[:, j * DISPATCH_BLOCK:(j + 1) * DISPATCH_BLOCK], axis=-1, keepdims=True)
        cnt = jnp.where(lane == j, cj, cnt)
    cnt_ref[0] = cnt


def _outproj_call(ya, zc, yc, yd, x2, mod_l, wo, cw, cb, lng, lnb, rw, rb, seq):
    n, d = x2.shape
    tm = TM_PROJ
    tps = seq // tm
    hw = 16
    row = lambda w: pl.BlockSpec((tm, w), lambda i: (i, 0))
    full = lambda a: pl.BlockSpec(a.shape, lambda i: (0,) * a.ndim)
    return pl.pallas_call(
        functools.partial(_outproj_kernel, tps),
        grid=(n // tm,),
        in_specs=[row(GW), row(ZC),
                  pl.BlockSpec((hw, ZC), lambda i: (jnp.maximum(i * (tm // hw) - 1, 0), 0)),
                  row(GW), row(GW), row(d),
                  pl.BlockSpec((1, 6, d), lambda i: (i // tps, 0, 0)),
                  full(wo), full(cw), full(cb), full(lng), full(lnb), full(rw), full(rb)],
        out_specs=[row(d), row(d), pl.BlockSpec((1, 8, tm), lambda i: (i, 0, 0)), row(LANES),
                   pl.BlockSpec((1, N_EXPERTS, LANES), lambda i: (i, 0, 0))],
        out_shape=[jax.ShapeDtypeStruct((n, d), F32), jax.ShapeDtypeStruct((n, d), BF16),
                   jax.ShapeDtypeStruct((n // tm, 8, tm), F32), jax.ShapeDtypeStruct((n, LANES), F32),
                   jax.ShapeDtypeStruct((n // tm, N_EXPERTS, LANES), F32)],
        compiler_params=_cparams(("parallel",)),
        name="out_proj_router",
    )(ya, zc, zc, yc, yd, x2, mod_l, wo, cw, cb, lng, lnb, rw, rb)


def _pack_bf16_pairs(x):
    half = x.shape[1] // 2
    xb = x.astype(BF16).astype(F32)
    hi = lax.bitcast_convert_type(xb[:, :half], jnp.uint32)
    lo = lax.bitcast_convert_type(xb[:, half:], jnp.uint32)
    return (hi & jnp.uint32(0xFFFF0000)) | (lo >> 16)


def _unpack_bf16_pairs(w):
    hi = lax.bitcast_convert_type(w & jnp.uint32(0xFFFF0000), F32).astype(BF16)
    lo = lax.bitcast_convert_type(w << 16, F32).astype(BF16)
    return jnp.concatenate([hi, lo], axis=1)


def _segment_pieces(count, fn):
    for b in reversed(range(SEG_PIECE_BITS)):
        size = SEG_ALIGN << b
        shift = SEG_ALIGN.bit_length() - 1 + b
        hi = lax.shift_left(lax.shift_right_logical(count, shift + 1), shift + 1)

        @pl.when((lax.shift_right_logical(count, shift) & 1) == 1)
        def _():
            fn(hi, size)


def _slot_positions(onehot_bf16, strict_lower_bf16, transposed):
    if transposed:
        return jnp.dot(strict_lower_bf16, onehot_bf16, preferred_element_type=F32)
    return lax.dot_general(onehot_bf16, strict_lower_bf16, (((1,), (1,)), ((), ())), preferred_element_type=F32)


def _dispatch_kernel(off_s, pc_s, gb_s, rows_s, gs_s, gl_s, tail_s, u_ref, rt_ref, xs_ref, slab, zbuf, sem):
    blk = pl.program_id(0)
    tb = u_ref.shape[0]
    n_slots = slab.shape[1]
    e1 = rt_ref[0, 0:1, :]
    e2 = rt_ref[0, 1:2, :]
    sub_e = lax.broadcasted_iota(jnp.int32, (N_EXPERTS, tb), 0).astype(F32)
    hit1 = e1 == sub_e
    hit2 = e2 == sub_e
    onehot = jnp.where(hit1 | hit2, 1.0, 0.0).astype(BF16)
    r = lax.broadcasted_iota(jnp.int32, (tb, tb), 0)
    c = lax.broadcasted_iota(jnp.int32, (tb, tb), 1)
    lower = jnp.where(c < r, 1.0, 0.0).astype(BF16)
    rank = _slot_positions(onehot, lower, transposed=False)
    sub1 = lax.broadcasted_iota(jnp.int32, (N_EXPERTS, 1), 0)
    offv = jnp.zeros((N_EXPERTS, 1), F32)
    for e in range(N_EXPERTS):
        offv = jnp.where(sub1 == e, off_s[blk * N_EXPERTS + e].astype(F32), offv)
    pos = offv + rank
    pos1 = jnp.sum(jnp.where(hit1, pos, 0.0), axis=0, keepdims=True)
    pos2 = jnp.sum(jnp.where(hit2, pos, 0.0), axis=0, keepdims=True)
    slot = lax.broadcasted_iota(jnp.int32, (n_slots, tb), 0).astype(F32)
    perm = jnp.where((slot == pos1) | (slot == pos2), 1.0, 0.0).astype(BF16)
    cur = blk % 2
    slab[cur] = _pack_bf16_pairs(jnp.dot(perm, u_ref[...], preferred_element_type=F32))

    @pl.when(blk == 0)
    def _():
        zbuf[...] = jnp.zeros_like(zbuf)

    def start_segments(b, slot):
        for e in range(N_EXPERTS):
            idx = b * N_EXPERTS + e
            src0, dst0 = off_s[idx], gb_s[idx]

            def piece(hi, size, src0=src0, dst0=dst0, prio=e % 2):
                pltpu.make_async_copy(slab.at[slot, pl.ds(pl.multiple_of(src0 + hi, SEG_ALIGN), size)],
                                      xs_ref.at[pl.ds(pl.multiple_of(dst0 + hi, SEG_ALIGN), size)],
                                      sem.at[slot]).start(priority=prio)
            _segment_pieces(pc_s[idx], piece)

    def wait_segments(b, slot):
        def wait_rows(n_static):
            pltpu.make_async_copy(slab.at[slot, pl.ds(0, n_static)], xs_ref.at[pl.ds(0, n_static)],
                                  sem.at[slot]).wait()
        wait_rows(2 * DISPATCH_BLOCK)
        _segment_pieces(rows_s[b] - 2 * DISPATCH_BLOCK, lambda hi, size: wait_rows(size))

    def fill_copies(slot, go):
        for e in range(N_EXPERTS):
            dst0 = gs_s[e]

            def piece(hi, size, dst0=dst0):
                go(pltpu.make_async_copy(zbuf.at[pl.ds(0, size)],
                                         xs_ref.at[pl.ds(pl.multiple_of(dst0 + hi, SEG_ALIGN), size)], sem.at[slot]))
            _segment_pieces(gl_s[e], piece)

        zrows = zbuf.shape[0]

        def tail_chunk(i, carry):
            go(pltpu.make_async_copy(zbuf, xs_ref.at[pl.ds(pl.multiple_of(tail_s[0] + i * zrows, zrows), zrows)],
                                     sem.at[slot]))
            return carry
        lax.fori_loop(0, tail_s[1], tail_chunk, 0)

    last = pl.num_programs(0) - 1
    start_segments(blk, cur)

    @pl.when(blk == 0)
    def _():
        fill_copies(cur, lambda cp: cp.start())

    @pl.when(blk > 0)
    def _():
        wait_segments(blk - 1, 1 - cur)

    @pl.when(blk == jnp.minimum(1, last))
    def _():
        fill_copies(0, lambda cp: cp.wait())

    @pl.when(blk == last)
    def _():
        wait_segments(blk, cur)


def _dispatch_call(meta, u2, rt, n_rows):
    n, d = u2.shape
    tb = DISPATCH_BLOCK
    per_tile = rt.shape[2] // tb
    grid_spec = pltpu.PrefetchScalarGridSpec(
        num_scalar_prefetch=7,
        grid=(n // tb,),
        in_specs=[pl.BlockSpec((tb, d), lambda i, *_: (i, 0)),
                  pl.BlockSpec((1, 8, tb), lambda i, *_: (i // per_tile, 0, i % per_tile))],
        out_specs=pl.BlockSpec(memory_space=pl.ANY),
        scratch_shapes=[pltpu.VMEM((2, DISPATCH_SLOTS, d // 2), jnp.uint32),
                        pltpu.VMEM((SEG_ALIGN << (SEG_PIECE_BITS - 1), d // 2), jnp.uint32),
                        pltpu.SemaphoreType.DMA((2,))],
    )
    return pl.pallas_call(
        _dispatch_kernel,
        grid_spec=grid_spec,
        out_shape=jax.ShapeDtypeStruct((n_rows, d // 2), jnp.uint32),
        compiler_params=_cparams(("arbitrary",)),
        name="moe_dispatch",
    )(meta["off"], meta["pc"], meta["gbase"], meta["rows"], meta["gap_start"], meta["gap_len"], meta["tail"], u2, rt)


def _experts_kernel(te_s, na_s, x_ref, wg_ref, wu_ref, wd_ref, y_ref, wg_bf, wu_bf, wd_bf):
    i = pl.program_id(0)
    active = i < na_s[0]

    @pl.when(active & ((i == 0) | (te_s[i] != te_s[jnp.maximum(i - 1, 0)])))
    def _():
        wg_bf[...] = wg_ref[0, 0].astype(BF16)
        wu_bf[...] = wu_ref[0, 0].astype(BF16)
        wd_bf[...] = wd_ref[0, 0].astype(BF16)

    @pl.when(active)
    def _():
        x = _unpack_bf16_pairs(x_ref[...])
        gate = jnp.dot(x, wg_bf[...], preferred_element_type=F32)
        up = jnp.dot(x, wu_bf[...], preferred_element_type=F32)
        hidden = (gate * _sigmoid(gate) * up).astype(BF16)
        y_ref[...] = _pack_bf16_pairs(jnp.dot(hidden, wd_bf[...], preferred_element_type=F32))

    @pl.when(jnp.logical_not(active))
    def _():
        y_ref[...] = jnp.zeros_like(y_ref)


def _experts_call(meta, xs, layer, wg, wu, wd):
    n_rows = xs.shape[0]
    tm = TM_MOE
    _, _, d, de = wg.shape
    tile = lambda i, te, na: jnp.minimum(i, na[0] - 1)
    grid_spec = pltpu.PrefetchScalarGridSpec(
        num_scalar_prefetch=2,
        grid=(n_rows // tm,),
        in_specs=[pl.BlockSpec((tm, d // 2), lambda i, te, na: (tile(i, te, na), 0)),
                  pl.BlockSpec((1, 1, d, de), lambda i, te, na: (layer, te[tile(i, te, na)], 0, 0)),
                  pl.BlockSpec((1, 1, d, de), lambda i, te, na: (layer, te[tile(i, te, na)], 0, 0)),
                  pl.BlockSpec((1, 1, de, d), lambda i, te, na: (layer, te[tile(i, te, na)], 0, 0))],
        out_specs=pl.BlockSpec((tm, d // 2), lambda i, te, na: (i, 0)),
        scratch_shapes=[pltpu.VMEM((d, de), BF16), pltpu.VMEM((d, de), BF16), pltpu.VMEM((de, d), BF16)],
    )
    return pl.pallas_call(
        _experts_kernel,
        grid_spec=grid_spec,
        out_shape=jax.ShapeDtypeStruct((n_rows, d // 2), jnp.uint32),
        compiler_params=_cparams(("arbitrary",)),
        name="experts",
    )(meta["tile_expert"], meta["n_active"], xs, wg, wu, wd)


def _combine_kernel(off_s, pc_s, gb_s, rows_s, rtc_ref, x1_ref, mod_ref, lng_ref, lnb_ref, ys_ref, o_ref, slab, sem):
    blk = pl.program_id(0)
    tb = x1_ref.shape[0]
    n_slots = slab.shape[1]
    cur = blk % 2

    def start_segments(b, slot):
        for e in range(N_EXPERTS):
            idx = b * N_EXPERTS + e
            dst0, src0 = off_s[idx], gb_s[idx]

            def piece(hi, size, src0=src0, dst0=dst0, prio=e % 2):
                pltpu.make_async_copy(ys_ref.at[pl.ds(pl.multiple_of(src0 + hi, SEG_ALIGN), size)],
                                      slab.at[slot, pl.ds(pl.multiple_of(dst0 + hi, SEG_ALIGN), size)],
                                      sem.at[slot]).start(priority=prio)
            _segment_pieces(pc_s[idx], piece)

    def wait_segments(b, slot):
        def wait_rows(n_static):
            pltpu.make_async_copy(ys_ref.at[pl.ds(0, n_static)], slab.at[slot, pl.ds(0, n_static)],
                                  sem.at[slot]).wait()
        wait_rows(2 * DISPATCH_BLOCK)
        _segment_pieces(rows_s[b] - 2 * DISPATCH_BLOCK, lambda hi, size: wait_rows(size))

    @pl.when(blk == 0)
    def _():
        slab[...] = jnp.zeros_like(slab)
        start_segments(blk, cur)

    @pl.when(blk + 1 < pl.num_programs(0))
    def _():
        start_segments(blk + 1, 1 - cur)

    rtc = rtc_ref[...]
    e1, e2, g1, g2 = rtc[:, 0:1], rtc[:, 1:2], rtc[:, 2:3], rtc[:, 3:4]
    lane_e = lax.broadcasted_iota(jnp.int32, (1, LANES), 1).astype(F32)
    hit1 = e1 == lane_e
    hit2 = e2 == lane_e
    onehot = jnp.where(hit1 | hit2, 1.0, 0.0).astype(BF16)
    r = lax.broadcasted_iota(jnp.int32, (tb, tb), 0)
    c = lax.broadcasted_iota(jnp.int32, (tb, tb), 1)
    lower = jnp.where(c < r, 1.0, 0.0).astype(BF16)
    rank = _slot_positions(onehot, lower, transposed=True)
    lane1 = lax.broadcasted_iota(jnp.int32, (1, LANES), 1)
    offv = jnp.zeros((1, LANES), F32)
    for e in range(N_EXPERTS):
        offv = jnp.where(lane1 == e, off_s[blk * N_EXPERTS + e].astype(F32), offv)
    pos = offv + rank
    pos1 = jnp.sum(jnp.where(hit1, pos, 0.0), axis=-1, keepdims=True)
    pos2 = jnp.sum(jnp.where(hit2, pos, 0.0), axis=-1, keepdims=True)
    slot = lax.broadcasted_iota(jnp.int32, (1, n_slots), 1).astype(F32)
    sel1 = jnp.where(slot == pos1, 1.0, 0.0).astype(BF16)
    sel2 = jnp.where(slot == pos2, 1.0, 0.0).astype(BF16)

    wait_segments(blk, cur)
    ys = _unpack_bf16_pairs(slab[cur])
    y = (g1 * jnp.dot(sel1, ys, preferred_element_type=F32) + g2 * jnp.dot(sel2, ys, preferred_element_type=F32))
    mod = mod_ref[0]
    o_ref[...] = _layer_norm(ALPHA * x1_ref[...] + (1.0 + mod[5:6, :]) * y, lng_ref[...], lnb_ref[...])


def _combine_call(meta, ys, rtc, x1, mod_l, lng, lnb, seq):
    n, d = x1.shape
    tb = DISPATCH_BLOCK
    bps = seq // tb
    grid_spec = pltpu.PrefetchScalarGridSpec(
        num_scalar_prefetch=4,
        grid=(n // tb,),
        in_specs=[pl.BlockSpec((tb, LANES), lambda i, *_: (i, 0)),
                  pl.BlockSpec((tb, d), lambda i, *_: (i, 0)),
                  pl.BlockSpec((1, 6, d), lambda i, *_: (i // bps, 0, 0)),
                  pl.BlockSpec((1, d), lambda i, *_: (0, 0)),
                  pl.BlockSpec((1, d), lambda i, *_: (0, 0)),
                  pl.BlockSpec(memory_space=pl.ANY)],
        out_specs=pl.BlockSpec((tb, d), lambda i, *_: (i, 0)),
        scratch_shapes=[pltpu.VMEM((2, DISPATCH_SLOTS, d // 2), jnp.uint32), pltpu.SemaphoreType.DMA((2,))],
    )
    return pl.pallas_call(
        _combine_kernel,
        grid_spec=grid_spec,
        out_shape=jax.ShapeDtypeStruct((n, d), F32),
        compiler_params=_cparams(("arbitrary",)),
        name="moe_combine_ln2",
    )(meta["off"], meta["pc"], meta["gbase"], meta["rows"], rtc, x1, mod_l, lng, lnb, ys)


def _sorted_rows_bound(n_tokens):
    n_blocks = n_tokens // DISPATCH_BLOCK
    worst = 2 * n_tokens + n_blocks * N_EXPERTS * (SEG_ALIGN - 1) + N_EXPERTS * (TM_MOE - 1)
    return -(-worst // TM_MOE) * TM_MOE


def _route_meta(cnt, n_rows):
    pc = (cnt + SEG_ALIGN - 1) // SEG_ALIGN * SEG_ALIGN
    off = jnp.cumsum(pc, axis=1) - pc
    tot = pc.sum(axis=0)
    reg = (tot + TM_MOE - 1) // TM_MOE * TM_MOE
    reg_end = jnp.cumsum(reg)
    ebase = reg_end - reg
    gbase = ebase[None, :] + jnp.cumsum(pc, axis=0) - pc
    tiles = jnp.arange(n_rows // TM_MOE, dtype=jnp.int32)
    tile_expert = jnp.minimum(jnp.sum(tiles[:, None] >= (reg_end // TM_MOE)[None, :], axis=1), N_EXPERTS - 1)
    i32 = lambda a: a.astype(jnp.int32).reshape(-1)
    used = reg_end[-1:]
    tail_chunk = SEG_ALIGN << (SEG_PIECE_BITS - 1)
    return {"off": i32(off), "pc": i32(pc), "gbase": i32(gbase), "rows": i32(pc.sum(axis=1)),
            "gap_start": i32(ebase + tot),
            "gap_len": i32(reg - tot), "tail": i32(jnp.concatenate([used, (n_rows - used) // tail_chunk])),
            "tile_expert": i32(tile_expert), "n_active": i32(used // TM_MOE)}


def _rearranged_w_in(w_in_l):
    d = w_in_l.shape[0]
    o_gate, o_cb, o_ga = ZM, ZM + 2 * HEADS, ZM + 2 * HEADS + ZC + ZQ + ZK + ZVP + ZG
    cols = [w_in_l[:, 0:ZM], w_in_l[:, o_cb:o_ga],
            w_in_l[:, o_gate:o_gate + 2 * HEADS], w_in_l[:, o_ga:o_ga + GLA_RANK],
            jnp.zeros((d, ZS - 2 * HEADS - GLA_RANK), w_in_l.dtype)]
    return jnp.concatenate(cols, axis=1).astype(BF16)


def _rope_tables(positions):
    inv_freq = ROPE_THETA ** (-jnp.arange(0, ROPE_DIM, 2, dtype=F32) / ROPE_DIM)
    ang = positions.astype(F32).reshape(-1, 1) * inv_freq
    cos, sin = jnp.cos(ang), jnp.sin(ang)
    d = np.arange(ZQ) % DIFF_DIM
    half = ROPE_DIM // 2
    idx = d % half
    cosf = jnp.where(d < ROPE_DIM, cos[:, idx], 1.0)
    sinf = jnp.where(d < half, -sin[:, idx], jnp.where(d < ROPE_DIM, sin[:, idx], 0.0))
    return cosf, sinf


def _pad_lanes(v, width=LANES):
    v = v.reshape(1, -1)
    return jnp.pad(v, ((0, 0), (0, width - v.shape[1])))


def kernel(x, c, positions, w_ada, b_ada, w_in, w_out, mlstm_b_i, mlstm_b_f, mlstm_norm_g, conv_w, conv_b, diff_lambda_q1, diff_lambda_k1, diff_lambda_q2, diff_lambda_k2, diff_norm_g, gla_w_a2, gla_b_a, gla_norm_g, ln1_g, ln1_b, ln2_g, ln2_b, router_w, router_bias, exp_w_gate, exp_w_up, exp_w_down):
    bsz, seq, d = x.shape
    depth = w_in.shape[0]
    n = bsz * seq
    x2 = x.reshape(n, d)
    mod = _mod_call(c, w_ada, b_ada)
    cosf, sinf = _rope_tables(positions)
    rw_t = router_w.T.astype(BF16)
    rb = router_bias.reshape(N_EXPERTS, 1).astype(F32)
    n_sorted = _sorted_rows_bound(n)
    for l in range(depth):
        lam_init = 0.8 - 0.6 * math.exp(-0.3 * l)
        w_l = _rearranged_w_in(w_in[l])
        zm, zc, zq, zk, zvt, zg, zs = _inproj_call(x2, mod[l], w_l, cosf, sinf, seq)
        bif = _pad_lanes(jnp.concatenate([mlstm_b_i[l], mlstm_b_f[l]]))
        ya = _mlstm_call(zm, zs, bif, mlstm_norm_g[l].reshape(1, -1), bsz, seq)
        lamp = jnp.concatenate([_pad_lanes(diff_lambda_q1[l]), _pad_lanes(diff_lambda_k1[l]),
                                _pad_lanes(diff_lambda_q2[l]), _pad_lanes(diff_lambda_k2[l])], axis=0)
        yc = _attn_call(zq, zk, zvt, lamp, diff_norm_g[l].reshape(1, -1), lam_init, bsz, seq)
        wa = jnp.pad(gla_w_a2[l], ((2 * HEADS, ZS - 2 * HEADS - GLA_RANK), (0, 0))).astype(BF16)
        yd = _gla_call(zg, zs, wa, gla_b_a[l].reshape(1, -1), gla_norm_g[l].reshape(1, -1), bsz, seq)
        x1, u2, rt, rtc, cnt = _outproj_call(ya, zc, yc, yd, x2, mod[l], w_out[l].astype(BF16), conv_w[l],
                                             conv_b[l].reshape(1, -1), ln1_g[l].reshape(1, -1),
                                             ln1_b[l].reshape(1, -1), rw_t, rb, seq)
        per_tile = TM_PROJ // DISPATCH_BLOCK
        cnt_blk = cnt[:, :, :per_tile].transpose(0, 2, 1).reshape(-1, N_EXPERTS).astype(jnp.int32)
        meta = _route_meta(cnt_blk, n_sorted)
        xs = _dispatch_call(meta, u2, rt, n_sorted)
        ys = _experts_call(meta, xs, l, exp_w_gate, exp_w_up, exp_w_down)
        x2 = _combine_call(meta, ys, rtc, x1, mod[l], ln2_g[l].reshape(1, -1), ln2_b[l].reshape(1, -1), seq)
    return x2.reshape(bsz, seq, d)
```
